```python
import math
import jax
import jax.numpy as jnp
from jax import lax
import numpy as np


D_MODEL = 2048
BATCH = 8
SEQ = 2048
DEPTH = 2

HEAD_DIM = 128
MIX_WIDTH = D_MODEL
A_WIDTH = MIX_WIDTH // 2
A_GROUPS = A_WIDTH // HEAD_DIM
B_HEADS = (MIX_WIDTH - A_WIDTH) // HEAD_DIM
B_WIDTH = B_HEADS * HEAD_DIM
C_WIDTH = MIX_WIDTH // 2
D_HEADS = (MIX_WIDTH - C_WIDTH) // HEAD_DIM
D_WIDTH = D_HEADS * HEAD_DIM
IN_AB = 2 * A_WIDTH + 3 * B_WIDTH
IN_CD = 3 * C_WIDTH + 3 * D_WIDTH
CHUNK = 128
GRID_W = 64
NA_ROWS = 8
NA_COLS = 16
NA_QBLK = 16
NA_BAND = 32
CONV_W = 3
DIL_PATTERNS = ((128, 1), (512, 4), (2048, 16))
DIL_BLOCK = 64
N_EXPERTS = 16
EC_CAPACITY_FACTOR = 2
D_EXPERT = 2048
N_EVEN = (DEPTH + 1) // 2
N_ODD = DEPTH // 2
RMS_EPS = 1e-6
LN_EPS = 1e-5
NEG_INF = -1e30

kernel_name = "hybrid_sgu_natten_shortconv_dilated_ec_moe"


def rms_norm(x, g):
    xf = x.astype(jnp.float32)
    y = xf * lax.rsqrt(jnp.mean(xf * xf, axis=-1, keepdims=True) + RMS_EPS)
    return (y * g.astype(jnp.float32)).astype(x.dtype)


def alibi_slopes(n):
    return np.array([2.0 ** (-8.0 * (h + 1) / n) for h in range(n)], dtype=np.float32)


def spatial_gating_unit(u_raw, v_raw, ln_g, w_s, b_s):
    bsz, seq, _ = u_raw.shape
    u = jax.nn.gelu(u_raw)
    vf = jax.nn.gelu(v_raw).astype(jnp.float32)
    mu = jnp.mean(vf, axis=-1, keepdims=True)
    var = jnp.mean(jnp.square(vf - mu), axis=-1, keepdims=True)
    v = ((vf - mu) * lax.rsqrt(var + LN_EPS) * ln_g.astype(jnp.float32)).astype(u_raw.dtype)
    v = v.reshape(bsz, seq // CHUNK, CHUNK, A_GROUPS, HEAD_DIM)
    mixed = jnp.einsum('gts,bnsgc->bntgc', w_s, v) + b_s.T[None, None, :, :, None]
    return u * mixed.reshape(bsz, seq, A_WIDTH)


def neighbourhood_attention(q, k, v, rpb):
    bsz, seq, nh, dh = q.shape
    rows = seq // GRID_W
    kh = min(NA_ROWS, rows)
    scale = dh ** -0.5
    qg = q.reshape(bsz, rows, GRID_W, nh, dh)
    kg = k.reshape(bsz, rows, GRID_W, nh, dh)
    vg = v.reshape(bsz, rows, GRID_W, nh, dh)
    row_start = jnp.asarray(np.clip(np.arange(rows) - kh // 2, 0, rows - kh), jnp.int32)
    col_start = np.clip(np.arange(GRID_W) - NA_COLS // 2, 0, GRID_W - NA_COLS)
    n_cb = GRID_W // NA_QBLK
    band0 = np.clip(np.arange(n_cb) * NA_QBLK - NA_COLS // 2, 0, GRID_W - NA_BAND)
    key_cols = band0[:, None] + np.arange(NA_BAND)
    q_cols = np.arange(n_cb)[:, None] * NA_QBLK + np.arange(NA_QBLK)
    kc = key_cols[:, None, :]
    qc = q_cols[:, :, None]
    col_valid = jnp.asarray((kc >= col_start[qc]) & (kc < col_start[qc] + NA_COLS))
    dc_idx = np.clip(kc - qc + NA_COLS - 1, 0, 2 * NA_COLS - 2)
    rpb_cols = rpb[:, :, dc_idx]

    def row_block(r):
        rs = row_start[r]
        k_rows = lax.dynamic_slice_in_dim(kg, rs, kh, axis=1)
        v_rows = lax.dynamic_slice_in_dim(vg, rs, kh, axis=1)
        k_band = k_rows[:, :, key_cols]
        v_band = v_rows[:, :, key_cols]
        q_row = lax.dynamic_index_in_dim(qg, r, axis=1, keepdims=False).reshape(bsz, n_cb, NA_QBLK, nh, dh)
        s = jnp.einsum('bcihd,bacjhd->bhciaj', q_row, k_band).astype(jnp.float32) * scale
        bias = rpb_cols[:, rs + jnp.arange(kh) - r + NA_ROWS - 1]
        s = s + bias.transpose(0, 2, 3, 1, 4)[None].astype(jnp.float32)
        s = jnp.where(col_valid[None, None, :, :, None, :], s, NEG_INF)
        p = jax.nn.softmax(s, axis=(-2, -1))
        o = jnp.einsum('bhciaj,bacjhd->bcihd', p.astype(v.dtype), v_band)
        return o.reshape(bsz, GRID_W, nh, dh)

    out = lax.map(row_block, jnp.arange(rows, dtype=jnp.int32))
    return out.transpose(1, 0, 2, 3, 4).reshape(bsz, seq, nh * dh)


def dilated_branch(q, k, v, dil, radius, slopes):
    bsz, seq, nh, dh = q.shape
    length = seq // dil
    blk = DIL_BLOCK
    nb = -(-length // blk)
    lp = nb * blk
    n = bsz * dil
    scale = dh ** -0.5

    def to_sub(t):
        return t.reshape(bsz, length, dil, nh, dh).transpose(0, 2, 1, 3, 4).reshape(n, length, nh, dh)

    def band(t):
        tp = jnp.pad(t, ((0, 0), (blk, lp - length + blk), (0, 0), (0, 0))).reshape(n, nb + 2, blk, nh, dh)
        return jnp.concatenate([tp[:, :-2], tp[:, 1:-1], tp[:, 2:]], axis=2)

    qs = jnp.pad(to_sub(q), ((0, 0), (0, lp - length), (0, 0), (0, 0))).reshape(n, nb, blk, nh, dh)
    kb = band(to_sub(k))
    vb = band(to_sub(v))
    s = jnp.einsum('nqihd,nqjhd->nqhij', qs, kb).astype(jnp.float32) * scale
    delta = np.arange(3 * blk)[None, :] - blk - np.arange(blk)[:, None]
    tk = (np.arange(nb)[:, None] - 1) * blk + np.arange(3 * blk)[None, :]
    valid = (np.abs(delta) <= radius)[None] & ((tk >= 0) & (tk < length))[:, None, :]
    penalty = (slopes[:, None, None] * (np.abs(delta) * dil)[None]).astype(np.float32)
    s = s - jnp.asarray(penalty)[None, None]
    s = jnp.where(jnp.asarray(valid)[None, :, None], s, NEG_INF)
    m = jnp.max(s, axis=-1, keepdims=True)
    e = jnp.exp(s - m)
    l = jnp.sum(e, axis=-1, keepdims=True)
    o = jnp.einsum('nqhij,nqjhd->nqihd', (e / l).astype(v.dtype), vb).reshape(n, lp, nh, dh)[:, :length]
    lse = (m + jnp.log(l))[..., 0].transpose(0, 1, 3, 2).reshape(n, lp, nh)[:, :length]
    o = o.reshape(bsz, dil, length, nh, dh).transpose(0, 2, 1, 3, 4).reshape(bsz, seq, nh, dh)
    lse = lse.reshape(bsz, dil, length, nh).transpose(0, 2, 1, 3).reshape(bsz, seq, nh)
    return o, lse


def dilated_attention(q, k, v):
    bsz, seq, nh, dh = q.shape
    slopes = alibi_slopes(nh)
    outs, lses = [], []
    for window, dil in DIL_PATTERNS:
        o, lse = dilated_branch(q, k, v, dil, window // (2 * dil), slopes)
        outs.append(o)
        lses.append(lse)
    alpha = jax.nn.softmax(jnp.stack(lses), axis=0)
    out = jnp.einsum('pbsh,pbshd->bshd', alpha.astype(q.dtype), jnp.stack(outs))
    return out.reshape(bsz, seq, nh * dh)


def gated_short_conv(b_gate, c_gate, xin, taps):
    z = c_gate * xin
    y = lax.conv_general_dilated(z, taps[:, None, :], window_strides=(1,),
                                 padding=((CONV_W // 2, CONV_W // 2),),
                                 dimension_numbers=('NWC', 'WIO', 'NWC'),
                                 feature_group_count=C_WIDTH)
    return b_gate * y


def expert_choice_moe(h, router, w_gate, w_up, w_down):
    bsz, seq, d = h.shape
    cap = EC_CAPACITY_FACTOR * seq // N_EXPERTS
    aff = jax.nn.softmax(jnp.einsum('bsd,de->bse', h, router).astype(jnp.float32), axis=-1)
    gate, idx = lax.top_k(aff.transpose(0, 2, 1), cap)
    xe = jax.vmap(lambda hb, ib: hb[ib])(h, idx)
    hid = jax.nn.silu(jnp.einsum('becd,edf->becf', xe, w_gate)) * jnp.einsum('becd,edf->becf', xe, w_up)
    ye = jnp.einsum('becf,efd->becd', hid, w_down) * gate[..., None].astype(h.dtype)
    return jax.vmap(lambda ib, yb: jnp.zeros((seq, d), yb.dtype).at[ib.reshape(-1)].add(yb.reshape(-1, d)))(idx, ye)


def setup_inputs(seed: int = 0) -> dict:
    key = jax.random.key(seed)
    ks = jax.random.split(key, 17)
    nrm = jax.random.normal
    f32 = jnp.float32
    return {
        'x': nrm(ks[0], (BATCH, SEQ, D_MODEL), f32),
        'norm_mix': 1.0 + 0.05 * nrm(ks[1], (DEPTH, D_MODEL), f32),
        'norm_ffn': 1.0 + 0.05 * nrm(ks[2], (DEPTH, D_MODEL), f32),
        'norm_final': 1.0 + 0.05 * nrm(ks[3], (D_MODEL,), f32),
        'w_in_ab': nrm(ks[4], (N_EVEN, D_MODEL, IN_AB), f32) * D_MODEL ** -0.5,
        'a_v_norm': 1.0 + 0.05 * nrm(ks[5], (N_EVEN, A_WIDTH), f32),
        'a_spatial_w': nrm(ks[6], (N_EVEN, A_GROUPS, CHUNK, CHUNK), f32) * CHUNK ** -0.5,
        'a_spatial_b': 1.0 + 0.05 * nrm(ks[7], (N_EVEN, A_GROUPS, CHUNK), f32),
        'b_rpb': 0.1 * nrm(ks[8], (N_EVEN, B_HEADS, 2 * NA_ROWS - 1, 2 * NA_COLS - 1), f32),
        'w_out_ab': nrm(ks[9], (N_EVEN, MIX_WIDTH, D_MODEL), f32) * MIX_WIDTH ** -0.5,
        'w_in_cd': nrm(ks[10], (N_ODD, D_MODEL, IN_CD), f32) * D_MODEL ** -0.5,
        'c_conv': nrm(ks[11], (N_ODD, CONV_W, C_WIDTH), f32) * CONV_W ** -0.5,
        'w_out_cd': nrm(ks[12], (N_ODD, MIX_WIDTH, D_MODEL), f32) * MIX_WIDTH ** -0.5,
        'router': nrm(ks[13], (DEPTH, D_MODEL, N_EXPERTS), f32) * D_MODEL ** -0.5,
        'w_gate': nrm(ks[14], (DEPTH, N_EXPERTS, D_MODEL, D_EXPERT), f32) * D_MODEL ** -0.5,
        'w_up': nrm(ks[15], (DEPTH, N_EXPERTS, D_MODEL, D_EXPERT), f32) * D_MODEL ** -0.5,
        'w_down': nrm(ks[16], (DEPTH, N_EXPERTS, D_EXPERT, D_MODEL), f32) * D_EXPERT ** -0.5,
    }


def reference(x, norm_mix, norm_ffn, norm_final, w_in_ab, a_v_norm, a_spatial_w, a_spatial_b,
              b_rpb, w_out_ab, w_in_cd, c_conv, w_out_cd, router, w_gate, w_up, w_down):
    bsz, seq, _ = x.shape
    for layer in range(DEPTH):
        h = rms_norm(x, norm_mix[layer])
        i = layer // 2
        if layer % 2 == 0:
            p = h @ w_in_ab[i]
            a_u = p[..., :A_WIDTH]
            a_v = p[..., A_WIDTH:2 * A_WIDTH]
            o = 2 * A_WIDTH
            b_q = p[..., o:o + B_WIDTH].reshape(bsz, seq, B_HEADS, HEAD_DIM)
            b_k = p[..., o + B_WIDTH:o + 2 * B_WIDTH].reshape(bsz, seq, B_HEADS, HEAD_DIM)
            b_v = p[..., o + 2 * B_WIDTH:o + 3 * B_WIDTH].reshape(bsz, seq, B_HEADS, HEAD_DIM)
            y_a = spatial_gating_unit(a_u, a_v, a_v_norm[i], a_spatial_w[i], a_spatial_b[i])
            y_b = neighbourhood_attention(b_q, b_k, b_v, b_rpb[i])
            x = x + jnp.concatenate([y_a, y_b], axis=-1) @ w_out_ab[i]
        else:
            p = h @ w_in_cd[i]
            c_b = p[..., :C_WIDTH]
            c_c = p[..., C_WIDTH:2 * C_WIDTH]
            c_x = p[..., 2 * C_WIDTH:3 * C_WIDTH]
            o = 3 * C_WIDTH
            d_q = p[..., o:o + D_WIDTH].reshape(bsz, seq, D_HEADS, HEAD_DIM)
            d_k = p[..., o + D_WIDTH:o + 2 * D_WIDTH].reshape(bsz, seq, D_HEADS, HEAD_DIM)
            d_v = p[..., o + 2 * D_WIDTH:o + 3 * D_WIDTH].reshape(bsz, seq, D_HEADS, HEAD_DIM)
            y_c = gated_short_conv(c_b, c_c, c_x, c_conv[i])
            y_d = dilated_attention(d_q, d_k, d_v)
            x = x + jnp.concatenate([y_c, y_d], axis=-1) @ w_out_cd[i]
        x = x + expert_choice_moe(rms_norm(x, norm_ffn[layer]), router[layer], w_gate[layer], w_up[layer], w_down[layer])
    return rms_norm(x, norm_final)
```

```python
import functools

import numpy as np
import jax
import jax.numpy as jnp
from jax import lax
from jax.experimental import pallas as pl
from jax.experimental.pallas import tpu as pltpu

F32 = jnp.float32
BF16 = jnp.bfloat16

HEAD_DIM = 128
CHUNK = 128
GRID_W = 64
NA_ROWS = 8
NA_COLS = 16
NA_QROWS = 4
NA_KROWS = NA_QROWS + NA_ROWS
CONV_W = 3
DIL_PATTERNS = ((128, 1), (512, 4), (2048, 16))
DIL_QBLK = 128
N_EXPERTS = 16
EC_CAPACITY_FACTOR = 2
RMS_EPS = 1e-6
LN_EPS = 1e-5
NEG_INF = -1e30
MIB = 1024 * 1024


def _params(semantics, vmem_mib):
    return pltpu.CompilerParams(dimension_semantics=semantics,
                                vmem_limit_bytes=int(vmem_mib * MIB))


def _rms_to_scratch(x_ref, g_ref, h_scr, rows):
    n = x_ref.shape[0] // rows

    def body(i, carry):
        r = pl.multiple_of(i * rows, rows)
        x = x_ref[pl.ds(r, rows), :]
        ms = jnp.mean(x * x, axis=-1, keepdims=True)
        h_scr[pl.ds(r, rows), :] = (x * lax.rsqrt(ms + RMS_EPS) * g_ref[...]).astype(h_scr.dtype)
        return carry

    lax.fori_loop(0, n, body, 0)


def _rms_matmul_kernel(x_ref, g_ref, w_ref, o_ref, h_scr):
    @pl.when(pl.program_id(1) == 0)
    def _():
        _rms_to_scratch(x_ref, g_ref, h_scr, 256)

    o_ref[...] = jnp.dot(h_scr[...], w_ref[...], preferred_element_type=F32).astype(o_ref.dtype)


def rms_matmul(x2d, g, w_bf16, tm=1024, tn=1024):
    t, d = x2d.shape
    n = w_bf16.shape[1]
    return pl.pallas_call(
        _rms_matmul_kernel,
        grid=(t // tm, n // tn),
        in_specs=[pl.BlockSpec((tm, d), lambda i, j: (i, 0)),
                  pl.BlockSpec((1, d), lambda i, j: (0, 0)),
                  pl.BlockSpec((d, tn), lambda i, j: (0, j))],
        out_specs=pl.BlockSpec((tm, tn), lambda i, j: (i, j)),
        out_shape=jax.ShapeDtypeStruct((t, n), BF16),
        scratch_shapes=[pltpu.VMEM((tm, d), BF16)],
        compiler_params=_params(("parallel", "arbitrary"), 48),
        name="rms_in_proj",
    )(x2d, g.reshape(1, d), w_bf16)


def _sgu_kernel(u_ref, v_ref, lng_ref, ws_ref, bias_ref, o_ref):
    tm, width = u_ref.shape
    groups = width // HEAD_DIM
    nchunks = tm // CHUNK
    vf = jax.nn.gelu(v_ref[...].astype(F32))
    mu = jnp.mean(vf, axis=-1, keepdims=True)
    dv = vf - mu
    var = jnp.mean(dv * dv, axis=-1, keepdims=True)
    vn = (dv * lax.rsqrt(var + LN_EPS) * lng_ref[...]).astype(BF16)
    for g in range(groups):
        cols = slice(g * HEAD_DIM, (g + 1) * HEAD_DIM)
        vg = jnp.concatenate([vn[n * CHUNK:(n + 1) * CHUNK, cols] for n in range(nchunks)], axis=1)
        mixed = jnp.dot(ws_ref[g], vg, preferred_element_type=F32)
        for n in range(nchunks):
            rows = slice(n * CHUNK, (n + 1) * CHUNK)
            u = jax.nn.gelu(u_ref[rows, cols].astype(F32))
            o_ref[rows, cols] = (u * (mixed[:, n * CHUNK:(n + 1) * CHUNK] + bias_ref[:, cols])).astype(o_ref.dtype)


def sgu(p2d, ln_g, w_s, b_s, width, tm=512):
    t = p2d.shape[0]
    groups = width // HEAD_DIM
    bias_full = jnp.repeat(b_s.T.astype(F32), HEAD_DIM, axis=1)
    return pl.pallas_call(
        _sgu_kernel,
        grid=(t // tm,),
        in_specs=[pl.BlockSpec((tm, width), lambda i: (i, 0)),
                  pl.BlockSpec((tm, width), lambda i: (i, 1)),
                  pl.BlockSpec((1, width), lambda i: (0, 0)),
                  pl.BlockSpec((groups, CHUNK, CHUNK), lambda i: (0, 0, 0)),
                  pl.BlockSpec((CHUNK, width), lambda i: (0, 0))],
        out_specs=pl.BlockSpec((tm, width), lambda i: (i, 0)),
        out_shape=jax.ShapeDtypeStruct((t, width), BF16),
        compiler_params=_params(("parallel",), 40),
        name="sgu",
    )(p2d, p2d, ln_g.reshape(1, width).astype(F32), w_s.astype(BF16), bias_full)


def _na_block_layout(rows):
    nblk = rows // NA_QROWS
    kh = min(NA_ROWS, rows)
    starts, variants, var_key = [], [], {}
    for qb in range(nblk):
        r0 = qb * NA_QROWS
        ws = int(np.clip(r0 - kh // 2, 0, rows - NA_KROWS))
        rs = np.clip(np.arange(r0, r0 + NA_QROWS) - kh // 2, 0, rows - kh)
        key = (tuple(rs - ws), r0 - ws)
        if key not in var_key:
            var_key[key] = len(var_key)
        starts.append(ws)
        variants.append(var_key[key])
    return starts, variants, list(var_key.keys()), kh


def _na_bias(rpb, rows):
    nh = rpb.shape[0]
    _, _, keys, kh = _na_block_layout(rows)
    qc = np.arange(GRID_W)[:, None]
    kc = np.arange(GRID_W)[None, :]
    col_start = np.clip(qc - NA_COLS // 2, 0, GRID_W - NA_COLS)
    col_valid = (kc >= col_start) & (kc < col_start + NA_COLS)
    dc_idx = np.clip(kc - qc + NA_COLS - 1, 0, 2 * NA_COLS - 2)
    tiles = jnp.where(jnp.asarray(col_valid)[None, None], rpb.astype(F32)[:, :, dc_idx], NEG_INF)
    tiles = jnp.concatenate([tiles, jnp.full((nh, 1, GRID_W, GRID_W), NEG_INF, F32)], axis=1)
    invalid = 2 * NA_ROWS - 1
    dr = np.full((len(keys), NA_QROWS, NA_KROWS), invalid, np.int32)
    for v, (rs_rel, r0_rel) in enumerate(keys):
        for i in range(NA_QROWS):
            for j in range(NA_KROWS):
                if rs_rel[i] <= j < rs_rel[i] + kh:
                    dr[v, i, j] = j - (r0_rel + i) + NA_ROWS - 1
    bias = tiles[:, dr]
    bias = bias.transpose(0, 1, 2, 4, 3, 5)
    return bias.reshape(nh, len(keys), NA_QROWS * GRID_W, NA_KROWS * GRID_W)


def _na_kernel(q_ref, k_ref, v_ref, bias_ref, o_ref, *, starts, variants, scale):
    qblk = NA_QROWS * GRID_W
    kblk = NA_KROWS * GRID_W
    for qb, (ws, var) in enumerate(zip(starts, variants)):
        q = (q_ref[qb * qblk:(qb + 1) * qblk, :].astype(F32) * scale).astype(BF16)
        k = k_ref[ws * GRID_W:ws * GRID_W + kblk, :]
        v = v_ref[ws * GRID_W:ws * GRID_W + kblk, :]
        s = lax.dot_general(q, k, (((1,), (1,)), ((), ())), preferred_element_type=F32)
        s = s + bias_ref[var]
        m = jnp.max(s, axis=-1, keepdims=True)
        e = jnp.exp(s - m)
        l = jnp.sum(e, axis=-1, keepdims=True)
        o = jnp.dot(e.astype(BF16), v, preferred_element_type=F32)
        o_ref[qb * qblk:(qb + 1) * qblk, :] = (o / l).astype(o_ref.dtype)


def neighbourhood_attention(p3, rpb, col0, nh):
    bsz, seq, _ = p3.shape
    rows = seq // GRID_W
    starts, variants, keys, _ = _na_block_layout(rows)
    bias = _na_bias(rpb, rows)
    nvar = len(keys)
    qblk, kblk = NA_QROWS * GRID_W, NA_KROWS * GRID_W
    kern = functools.partial(_na_kernel, starts=starts, variants=variants, scale=HEAD_DIM ** -0.5)
    head_spec = lambda off: pl.BlockSpec((None, seq, HEAD_DIM), lambda h, b: (b, 0, col0 + off + h))
    return pl.pallas_call(
        kern,
        grid=(nh, bsz),
        in_specs=[head_spec(0), head_spec(nh), head_spec(2 * nh),
                  pl.BlockSpec((None, nvar, qblk, kblk), lambda h, b: (h, 0, 0, 0))],
        out_specs=pl.BlockSpec((None, seq, HEAD_DIM), lambda h, b: (b, 0, h)),
        out_shape=jax.ShapeDtypeStruct((bsz, seq, nh * HEAD_DIM), BF16),
        compiler_params=_params(("parallel", "parallel"), 32),
        name="neighbourhood_attention",
    )(p3, p3, p3, bias)


def _conv_kernel(b_ref, c_ref, x_ref, taps_ref, o_ref):
    seq = b_ref.shape[0]
    z = c_ref[...].astype(F32) * x_ref[...].astype(F32)
    row = lax.broadcasted_iota(jnp.int32, z.shape, 0)
    z_prev = jnp.where(row == 0, 0.0, pltpu.roll(z, 1, axis=0))
    z_next = jnp.where(row == seq - 1, 0.0, pltpu.roll(z, seq - 1, axis=0))
    y = taps_ref[0:1, :] * z_prev + taps_ref[1:2, :] * z + taps_ref[2:3, :] * z_next
    o_ref[...] = (b_ref[...].astype(F32) * y).astype(o_ref.dtype)


def gated_short_conv(p3, taps, width, tc=256):
    bsz, seq, _ = p3.shape
    nblk = width // tc
    spec = lambda off: pl.BlockSpec((None, seq, tc), lambda b, j: (b, 0, off * nblk + j))
    return pl.pallas_call(
        _conv_kernel,
        grid=(bsz, nblk),
        in_specs=[spec(0), spec(1), spec(2), pl.BlockSpec((CONV_W, tc), lambda b, j: (0, j))],
        out_specs=pl.BlockSpec((None, seq, tc), lambda b, j: (b, 0, j)),
        out_shape=jax.ShapeDtypeStruct((bsz, seq, width), BF16),
        compiler_params=_params(("parallel", "parallel"), 40),
        name="gated_short_conv",
    )(p3, p3, p3, taps.astype(F32))


def _dil_windows(length):
    nk = min(2 * DIL_QBLK, length)
    out = []
    for i in range(length // DIL_QBLK):
        q0 = i * DIL_QBLK
        k0 = int(np.clip(q0 - DIL_QBLK // 2, 0, length - nk))
        out.append((q0, k0, nk))
    return out


def _dil_tables(seq, nh):
    slopes = np.array([2.0 ** (-8.0 * (h + 1) / nh) for h in range(nh)], dtype=np.float32)
    var_index, absd_tabs, dil_of = {}, [], []
    plan = []
    for window, dil in DIL_PATTERNS:
        radius = window // (2 * dil)
        length = seq // dil
        blocks = []
        for q0, k0, nk in _dil_windows(length):
            key = (dil, radius, k0 - q0, nk)
            if key not in var_index:
                var_index[key] = len(absd_tabs)
                delta = (k0 + np.arange(2 * DIL_QBLK)[None, :]) - (q0 + np.arange(DIL_QBLK)[:, None])
                absd = np.abs(delta).astype(np.float32)
                valid = (np.abs(delta) <= radius) & (np.arange(2 * DIL_QBLK)[None, :] < nk)
                absd_tabs.append((absd * dil, valid))
            blocks.append((q0, k0, nk, var_index[key]))
        plan.append((dil, length, blocks))
    pen = np.stack([a for a, _ in absd_tabs])
    valid = np.stack([v for _, v in absd_tabs])
    bias = np.where(valid[None], -(slopes[:, None, None, None] * pen[None]).astype(np.float32), np.float32(NEG_INF))
    return plan, bias.astype(np.float32)


def _dil_kernel(q_ref, k_ref, v_ref, bias_ref, o_ref, qf, kf, vf, o_scr, l_scr, *, plan, scale):
    qf[...] = q_ref[...].astype(F32) * scale
    kf[...] = k_ref[...].astype(F32)
    vf[...] = v_ref[...].astype(F32)

    def attend(q, k, v, var, nk):
        s = lax.dot_general(q, k, (((1,), (1,)), ((), ())), preferred_element_type=F32)
        s = s + bias_ref[var][:, :nk]
        m = jnp.max(s, axis=-1, keepdims=True)
        e = jnp.exp(s - m)
        l = jnp.sum(e, axis=-1, keepdims=True)
        o = jnp.dot(e.astype(BF16), v, preferred_element_type=F32) / l
        lse = jnp.broadcast_to(m + jnp.log(l), o.shape)
        return o, lse

    for p, (dil, length, blocks) in enumerate(plan):
        if dil == 1:
            for q0, k0, nk, var in blocks:
                o, lse = attend(qf[q0:q0 + DIL_QBLK, :].astype(BF16), kf[k0:k0 + nk, :].astype(BF16),
                                vf[k0:k0 + nk, :].astype(BF16), var, nk)
                o_scr[p, q0:q0 + DIL_QBLK, :] = o
                l_scr[p, q0:q0 + DIL_QBLK, :] = lse
        else:
            def residue(r, carry, dil=dil, length=length, blocks=blocks, p=p):
                qs = qf[pl.ds(r, length, stride=dil), :].astype(BF16)
                ks = kf[pl.ds(r, length, stride=dil), :].astype(BF16)
                vs = vf[pl.ds(r, length, stride=dil), :].astype(BF16)
                for q0, k0, nk, var in blocks:
                    o, lse = attend(qs[q0:q0 + DIL_QBLK], ks[k0:k0 + nk], vs[k0:k0 + nk], var, nk)
                    o_scr[p, pl.ds(q0 * dil + r, DIL_QBLK, stride=dil), :] = o
                    l_scr[p, pl.ds(q0 * dil + r, DIL_QBLK, stride=dil), :] = lse
                return carry

            lax.fori_loop(0, dil, residue, 0)

    npat = len(plan)
    lmax = l_scr[0]
    for p in range(1, npat):
        lmax = jnp.maximum(lmax, l_scr[p])
    num = jnp.zeros_like(lmax)
    den = jnp.zeros_like(lmax)
    for p in range(npat):
        w = jnp.exp(l_scr[p] - lmax)
        num = num + w * o_scr[p]
        den = den + w
    o_ref[...] = (num / den).astype(o_ref.dtype)


def dilated_attention(p3, col0, nh):
    bsz, seq, _ = p3.shape
    plan, bias_np = _dil_tables(seq, nh)
    nvar = bias_np.shape[1]
    kern = functools.partial(_dil_kernel, plan=plan, scale=HEAD_DIM ** -0.5)
    head_spec = lambda off: pl.BlockSpec((None, seq, HEAD_DIM), lambda h, b: (b, 0, col0 + off + h))
    npat = len(plan)
    return pl.pallas_call(
        kern,
        grid=(nh, bsz),
        in_specs=[head_spec(0), head_spec(nh), head_spec(2 * nh),
                  pl.BlockSpec((None, nvar, DIL_QBLK, 2 * DIL_QBLK), lambda h, b: (h, 0, 0, 0))],
        out_specs=pl.BlockSpec((None, seq, HEAD_DIM), lambda h, b: (b, 0, h)),
        out_shape=jax.ShapeDtypeStruct((bsz, seq, nh * HEAD_DIM), BF16),
        scratch_shapes=[pltpu.VMEM((seq, HEAD_DIM), F32)] * 3
                       + [pltpu.VMEM((npat, seq, HEAD_DIM), F32)] * 2,
        compiler_params=_params(("parallel", "parallel"), 32),
        name="dilated_attention",
    )(p3, p3, p3, jnp.asarray(bias_np))


def _out_proj_kernel(ya_ref, yb_ref, wa_ref, wb_ref, x_ref, o_ref):
    acc = jnp.dot(ya_ref[...], wa_ref[...], preferred_element_type=F32)
    acc = acc + jnp.dot(yb_ref[...], wb_ref[...], preferred_element_type=F32)
    o_ref[...] = x_ref[...] + acc


def out_proj_residual(ya, yb, w_bf16, x2d, tm=1024, tn=1024):
    t, d = x2d.shape
    ka = ya.shape[1]
    kb = yb.shape[1]
    assert ka == kb
    return pl.pallas_call(
        _out_proj_kernel,
        grid=(t // tm, d // tn),
        in_specs=[pl.BlockSpec((tm, ka), lambda i, j: (i, 0)),
                  pl.BlockSpec((tm, kb), lambda i, j: (i, 0)),
                  pl.BlockSpec((ka, tn), lambda i, j: (0, j)),
                  pl.BlockSpec((kb, tn), lambda i, j: (1, j)),
                  pl.BlockSpec((tm, tn), lambda i, j: (i, j))],
        out_specs=pl.BlockSpec((tm, tn), lambda i, j: (i, j)),
        out_shape=jax.ShapeDtypeStruct((t, d), F32),
        compiler_params=_params(("parallel", "parallel"), 48),
        name="out_proj_residual",
    )(ya, yb, w_bf16, w_bf16, x2d)


def _router_kernel(x_ref, g_ref, rt_ref, h_ref, aff_ref):
    x = x_ref[...]
    ms = jnp.mean(x * x, axis=-1, keepdims=True)
    h = x * lax.rsqrt(ms + RMS_EPS) * g_ref[...]
    h_ref[...] = h.astype(h_ref.dtype)
    logits = lax.dot_general(rt_ref[...], h, (((1,), (1,)), ((), ())),
                             preferred_element_type=F32, precision=lax.Precision.HIGHEST)
    m = jnp.max(logits, axis=0, keepdims=True)
    e = jnp.exp(logits - m)
    aff_ref[...] = e / jnp.sum(e, axis=0, keepdims=True)


def norm_router(x2d, g, router, tm=256):
    t, d = x2d.shape
    ne = router.shape[1]
    return pl.pallas_call(
        _router_kernel,
        grid=(t // tm,),
        in_specs=[pl.BlockSpec((tm, d), lambda i: (i, 0)),
                  pl.BlockSpec((1, d), lambda i: (0, 0)),
                  pl.BlockSpec((ne, d), lambda i: (0, 0))],
        out_specs=[pl.BlockSpec((tm, d), lambda i: (i, 0)),
                   pl.BlockSpec((ne, tm), lambda i: (0, i))],
        out_shape=[jax.ShapeDtypeStruct((t, d), BF16),
                   jax.ShapeDtypeStruct((ne, t), F32)],
        compiler_params=_params(("parallel",), 32),
        name="norm_router",
    )(x2d, g.reshape(1, d), router.T.astype(F32))


def _expert_ffn_kernel(xe_ref, wg_ref, wu_ref, wd_ref, gate_ref, o_ref, acc_ref):
    f = pl.program_id(2)
    xe = xe_ref[...]
    g = jnp.dot(xe, wg_ref[...], preferred_element_type=F32)
    u = jnp.dot(xe, wu_ref[...], preferred_element_type=F32)
    hid = (jax.nn.silu(g) * u).astype(BF16)
    part = jnp.dot(hid, wd_ref[...], preferred_element_type=F32)

    @pl.when(f == 0)
    def _():
        acc_ref[...] = part

    @pl.when(f > 0)
    def _():
        acc_ref[...] += part

    @pl.when(f == pl.num_programs(2) - 1)
    def _():
        o_ref[...] = (acc_ref[...] * gate_ref[...]).astype(o_ref.dtype)


def expert_ffn(xe, w_gate, w_up, w_down, gate, tm=1024, tf=512):
    ne, r, d = xe.shape
    fdim = w_gate.shape[2]
    return pl.pallas_call(
        _expert_ffn_kernel,
        grid=(ne, r // tm, fdim // tf),
        in_specs=[pl.BlockSpec((None, tm, d), lambda e, m, f: (e, m, 0)),
                  pl.BlockSpec((None, d, tf), lambda e, m, f: (e, 0, f)),
                  pl.BlockSpec((None, d, tf), lambda e, m, f: (e, 0, f)),
                  pl.BlockSpec((None, tf, d), lambda e, m, f: (e, f, 0)),
                  pl.BlockSpec((None, tm, 1), lambda e, m, f: (e, m, 0))],
        out_specs=pl.BlockSpec((None, tm, d), lambda e, m, f: (e, m, 0)),
        out_shape=jax.ShapeDtypeStruct((ne, r, d), F32),
        scratch_shapes=[pltpu.VMEM((tm, d), F32)],
        compiler_params=_params(("parallel", "parallel", "arbitrary"), 56),
        name="expert_ffn",
    )(xe, w_gate, w_up, w_down, gate)


def _final_norm_kernel(x_ref, g_ref, o_ref):
    x = x_ref[...]
    ms = jnp.mean(x * x, axis=-1, keepdims=True)
    o_ref[...] = x * lax.rsqrt(ms + RMS_EPS) * g_ref[...]


def final_norm(x2d, g, tm=512):
    t, d = x2d.shape
    return pl.pallas_call(
        _final_norm_kernel,
        grid=(t // tm,),
        in_specs=[pl.BlockSpec((tm, d), lambda i: (i, 0)), pl.BlockSpec((1, d), lambda i: (0, 0))],
        out_specs=pl.BlockSpec((tm, d), lambda i: (i, 0)),
        out_shape=jax.ShapeDtypeStruct((t, d), F32),
        compiler_params=_params(("parallel",), 32),
        name="final_norm",
    )(x2d, g.reshape(1, d))


def expert_choice_moe(x2d, bsz, seq, g, router, w_gate, w_up, w_down):
    t, d = x2d.shape
    ne = router.shape[1]
    cap = EC_CAPACITY_FACTOR * seq // ne
    h, aff = norm_router(x2d, g, router)
    aff_bes = aff.reshape(ne, bsz, seq).transpose(1, 0, 2)
    gate, idx = lax.top_k(aff_bes, cap)
    flat = (idx + (jnp.arange(bsz, dtype=idx.dtype) * seq)[:, None, None]).transpose(1, 0, 2).reshape(ne, bsz * cap)
    xe = h[flat]
    gate_er = gate.transpose(1, 0, 2).reshape(ne, bsz * cap, 1)
    ye = expert_ffn(xe, w_gate.astype(BF16), w_up.astype(BF16), w_down.astype(BF16), gate_er)
    return x2d.at[flat.reshape(-1)].add(ye.reshape(-1, d))


def kernel(x, norm_mix, norm_ffn, norm_final, w_in_ab, a_v_norm, a_spatial_w, a_spatial_b, b_rpb, w_out_ab,
           w_in_cd, c_conv, w_out_cd, router, w_gate, w_up, w_down):
    bsz, seq, d = x.shape
    depth = norm_mix.shape[0]
    half = d // 2
    nh = half // HEAD_DIM
    x2d = x.reshape(bsz * seq, d)
    for layer in range(depth):
        i = layer // 2
        if layer % 2 == 0:
            p = rms_matmul(x2d, norm_mix[layer], w_in_ab[i].astype(BF16))
            p3 = p.reshape(bsz, seq, -1)
            ya = sgu(p, a_v_norm[i], a_spatial_w[i], a_spatial_b[i], half)
            yb = neighbourhood_attention(p3, b_rpb[i], 2 * half // HEAD_DIM, nh).reshape(bsz * seq, half)
            x2d = out_proj_residual(ya, yb, w_out_ab[i].astype(BF16), x2d)
        else:
            p = rms_matmul(x2d, norm_mix[layer], w_in_cd[i].astype(BF16))
            p3 = p.reshape(bsz, seq, -1)
            yc = gated_short_conv(p3, c_conv[i], half).reshape(bsz * seq, half)
            yd = dilated_attention(p3, 3 * half // HEAD_DIM, nh).reshape(bsz * seq, half)
            x2d = out_proj_residual(yc, yd, w_out_cd[i].astype(BF16), x2d)
        x2d = expert_choice_moe(x2d, bsz, seq, norm_ffn[layer], router[layer],
                                w_gate[layer], w_up[layer], w_down[layer])
    return final_norm(x2d, norm_final).reshape(bsz, seq, d)
```

```python
import functools

import numpy as np
import jax
import jax.numpy as jnp
from jax import lax
from jax.experimental import pallas as pl
from jax.experimental.pallas import tpu as pltpu

F32 = jnp.float32
BF16 = jnp.bfloat16

HEAD_DIM = 128
CHUNK = 128
GRID_W = 64
NA_ROWS = 8
NA_COLS = 16
NA_QROWS = 4
NA_KROWS = NA_QROWS + NA_ROWS
CONV_W = 3
DIL_PATTERNS = ((128, 1), (512, 4), (2048, 16))
DIL_QBLK = 128
N_EXPERTS = 16
EC_CAPACITY_FACTOR = 2
RMS_EPS = 1e-6
LN_EPS = 1e-5
NEG_INF = -1e30
MIB = 1024 * 1024


def _params(semantics, vmem_mib):
    return pltpu.CompilerParams(dimension_semantics=semantics,
                                vmem_limit_bytes=int(vmem_mib * MIB))


def _rms_to_scratch(x_ref, g_ref, h_scr, rows):
    n = x_ref.shape[0] // rows

    def body(i, carry):
        r = pl.multiple_of(i * rows, rows)
        x = x_ref[pl.ds(r, rows), :]
        ms = jnp.mean(x * x, axis=-1, keepdims=True)
        h_scr[pl.ds(r, rows), :] = (x * lax.rsqrt(ms + RMS_EPS) * g_ref[...]).astype(h_scr.dtype)
        return carry

    lax.fori_loop(0, n, body, 0)


def _rms_matmul_kernel(x_ref, g_ref, w_ref, o_ref, h_scr):
    @pl.when(pl.program_id(1) == 0)
    def _():
        _rms_to_scratch(x_ref, g_ref, h_scr, 256)

    o_ref[...] = jnp.dot(h_scr[...], w_ref[...], preferred_element_type=F32).astype(o_ref.dtype)


def rms_matmul(x2d, g, w_bf16, tm=1024, tn=1024):
    t, d = x2d.shape
    n = w_bf16.shape[1]
    return pl.pallas_call(
        _rms_matmul_kernel,
        grid=(t // tm, n // tn),
        in_specs=[pl.BlockSpec((tm, d), lambda i, j: (i, 0)),
                  pl.BlockSpec((1, d), lambda i, j: (0, 0)),
                  pl.BlockSpec((d, tn), lambda i, j: (0, j))],
        out_specs=pl.BlockSpec((tm, tn), lambda i, j: (i, j)),
        out_shape=jax.ShapeDtypeStruct((t, n), BF16),
        scratch_shapes=[pltpu.VMEM((tm, d), BF16)],
        compiler_params=_params(("parallel", "arbitrary"), 48),
        name="rms_in_proj",
    )(x2d, g.reshape(1, d), w_bf16)


def _sgu_kernel(u_ref, v_ref, lng_ref, ws_ref, bias_ref, o_ref):
    tm, width = u_ref.shape
    groups = width // HEAD_DIM
    nchunks = tm // CHUNK
    vf = jax.nn.gelu(v_ref[...].astype(F32))
    mu = jnp.mean(vf, axis=-1, keepdims=True)
    dv = vf - mu
    var = jnp.mean(dv * dv, axis=-1, keepdims=True)
    vn = (dv * lax.rsqrt(var + LN_EPS) * lng_ref[...]).astype(BF16)
    for g in range(groups):
        cols = slice(g * HEAD_DIM, (g + 1) * HEAD_DIM)
        vg = jnp.concatenate([vn[n * CHUNK:(n + 1) * CHUNK, cols] for n in range(nchunks)], axis=1)
        mixed = jnp.dot(ws_ref[g], vg, preferred_element_type=F32)
        for n in range(nchunks):
            rows = slice(n * CHUNK, (n + 1) * CHUNK)
            u = jax.nn.gelu(u_ref[rows, cols].astype(F32))
            o_ref[rows, cols] = (u * (mixed[:, n * CHUNK:(n + 1) * CHUNK] + bias_ref[:, cols])).astype(o_ref.dtype)


def sgu(p2d, ln_g, w_s, b_s, width, tm=512):
    t = p2d.shape[0]
    groups = width // HEAD_DIM
    bias_full = jnp.repeat(b_s.T.astype(F32), HEAD_DIM, axis=1)
    return pl.pallas_call(
        _sgu_kernel,
        grid=(t // tm,),
        in_specs=[pl.BlockSpec((tm, width), lambda i: (i, 0)),
                  pl.BlockSpec((tm, width), lambda i: (i, 1)),
                  pl.BlockSpec((1, width), lambda i: (0, 0)),
                  pl.BlockSpec((groups, CHUNK, CHUNK), lambda i: (0, 0, 0)),
                  pl.BlockSpec((CHUNK, width), lambda i: (0, 0))],
        out_specs=pl.BlockSpec((tm, width), lambda i: (i, 0)),
        out_shape=jax.ShapeDtypeStruct((t, width), BF16),
        compiler_params=_params(("parallel",), 40),
        name="sgu",
    )(p2d, p2d, ln_g.reshape(1, width).astype(F32), w_s.astype(BF16), bias_full)


def _na_block_layout(rows):
    nblk = rows // NA_QROWS
    kh = min(NA_ROWS, rows)
    starts, variants, var_key = [], [], {}
    for qb in range(nblk):
        r0 = qb * NA_QROWS
        ws = int(np.clip(r0 - kh // 2, 0, rows - NA_KROWS))
        rs = np.clip(np.arange(r0, r0 + NA_QROWS) - kh // 2, 0, rows - kh)
        key = (tuple(rs - ws), r0 - ws)
        if key not in var_key:
            var_key[key] = len(var_key)
        starts.append(ws)
        variants.append(var_key[key])
    return starts, variants, list(var_key.keys()), kh


def _na_bias(rpb, rows):
    nh = rpb.shape[0]
    _, _, keys, kh = _na_block_layout(rows)
    qc = np.arange(GRID_W)[:, None]
    kc = np.arange(GRID_W)[None, :]
    col_start = np.clip(qc - NA_COLS // 2, 0, GRID_W - NA_COLS)
    col_valid = (kc >= col_start) & (kc < col_start + NA_COLS)
    dc_idx = np.clip(kc - qc + NA_COLS - 1, 0, 2 * NA_COLS - 2)
    tiles = jnp.where(jnp.asarray(col_valid)[None, None], rpb.astype(F32)[:, :, dc_idx], NEG_INF)
    tiles = jnp.concatenate([tiles, jnp.full((nh, 1, GRID_W, GRID_W), NEG_INF, F32)], axis=1)
    invalid = 2 * NA_ROWS - 1
    dr = np.full((len(keys), NA_QROWS, NA_KROWS), invalid, np.int32)
    for v, (rs_rel, r0_rel) in enumerate(keys):
        for i in range(NA_QROWS):
            for j in range(NA_KROWS):
                if rs_rel[i] <= j < rs_rel[i] + kh:
                    dr[v, i, j] = j - (r0_rel + i) + NA_ROWS - 1
    bias = tiles[:, dr]
    bias = bias.transpose(0, 1, 2, 4, 3, 5)
    return bias.reshape(nh, len(keys), NA_QROWS * GRID_W, NA_KROWS * GRID_W)


def _na_kernel(q_ref, k_ref, v_ref, bias_ref, o_ref, *, starts, variants, scale):
    qblk = NA_QROWS * GRID_W
    kblk = NA_KROWS * GRID_W
    for qb, (ws, var) in enumerate(zip(starts, variants)):
        q = (q_ref[qb * qblk:(qb + 1) * qblk, :].astype(F32) * scale).astype(BF16)
        k = k_ref[ws * GRID_W:ws * GRID_W + kblk, :]
        v = v_ref[ws * GRID_W:ws * GRID_W + kblk, :]
        s = lax.dot_general(q, k, (((1,), (1,)), ((), ())), preferred_element_type=F32)
        s = s + bias_ref[var]
        m = jnp.max(s, axis=-1, keepdims=True)
        e = jnp.exp(s - m)
        l = jnp.sum(e, axis=-1, keepdims=True)
        o = jnp.dot(e.astype(BF16), v, preferred_element_type=F32)
        o_ref[qb * qblk:(qb + 1) * qblk, :] = (o / l).astype(o_ref.dtype)


def neighbourhood_attention(p3, rpb, col0, nh):
    bsz, seq, _ = p3.shape
    rows = seq // GRID_W
    starts, variants, keys, _ = _na_block_layout(rows)
    bias = _na_bias(rpb, rows)
    nvar = len(keys)
    qblk, kblk = NA_QROWS * GRID_W, NA_KROWS * GRID_W
    kern = functools.partial(_na_kernel, starts=starts, variants=variants, scale=HEAD_DIM ** -0.5)
    head_spec = lambda off: pl.BlockSpec((None, seq, HEAD_DIM), lambda h, b: (b, 0, col0 + off + h))
    return pl.pallas_call(
        kern,
        grid=(nh, bsz),
        in_specs=[head_spec(0), head_spec(nh), head_spec(2 * nh),
                  pl.BlockSpec((None, nvar, qblk, kblk), lambda h, b: (h, 0, 0, 0))],
        out_specs=pl.BlockSpec((None, seq, HEAD_DIM), lambda h, b: (b, 0, h)),
        out_shape=jax.ShapeDtypeStruct((bsz, seq, nh * HEAD_DIM), BF16),
        compiler_params=_params(("parallel", "parallel"), 32),
        name="neighbourhood_attention",
    )(p3, p3, p3, bias)


def _conv_kernel(b_ref, c_ref, x_ref, taps_ref, o_ref):
    seq = b_ref.shape[0]
    z = c_ref[...].astype(F32) * x_ref[...].astype(F32)
    row = lax.broadcasted_iota(jnp.int32, z.shape, 0)
    z_prev = jnp.where(row == 0, 0.0, pltpu.roll(z, 1, axis=0))
    z_next = jnp.where(row == seq - 1, 0.0, pltpu.roll(z, seq - 1, axis=0))
    y = taps_ref[0:1, :] * z_prev + taps_ref[1:2, :] * z + taps_ref[2:3, :] * z_next
    o_ref[...] = (b_ref[...].astype(F32) * y).astype(o_ref.dtype)


def gated_short_conv(p3, taps, width, tc=256):
    bsz, seq, _ = p3.shape
    nblk = width // tc
    spec = lambda off: pl.BlockSpec((None, seq, tc), lambda b, j: (b, 0, off * nblk + j))
    return pl.pallas_call(
        _conv_kernel,
        grid=(bsz, nblk),
        in_specs=[spec(0), spec(1), spec(2), pl.BlockSpec((CONV_W, tc), lambda b, j: (0, j))],
        out_specs=pl.BlockSpec((None, seq, tc), lambda b, j: (b, 0, j)),
        out_shape=jax.ShapeDtypeStruct((bsz, seq, width), BF16),
        compiler_params=_params(("parallel", "parallel"), 40),
        name="gated_short_conv",
    )(p3, p3, p3, taps.astype(F32))


def _dil_windows(length):
    nk = min(2 * DIL_QBLK, length)
    out = []
    for i in range(length // DIL_QBLK):
        q0 = i * DIL_QBLK
        k0 = int(np.clip(q0 - DIL_QBLK // 2, 0, length - nk))
        out.append((q0, k0, nk))
    return out


def _dil_tables(seq, nh):
    slopes = np.array([2.0 ** (-8.0 * (h + 1) / nh) for h in range(nh)], dtype=np.float32)
    var_index, absd_tabs, dil_of = {}, [], []
    plan = []
    for window, dil in DIL_PATTERNS:
        radius = window // (2 * dil)
        length = seq // dil
        blocks = []
        for q0, k0, nk in _dil_windows(length):
            key = (dil, radius, k0 - q0, nk)
            if key not in var_index:
                var_index[key] = len(absd_tabs)
                delta = (k0 + np.arange(2 * DIL_QBLK)[None, :]) - (q0 + np.arange(DIL_QBLK)[:, None])
                absd = np.abs(delta).astype(np.float32)
                valid = (np.abs(delta) <= radius) & (np.arange(2 * DIL_QBLK)[None, :] < nk)
                absd_tabs.append((absd * dil, valid))
            blocks.append((q0, k0, nk, var_index[key]))
        plan.append((dil, length, blocks))
    pen = np.stack([a for a, _ in absd_tabs])
    valid = np.stack([v for _, v in absd_tabs])
    bias = np.where(valid[None], -(slopes[:, None, None, None] * pen[None]).astype(np.float32), np.float32(NEG_INF))
    return plan, bias.astype(np.float32)


def _dil_kernel(q_ref, k_ref, v_ref, bias_ref, o_ref, qf, kf, vf, o_scr, l_scr, *, plan, scale):
    qf[...] = q_ref[...].astype(F32) * scale
    kf[...] = k_ref[...].astype(F32)
    vf[...] = v_ref[...].astype(F32)

    def attend(q, k, v, var, nk):
        s = lax.dot_general(q, k, (((1,), (1,)), ((), ())), preferred_element_type=F32)
        s = s + bias_ref[var][:, :nk]
        m = jnp.max(s, axis=-1, keepdims=True)
        e = jnp.exp(s - m)
        l = jnp.sum(e, axis=-1, keepdims=True)
        o = jnp.dot(e.astype(BF16), v, preferred_element_type=F32) / l
        lse = jnp.broadcast_to(m + jnp.log(l), o.shape)
        return o, lse

    for p, (dil, length, blocks) in enumerate(plan):
        if dil == 1:
            for q0, k0, nk, var in blocks:
                o, lse = attend(qf[q0:q0 + DIL_QBLK, :].astype(BF16), kf[k0:k0 + nk, :].astype(BF16),
                                vf[k0:k0 + nk, :].astype(BF16), var, nk)
                o_scr[p, q0:q0 + DIL_QBLK, :] = o
                l_scr[p, q0:q0 + DIL_QBLK, :] = lse
        else:
            for r in range(dil):
                qs = qf[pl.ds(r, length, stride=dil), :].astype(BF16)
                ks = kf[pl.ds(r, length, stride=dil), :].astype(BF16)
                vs = vf[pl.ds(r, length, stride=dil), :].astype(BF16)
                for q0, k0, nk, var in blocks:
                    o, lse = attend(qs[q0:q0 + DIL_QBLK], ks[k0:k0 + nk], vs[k0:k0 + nk], var, nk)
                    o_scr[p, pl.ds(q0 * dil + r, DIL_QBLK, stride=dil), :] = o
                    l_scr[p, pl.ds(q0 * dil + r, DIL_QBLK, stride=dil), :] = lse

    npat = len(plan)
    lmax = l_scr[0]
    for p in range(1, npat):
        lmax = jnp.maximum(lmax, l_scr[p])
    num = jnp.zeros_like(lmax)
    den = jnp.zeros_like(lmax)
    for p in range(npat):
        w = jnp.exp(l_scr[p] - lmax)
        num = num + w * o_scr[p]
        den = den + w
    o_ref[...] = (num / den).astype(o_ref.dtype)


def dilated_attention(p3, col0, nh):
    bsz, seq, _ = p3.shape
    plan, bias_np = _dil_tables(seq, nh)
    nvar = bias_np.shape[1]
    kern = functools.partial(_dil_kernel, plan=plan, scale=HEAD_DIM ** -0.5)
    head_spec = lambda off: pl.BlockSpec((None, seq, HEAD_DIM), lambda h, b: (b, 0, col0 + off + h))
    npat = len(plan)
    return pl.pallas_call(
        kern,
        grid=(nh, bsz),
        in_specs=[head_spec(0), head_spec(nh), head_spec(2 * nh),
                  pl.BlockSpec((None, nvar, DIL_QBLK, 2 * DIL_QBLK), lambda h, b: (h, 0, 0, 0))],
        out_specs=pl.BlockSpec((None, seq, HEAD_DIM), lambda h, b: (b, 0, h)),
        out_shape=jax.ShapeDtypeStruct((bsz, seq, nh * HEAD_DIM), BF16),
        scratch_shapes=[pltpu.VMEM((seq, HEAD_DIM), F32)] * 3
                       + [pltpu.VMEM((npat, seq, HEAD_DIM), F32)] * 2,
        compiler_params=_params(("parallel", "parallel"), 32),
        name="dilated_attention",
    )(p3, p3, p3, jnp.asarray(bias_np))


def _out_proj_kernel(ya_ref, yb_ref, wa_ref, wb_ref, x_ref, o_ref):
    acc = jnp.dot(ya_ref[...], wa_ref[...], preferred_element_type=F32)
    acc = acc + jnp.dot(yb_ref[...], wb_ref[...], preferred_element_type=F32)
    o_ref[...] = x_ref[...] + acc


def out_proj_residual(ya, yb, w_bf16, x2d, tm=1024, tn=1024):
    t, d = x2d.shape
    ka = ya.shape[1]
    kb = yb.shape[1]
    assert ka == kb
    return pl.pallas_call(
        _out_proj_kernel,
        grid=(t // tm, d // tn),
        in_specs=[pl.BlockSpec((tm, ka), lambda i, j: (i, 0)),
                  pl.BlockSpec((tm, kb), lambda i, j: (i, 0)),
                  pl.BlockSpec((ka, tn), lambda i, j: (0, j)),
                  pl.BlockSpec((kb, tn), lambda i, j: (1, j)),
                  pl.BlockSpec((tm, tn), lambda i, j: (i, j))],
        out_specs=pl.BlockSpec((tm, tn), lambda i, j: (i, j)),
        out_shape=jax.ShapeDtypeStruct((t, d), F32),
        compiler_params=_params(("parallel", "parallel"), 48),
        name="out_proj_residual",
    )(ya, yb, w_bf16, w_bf16, x2d)


def _router_kernel(x_ref, g_ref, rt_ref, h_ref, aff_ref):
    x = x_ref[...]
    ms = jnp.mean(x * x, axis=-1, keepdims=True)
    h = x * lax.rsqrt(ms + RMS_EPS) * g_ref[...]
    h_ref[...] = h.astype(h_ref.dtype)
    logits = lax.dot_general(rt_ref[...], h, (((1,), (1,)), ((), ())),
                             preferred_element_type=F32, precision=lax.Precision.HIGHEST)
    m = jnp.max(logits, axis=0, keepdims=True)
    e = jnp.exp(logits - m)
    aff_ref[...] = e / jnp.sum(e, axis=0, keepdims=True)


def norm_router(x2d, g, router, tm=256):
    t, d = x2d.shape
    ne = router.shape[1]
    return pl.pallas_call(
        _router_kernel,
        grid=(t // tm,),
        in_specs=[pl.BlockSpec((tm, d), lambda i: (i, 0)),
                  pl.BlockSpec((1, d), lambda i: (0, 0)),
                  pl.BlockSpec((ne, d), lambda i: (0, 0))],
        out_specs=[pl.BlockSpec((tm, d), lambda i: (i, 0)),
                   pl.BlockSpec((ne, tm), lambda i: (0, i))],
        out_shape=[jax.ShapeDtypeStruct((t, d), BF16),
                   jax.ShapeDtypeStruct((ne, t), F32)],
        compiler_params=_params(("parallel",), 32),
        name="norm_router",
    )(x2d, g.reshape(1, d), router.T.astype(F32))


def _route_kernel(aff_ref, slot_ref, *, cap):
    a = aff_ref[...]
    ne, seq = a.shape
    lanes = 128
    r = lax.broadcasted_iota(jnp.int32, (lanes, lanes), 0)
    c = lax.broadcasted_iota(jnp.int32, (lanes, lanes), 1)
    tri = jnp.where(r <= c, 1.0, 0.0).astype(BF16)

    def count(mask_f):
        return jnp.sum(mask_f, axis=1, keepdims=True)

    def prefix(x):
        parts = []
        run = jnp.zeros((ne, 1), F32)
        for j in range(seq // lanes):
            blk = x[:, j * lanes:(j + 1) * lanes]
            inc = jnp.dot(blk.astype(BF16), tri, preferred_element_type=F32)
            parts.append(inc - blk + run)
            run = run + count(blk)
        return jnp.concatenate(parts, axis=1)

    def body(i, ans):
        cand = ans | jnp.left_shift(jnp.int32(1), 30 - i)
        cnt = count(jnp.where(a >= lax.bitcast_convert_type(cand, F32), 1.0, 0.0))
        return jnp.where(cnt >= cap, cand, ans)

    ans = lax.fori_loop(0, 31, body, jnp.zeros((ne, 1), jnp.int32))
    thr = lax.bitcast_convert_type(ans, F32)
    gt = jnp.where(a > thr, 1.0, 0.0)
    eq = jnp.where(a >= thr, 1.0, 0.0) - gt
    need = cap - count(gt)
    sel = gt + eq * jnp.where(prefix(eq) < need, 1.0, 0.0)
    slot_ref[...] = jnp.where(sel > 0.5, prefix(sel), -1.0).astype(jnp.int32)


def route(aff, bsz, seq, cap):
    ne = aff.shape[0]
    return pl.pallas_call(
        functools.partial(_route_kernel, cap=cap),
        grid=(bsz,),
        in_specs=[pl.BlockSpec((ne, seq), lambda b: (0, b))],
        out_specs=pl.BlockSpec((None, ne, seq), lambda b: (b, 0, 0)),
        out_shape=jax.ShapeDtypeStruct((bsz, ne, seq), jnp.int32),
        compiler_params=_params(("parallel",), 32),
        name="route",
    )(aff)


def _gather_kernel(slot_ref, aff_ref, h_ref, xe_ref, gate_ref, p_scr, *, cap, egroup):
    ne, seq = slot_ref.shape
    tn = h_ref.shape[1]

    @pl.when(pl.program_id(1) == 0)
    def _():
        cidx = lax.broadcasted_iota(jnp.int32, (cap, seq), 0)
        for e in range(ne):
            hit = cidx == slot_ref[e:e + 1, :]
            p_scr[e * cap:(e + 1) * cap, :] = jnp.where(hit, 1.0, 0.0).astype(BF16)
            gate_ref[e] = jnp.sum(jnp.where(hit, aff_ref[e:e + 1, :], 0.0), axis=1, keepdims=True)

    for g in range(ne // egroup):
        rows = slice(g * egroup * cap, (g + 1) * egroup * cap)
        xe = jnp.dot(p_scr[rows, :], h_ref[...], preferred_element_type=F32)
        xe_ref[g * egroup:(g + 1) * egroup] = xe.reshape(egroup, cap, tn).astype(xe_ref.dtype)


def gather_rows(slot, aff, h, cap, tn=512, egroup=4):
    bsz, ne, seq = slot.shape
    d = h.shape[1]
    return pl.pallas_call(
        functools.partial(_gather_kernel, cap=cap, egroup=egroup),
        grid=(bsz, d // tn),
        in_specs=[pl.BlockSpec((None, ne, seq), lambda b, n: (b, 0, 0)),
                  pl.BlockSpec((ne, seq), lambda b, n: (0, b)),
                  pl.BlockSpec((seq, tn), lambda b, n: (b, n))],
        out_specs=[pl.BlockSpec((ne, None, cap, tn), lambda b, n: (0, b, 0, n)),
                   pl.BlockSpec((ne, None, cap, 1), lambda b, n: (0, b, 0, 0))],
        out_shape=[jax.ShapeDtypeStruct((ne, bsz, cap, d), BF16),
                   jax.ShapeDtypeStruct((ne, bsz, cap, 1), F32)],
        scratch_shapes=[pltpu.VMEM((ne * cap, seq), BF16)],
        compiler_params=_params(("parallel", "arbitrary"), 48),
        name="moe_gather",
    )(slot, aff, h)


def _combine_kernel(slot_t_ref, ye_ref, x_ref, o_ref, pt_scr, *, cap):
    seq, ne = slot_t_ref.shape
    tn = x_ref.shape[1]

    @pl.when(pl.program_id(1) == 0)
    def _():
        cidx = lax.broadcasted_iota(jnp.int32, (seq, cap), 1)
        for e in range(ne):
            hit = cidx == slot_t_ref[:, e:e + 1]
            pt_scr[:, e * cap:(e + 1) * cap] = jnp.where(hit, 1.0, 0.0).astype(BF16)

    ye = ye_ref[...].reshape(ne * cap, tn)
    o_ref[...] = x_ref[...] + jnp.dot(pt_scr[...], ye, preferred_element_type=F32)


def combine(slot_t, ye, x2d, cap, tn=512):
    bsz, seq, ne = slot_t.shape
    d = x2d.shape[1]
    return pl.pallas_call(
        functools.partial(_combine_kernel, cap=cap),
        grid=(bsz, d // tn),
        in_specs=[pl.BlockSpec((None, seq, ne), lambda b, n: (b, 0, 0)),
                  pl.BlockSpec((ne, None, cap, tn), lambda b, n: (0, b, 0, n)),
                  pl.BlockSpec((seq, tn), lambda b, n: (b, n))],
        out_specs=pl.BlockSpec((seq, tn), lambda b, n: (b, n)),
        out_shape=jax.ShapeDtypeStruct(x2d.shape, F32),
        scratch_shapes=[pltpu.VMEM((seq, ne * cap), BF16)],
        compiler_params=_params(("parallel", "arbitrary"), 48),
        name="moe_combine",
    )(slot_t, ye, x2d)


def _expert_ffn_kernel(xe_ref, wg_ref, wu_ref, wd_ref, gate_ref, o_ref, acc_ref):
    f = pl.program_id(2)
    xe = xe_ref[...]
    g = jnp.dot(xe, wg_ref[...].astype(BF16), preferred_element_type=F32)
    u = jnp.dot(xe, wu_ref[...].astype(BF16), preferred_element_type=F32)
    hid = (jax.nn.silu(g) * u).astype(BF16)
    part = jnp.dot(hid, wd_ref[...].astype(BF16), preferred_element_type=F32)

    @pl.when(f == 0)
    def _():
        acc_ref[...] = part

    @pl.when(f > 0)
    def _():
        acc_ref[...] += part

    @pl.when(f == pl.num_programs(2) - 1)
    def _():
        o_ref[...] = (acc_ref[...] * gate_ref[...]).astype(o_ref.dtype)


def expert_ffn(xe, w_gate, w_up, w_down, layer, gate, tm=1024, tf=256):
    ne, r, d = xe.shape
    fdim = w_gate.shape[3]
    return pl.pallas_call(
        _expert_ffn_kernel,
        grid=(ne, r // tm, fdim // tf),
        in_specs=[pl.BlockSpec((None, tm, d), lambda e, m, f: (e, m, 0)),
                  pl.BlockSpec((None, None, d, tf), lambda e, m, f: (layer, e, 0, f)),
                  pl.BlockSpec((None, None, d, tf), lambda e, m, f: (layer, e, 0, f)),
                  pl.BlockSpec((None, None, tf, d), lambda e, m, f: (layer, e, f, 0)),
                  pl.BlockSpec((None, tm, 1), lambda e, m, f: (e, m, 0))],
        out_specs=pl.BlockSpec((None, tm, d), lambda e, m, f: (e, m, 0)),
        out_shape=jax.ShapeDtypeStruct((ne, r, d), BF16),
        scratch_shapes=[pltpu.VMEM((tm, d), F32)],
        compiler_params=_params(("parallel", "parallel", "arbitrary"), 56),
        name="expert_ffn",
    )(xe, w_gate, w_up, w_down, gate)


def _final_norm_kernel(x_ref, g_ref, o_ref):
    x = x_ref[...]
    ms = jnp.mean(x * x, axis=-1, keepdims=True)
    o_ref[...] = x * lax.rsqrt(ms + RMS_EPS) * g_ref[...]


def final_norm(x2d, g, tm=512):
    t, d = x2d.shape
    return pl.pallas_call(
        _final_norm_kernel,
        grid=(t // tm,),
        in_specs=[pl.BlockSpec((tm, d), lambda i: (i, 0)), pl.BlockSpec((1, d), lambda i: (0, 0))],
        out_specs=pl.BlockSpec((tm, d), lambda i: (i, 0)),
        out_shape=jax.ShapeDtypeStruct((t, d), F32),
        compiler_params=_params(("parallel",), 32),
        name="final_norm",
    )(x2d, g.reshape(1, d))


def expert_choice_moe(x2d, bsz, seq, layer, norm_ffn, router, w_gate, w_up, w_down):
    d = x2d.shape[1]
    ne = router.shape[2]
    cap = EC_CAPACITY_FACTOR * seq // ne
    h, aff = norm_router(x2d, norm_ffn[layer], router[layer])
    slot = route(aff, bsz, seq, cap)
    xe, gate = gather_rows(slot, aff, h, cap)
    ye = expert_ffn(xe.reshape(ne, bsz * cap, d), w_gate, w_up, w_down, layer,
                    gate.reshape(ne, bsz * cap, 1))
    return combine(slot.transpose(0, 2, 1), ye.reshape(ne, bsz, cap, d), x2d, cap)


def kernel(x, norm_mix, norm_ffn, norm_final, w_in_ab, a_v_norm, a_spatial_w, a_spatial_b, b_rpb, w_out_ab,
           w_in_cd, c_conv, w_out_cd, router, w_gate, w_up, w_down):
    bsz, seq, d = x.shape
    depth = norm_mix.shape[0]
    half = d // 2
    nh = half // HEAD_DIM
    x2d = x.reshape(bsz * seq, d)
    for layer in range(depth):
        i = layer // 2
        if layer % 2 == 0:
            p = rms_matmul(x2d, norm_mix[layer], w_in_ab[i].astype(BF16))
            p3 = p.reshape(bsz, seq, -1)
            ya = sgu(p, a_v_norm[i], a_spatial_w[i], a_spatial_b[i], half)
            yb = neighbourhood_attention(p3, b_rpb[i], 2 * half // HEAD_DIM, nh).reshape(bsz * seq, half)
            x2d = out_proj_residual(ya, yb, w_out_ab[i].astype(BF16), x2d)
        else:
            p = rms_matmul(x2d, norm_mix[layer], w_in_cd[i].astype(BF16))
            p3 = p.reshape(bsz, seq, -1)
            yc = gated_short_conv(p3, c_conv[i], half).reshape(bsz * seq, half)
            yd = dilated_attention(p3, 3 * half // HEAD_DIM, nh).reshape(bsz * seq, half)
            x2d = out_proj_residual(yc, yd, w_out_cd[i].astype(BF16), x2d)
        x2d = expert_choice_moe(x2d, bsz, seq, layer, norm_ffn, router, w_gate, w_up, w_down)
    return final_norm(x2d, norm_final).reshape(bsz, seq, d)
```

```python
import functools

import numpy as np
import jax
import jax.numpy as jnp
from jax import lax
from jax.experimental import pallas as pl
from jax.experimental.pallas import tpu as pltpu

F32 = jnp.float32
BF16 = jnp.bfloat16

HEAD_DIM = 128
CHUNK = 128
GRID_W = 64
NA_ROWS = 8
NA_COLS = 16
NA_QROWS = 4
NA_KROWS = NA_QROWS + NA_ROWS
CONV_W = 3
DIL_PATTERNS = ((128, 1), (512, 4), (2048, 16))
DIL_QBLK = 128
N_EXPERTS = 16
EC_CAPACITY_FACTOR = 2
RMS_EPS = 1e-6
LN_EPS = 1e-5
NEG_INF = -1e30
MIB = 1024 * 1024


def _params(semantics, vmem_mib):
    return pltpu.CompilerParams(dimension_semantics=semantics,
                                vmem_limit_bytes=int(vmem_mib * MIB))


def _rms_to_scratch(x_ref, g_ref, h_scr, rows):
    n = x_ref.shape[0] // rows

    def body(i, carry):
        r = pl.multiple_of(i * rows, rows)
        x = x_ref[pl.ds(r, rows), :]
        ms = jnp.mean(x * x, axis=-1, keepdims=True)
        h_scr[pl.ds(r, rows), :] = (x * lax.rsqrt(ms + RMS_EPS) * g_ref[...]).astype(h_scr.dtype)
        return carry

    lax.fori_loop(0, n, body, 0)


def _rms_matmul_kernel(x_ref, g_ref, w_ref, o_ref, h_scr):
    @pl.when(pl.program_id(1) == 0)
    def _():
        _rms_to_scratch(x_ref, g_ref, h_scr, 256)

    o_ref[...] = jnp.dot(h_scr[...], w_ref[...], preferred_element_type=F32).astype(o_ref.dtype)


def rms_matmul(x2d, g, w_bf16, tm=1024, tn=1024):
    t, d = x2d.shape
    n = w_bf16.shape[1]
    return pl.pallas_call(
        _rms_matmul_kernel,
        grid=(t // tm, n // tn),
        in_specs=[pl.BlockSpec((tm, d), lambda i, j: (i, 0)),
                  pl.BlockSpec((1, d), lambda i, j: (0, 0)),
                  pl.BlockSpec((d, tn), lambda i, j: (0, j))],
        out_specs=pl.BlockSpec((tm, tn), lambda i, j: (i, j)),
        out_shape=jax.ShapeDtypeStruct((t, n), BF16),
        scratch_shapes=[pltpu.VMEM((tm, d), BF16)],
        compiler_params=_params(("parallel", "arbitrary"), 48),
        name="rms_in_proj",
    )(x2d, g.reshape(1, d), w_bf16)


def _sgu_kernel(u_ref, v_ref, lng_ref, ws_ref, bias_ref, o_ref):
    tm, width = u_ref.shape
    groups = width // HEAD_DIM
    nchunks = tm // CHUNK
    vf = jax.nn.gelu(v_ref[...].astype(F32))
    mu = jnp.mean(vf, axis=-1, keepdims=True)
    dv = vf - mu
    var = jnp.mean(dv * dv, axis=-1, keepdims=True)
    vn = (dv * lax.rsqrt(var + LN_EPS) * lng_ref[...]).astype(BF16)
    for g in range(groups):
        cols = slice(g * HEAD_DIM, (g + 1) * HEAD_DIM)
        vg = jnp.concatenate([vn[n * CHUNK:(n + 1) * CHUNK, cols] for n in range(nchunks)], axis=1)
        mixed = jnp.dot(ws_ref[g], vg, preferred_element_type=F32)
        for n in range(nchunks):
            rows = slice(n * CHUNK, (n + 1) * CHUNK)
            u = jax.nn.gelu(u_ref[rows, cols].astype(F32))
            o_ref[rows, cols] = (u * (mixed[:, n * CHUNK:(n + 1) * CHUNK] + bias_ref[:, cols])).astype(o_ref.dtype)


def sgu(p2d, ln_g, w_s, b_s, width, tm=512):
    t = p2d.shape[0]
    groups = width // HEAD_DIM
    bias_full = jnp.repeat(b_s.T.astype(F32), HEAD_DIM, axis=1)
    return pl.pallas_call(
        _sgu_kernel,
        grid=(t // tm,),
        in_specs=[pl.BlockSpec((tm, width), lambda i: (i, 0)),
                  pl.BlockSpec((tm, width), lambda i: (i, 1)),
                  pl.BlockSpec((1, width), lambda i: (0, 0)),
                  pl.BlockSpec((groups, CHUNK, CHUNK), lambda i: (0, 0, 0)),
                  pl.BlockSpec((CHUNK, width), lambda i: (0, 0))],
        out_specs=pl.BlockSpec((tm, width), lambda i: (i, 0)),
        out_shape=jax.ShapeDtypeStruct((t, width), BF16),
        compiler_params=_params(("parallel",), 40),
        name="sgu",
    )(p2d, p2d, ln_g.reshape(1, width).astype(F32), w_s.astype(BF16), bias_full)


def _na_block_layout(rows):
    nblk = rows // NA_QROWS
    kh = min(NA_ROWS, rows)
    starts, variants, var_key = [], [], {}
    for qb in range(nblk):
        r0 = qb * NA_QROWS
        ws = int(np.clip(r0 - kh // 2, 0, rows - NA_KROWS))
        rs = np.clip(np.arange(r0, r0 + NA_QROWS) - kh // 2, 0, rows - kh)
        key = (tuple(rs - ws), r0 - ws)
        if key not in var_key:
            var_key[key] = len(var_key)
        starts.append(ws)
        variants.append(var_key[key])
    return starts, variants, list(var_key.keys()), kh


def _na_bias(rpb, rows):
    nh = rpb.shape[0]
    _, _, keys, kh = _na_block_layout(rows)
    qc = np.arange(GRID_W)[:, None]
    kc = np.arange(GRID_W)[None, :]
    col_start = np.clip(qc - NA_COLS // 2, 0, GRID_W - NA_COLS)
    col_valid = (kc >= col_start) & (kc < col_start + NA_COLS)
    dc_idx = np.clip(kc - qc + NA_COLS - 1, 0, 2 * NA_COLS - 2)
    tiles = jnp.where(jnp.asarray(col_valid)[None, None], rpb.astype(F32)[:, :, dc_idx], NEG_INF)
    tiles = jnp.concatenate([tiles, jnp.full((nh, 1, GRID_W, GRID_W), NEG_INF, F32)], axis=1)
    invalid = 2 * NA_ROWS - 1
    dr = np.full((len(keys), NA_QROWS, NA_KROWS), invalid, np.int32)
    for v, (rs_rel, r0_rel) in enumerate(keys):
        for i in range(NA_QROWS):
            for j in range(NA_KROWS):
                if rs_rel[i] <= j < rs_rel[i] + kh:
                    dr[v, i, j] = j - (r0_rel + i) + NA_ROWS - 1
    bias = tiles[:, dr]
    bias = bias.transpose(0, 1, 2, 4, 3, 5)
    return bias.reshape(nh, len(keys), NA_QROWS * GRID_W, NA_KROWS * GRID_W)


def _na_kernel(q_ref, k_ref, v_ref, bias_ref, o_ref, *, starts, variants, scale):
    qblk = NA_QROWS * GRID_W
    kblk = NA_KROWS * GRID_W
    for qb, (ws, var) in enumerate(zip(starts, variants)):
        q = (q_ref[qb * qblk:(qb + 1) * qblk, :].astype(F32) * scale).astype(BF16)
        k = k_ref[ws * GRID_W:ws * GRID_W + kblk, :]
        v = v_ref[ws * GRID_W:ws * GRID_W + kblk, :]
        s = lax.dot_general(q, k, (((1,), (1,)), ((), ())), preferred_element_type=F32)
        s = s + bias_ref[var]
        m = jnp.max(s, axis=-1, keepdims=True)
        e = jnp.exp(s - m)
        l = jnp.sum(e, axis=-1, keepdims=True)
        o = jnp.dot(e.astype(BF16), v, preferred_element_type=F32)
        o_ref[qb * qblk:(qb + 1) * qblk, :] = (o / l).astype(o_ref.dtype)


def neighbourhood_attention(p3, rpb, col0, nh):
    bsz, seq, _ = p3.shape
    rows = seq // GRID_W
    starts, variants, keys, _ = _na_block_layout(rows)
    bias = _na_bias(rpb, rows)
    nvar = len(keys)
    qblk, kblk = NA_QROWS * GRID_W, NA_KROWS * GRID_W
    kern = functools.partial(_na_kernel, starts=starts, variants=variants, scale=HEAD_DIM ** -0.5)
    head_spec = lambda off: pl.BlockSpec((None, seq, HEAD_DIM), lambda h, b: (b, 0, col0 + off + h))
    return pl.pallas_call(
        kern,
        grid=(nh, bsz),
        in_specs=[head_spec(0), head_spec(nh), head_spec(2 * nh),
                  pl.BlockSpec((None, nvar, qblk, kblk), lambda h, b: (h, 0, 0, 0))],
        out_specs=pl.BlockSpec((None, seq, HEAD_DIM), lambda h, b: (b, 0, h)),
        out_shape=jax.ShapeDtypeStruct((bsz, seq, nh * HEAD_DIM), BF16),
        compiler_params=_params(("parallel", "parallel"), 32),
        name="neighbourhood_attention",
    )(p3, p3, p3, bias)


def _conv_kernel(b_ref, c_ref, x_ref, taps_ref, o_ref):
    seq = b_ref.shape[0]
    z = c_ref[...].astype(F32) * x_ref[...].astype(F32)
    row = lax.broadcasted_iota(jnp.int32, z.shape, 0)
    z_prev = jnp.where(row == 0, 0.0, pltpu.roll(z, 1, axis=0))
    z_next = jnp.where(row == seq - 1, 0.0, pltpu.roll(z, seq - 1, axis=0))
    y = taps_ref[0:1, :] * z_prev + taps_ref[1:2, :] * z + taps_ref[2:3, :] * z_next
    o_ref[...] = (b_ref[...].astype(F32) * y).astype(o_ref.dtype)


def gated_short_conv(p3, taps, width, tc=256):
    bsz, seq, _ = p3.shape
    nblk = width // tc
    spec = lambda off: pl.BlockSpec((None, seq, tc), lambda b, j: (b, 0, off * nblk + j))
    return pl.pallas_call(
        _conv_kernel,
        grid=(bsz, nblk),
        in_specs=[spec(0), spec(1), spec(2), pl.BlockSpec((CONV_W, tc), lambda b, j: (0, j))],
        out_specs=pl.BlockSpec((None, seq, tc), lambda b, j: (b, 0, j)),
        out_shape=jax.ShapeDtypeStruct((bsz, seq, width), BF16),
        compiler_params=_params(("parallel", "parallel"), 40),
        name="gated_short_conv",
    )(p3, p3, p3, taps.astype(F32))


def _dil_windows(length):
    nk = min(2 * DIL_QBLK, length)
    out = []
    for i in range(length // DIL_QBLK):
        q0 = i * DIL_QBLK
        k0 = int(np.clip(q0 - DIL_QBLK // 2, 0, length - nk))
        out.append((q0, k0, nk))
    return out


def _dil_tables(seq, nh):
    slopes = np.array([2.0 ** (-8.0 * (h + 1) / nh) for h in range(nh)], dtype=np.float32)
    var_index, absd_tabs, dil_of = {}, [], []
    plan = []
    for window, dil in DIL_PATTERNS:
        radius = window // (2 * dil)
        length = seq // dil
        blocks = []
        for q0, k0, nk in _dil_windows(length):
            key = (dil, radius, k0 - q0, nk)
            if key not in var_index:
                var_index[key] = len(absd_tabs)
                delta = (k0 + np.arange(2 * DIL_QBLK)[None, :]) - (q0 + np.arange(DIL_QBLK)[:, None])
                absd = np.abs(delta).astype(np.float32)
                valid = (np.abs(delta) <= radius) & (np.arange(2 * DIL_QBLK)[None, :] < nk)
                absd_tabs.append((absd * dil, valid))
            blocks.append((q0, k0, nk, var_index[key]))
        plan.append((dil, length, blocks))
    pen = np.stack([a for a, _ in absd_tabs])
    valid = np.stack([v for _, v in absd_tabs])
    bias = np.where(valid[None], -(slopes[:, None, None, None] * pen[None]).astype(np.float32), np.float32(NEG_INF))
    return plan, bias.astype(np.float32)


def _dil_kernel(q_ref, k_ref, v_ref, bias_ref, o_ref, qf, kf, vf, o_scr, l_scr, *, plan, scale):
    qf[...] = q_ref[...].astype(F32) * scale
    kf[...] = k_ref[...].astype(F32)
    vf[...] = v_ref[...].astype(F32)

    def attend(q, k, v, var, nk):
        s = lax.dot_general(q, k, (((1,), (1,)), ((), ())), preferred_element_type=F32)
        s = s + bias_ref[var][:, :nk]
        m = jnp.max(s, axis=-1, keepdims=True)
        e = jnp.exp(s - m)
        l = jnp.sum(e, axis=-1, keepdims=True)
        o = jnp.dot(e.astype(BF16), v, preferred_element_type=F32) / l
        lse = jnp.broadcast_to(m + jnp.log(l), o.shape)
        return o, lse

    for p, (dil, length, blocks) in enumerate(plan):
        if dil == 1:
            for q0, k0, nk, var in blocks:
                o, lse = attend(qf[q0:q0 + DIL_QBLK, :].astype(BF16), kf[k0:k0 + nk, :].astype(BF16),
                                vf[k0:k0 + nk, :].astype(BF16), var, nk)
                o_scr[p, q0:q0 + DIL_QBLK, :] = o
                l_scr[p, q0:q0 + DIL_QBLK, :] = lse
        else:
            for r in range(dil):
                qs = qf[pl.ds(r, length, stride=dil), :].astype(BF16)
                ks = kf[pl.ds(r, length, stride=dil), :].astype(BF16)
                vs = vf[pl.ds(r, length, stride=dil), :].astype(BF16)
                for q0, k0, nk, var in blocks:
                    o, lse = attend(qs[q0:q0 + DIL_QBLK], ks[k0:k0 + nk], vs[k0:k0 + nk], var, nk)
                    o_scr[p, pl.ds(q0 * dil + r, DIL_QBLK, stride=dil), :] = o
                    l_scr[p, pl.ds(q0 * dil + r, DIL_QBLK, stride=dil), :] = lse

    npat = len(plan)
    lmax = l_scr[0]
    for p in range(1, npat):
        lmax = jnp.maximum(lmax, l_scr[p])
    num = jnp.zeros_like(lmax)
    den = jnp.zeros_like(lmax)
    for p in range(npat):
        w = jnp.exp(l_scr[p] - lmax)
        num = num + w * o_scr[p]
        den = den + w
    o_ref[...] = (num / den).astype(o_ref.dtype)


def dilated_attention(p3, col0, nh):
    bsz, seq, _ = p3.shape
    plan, bias_np = _dil_tables(seq, nh)
    nvar = bias_np.shape[1]
    kern = functools.partial(_dil_kernel, plan=plan, scale=HEAD_DIM ** -0.5)
    head_spec = lambda off: pl.BlockSpec((None, seq, HEAD_DIM), lambda h, b: (b, 0, col0 + off + h))
    npat = len(plan)
    return pl.pallas_call(
        kern,
        grid=(nh, bsz),
        in_specs=[head_spec(0), head_spec(nh), head_spec(2 * nh),
                  pl.BlockSpec((None, nvar, DIL_QBLK, 2 * DIL_QBLK), lambda h, b: (h, 0, 0, 0))],
        out_specs=pl.BlockSpec((None, seq, HEAD_DIM), lambda h, b: (b, 0, h)),
        out_shape=jax.ShapeDtypeStruct((bsz, seq, nh * HEAD_DIM), BF16),
        scratch_shapes=[pltpu.VMEM((seq, HEAD_DIM), F32)] * 3
                       + [pltpu.VMEM((npat, seq, HEAD_DIM), F32)] * 2,
        compiler_params=_params(("parallel", "parallel"), 32),
        name="dilated_attention",
    )(p3, p3, p3, jnp.asarray(bias_np))


def _out_proj_router_kernel(ya_ref, yb_ref, wa_ref, wb_ref, x_ref, g_ref, r_ref, xo_ref, h_ref, lg_ref):
    acc = jnp.dot(ya_ref[...], wa_ref[...], preferred_element_type=F32)
    acc = acc + jnp.dot(yb_ref[...], wb_ref[...], preferred_element_type=F32)
    xn = x_ref[...] + acc
    xo_ref[...] = xn
    ms = jnp.mean(xn * xn, axis=-1, keepdims=True)
    h = xn * lax.rsqrt(ms + RMS_EPS) * g_ref[...]
    h_hi = h.astype(BF16)
    h_ref[...] = h_hi
    h_lo = (h - h_hi.astype(F32)).astype(BF16)
    s = jnp.dot(h_hi, r_ref[...], preferred_element_type=F32) + jnp.dot(h_lo, r_ref[...], preferred_element_type=F32)
    ne = lg_ref.shape[1]
    lg_ref[...] = s[:, :ne] + s[:, ne:]


def out_proj_norm_router(ya, yb, w_bf16, x2d, g, router, tm=512):
    t, d = x2d.shape
    ka = ya.shape[1]
    kb = yb.shape[1]
    assert ka == kb
    ne = router.shape[1]
    r_hi = router.astype(BF16)
    r_lo = (router - r_hi.astype(F32)).astype(BF16)
    r_split = jnp.concatenate([r_hi, r_lo], axis=1)
    return pl.pallas_call(
        _out_proj_router_kernel,
        grid=(t // tm,),
        in_specs=[pl.BlockSpec((tm, ka), lambda i: (i, 0)),
                  pl.BlockSpec((tm, kb), lambda i: (i, 0)),
                  pl.BlockSpec((ka, d), lambda i: (0, 0)),
                  pl.BlockSpec((kb, d), lambda i: (1, 0)),
                  pl.BlockSpec((tm, d), lambda i: (i, 0)),
                  pl.BlockSpec((1, d), lambda i: (0, 0)),
                  pl.BlockSpec((d, 2 * ne), lambda i: (0, 0))],
        out_specs=[pl.BlockSpec((tm, d), lambda i: (i, 0)),
                   pl.BlockSpec((tm, d), lambda i: (i, 0)),
                   pl.BlockSpec((tm, ne), lambda i: (i, 0))],
        out_shape=[jax.ShapeDtypeStruct((t, d), F32),
                   jax.ShapeDtypeStruct((t, d), BF16),
                   jax.ShapeDtypeStruct((t, ne), F32)],
        compiler_params=_params(("parallel",), 56),
        name="out_proj_norm_router",
    )(ya, yb, w_bf16, w_bf16, x2d, g.reshape(1, d), r_split)


def _route_kernel(lg_ref, slot_ref, aff_ref, *, cap):
    logits = lg_ref[...]
    mx = jnp.max(logits, axis=0, keepdims=True)
    ex = jnp.exp(logits - mx)
    a = ex / jnp.sum(ex, axis=0, keepdims=True)
    aff_ref[...] = a
    ne, seq = a.shape
    lanes = 128
    r = lax.broadcasted_iota(jnp.int32, (lanes, lanes), 0)
    c = lax.broadcasted_iota(jnp.int32, (lanes, lanes), 1)
    tri = jnp.where(r <= c, 1.0, 0.0).astype(BF16)

    def count(mask_f):
        return jnp.sum(mask_f, axis=1, keepdims=True)

    def prefix(x):
        parts = []
        run = jnp.zeros((ne, 1), F32)
        for j in range(seq // lanes):
            blk = x[:, j * lanes:(j + 1) * lanes]
            inc = jnp.dot(blk.astype(BF16), tri, preferred_element_type=F32)
            parts.append(inc - blk + run)
            run = run + count(blk)
        return jnp.concatenate(parts, axis=1)

    def body(i, ans):
        cand = ans | jnp.left_shift(jnp.int32(1), 30 - i)
        cnt = count(jnp.where(a >= lax.bitcast_convert_type(cand, F32), 1.0, 0.0))
        return jnp.where(cnt >= cap, cand, ans)

    ans = lax.fori_loop(0, 31, body, jnp.zeros((ne, 1), jnp.int32))
    thr = lax.bitcast_convert_type(ans, F32)
    gt = jnp.where(a > thr, 1.0, 0.0)
    eq = jnp.where(a >= thr, 1.0, 0.0) - gt
    need = cap - count(gt)
    sel = gt + eq * jnp.where(prefix(eq) < need, 1.0, 0.0)
    slot_ref[...] = jnp.where(sel > 0.5, prefix(sel), -1.0).astype(jnp.int32)


def route(logits_t, bsz, seq, cap):
    ne = logits_t.shape[0]
    return pl.pallas_call(
        functools.partial(_route_kernel, cap=cap),
        grid=(bsz,),
        in_specs=[pl.BlockSpec((ne, seq), lambda b: (0, b))],
        out_specs=[pl.BlockSpec((None, ne, seq), lambda b: (b, 0, 0)),
                   pl.BlockSpec((None, ne, seq), lambda b: (b, 0, 0))],
        out_shape=[jax.ShapeDtypeStruct((bsz, ne, seq), jnp.int32),
                   jax.ShapeDtypeStruct((bsz, ne, seq), F32)],
        compiler_params=_params(("parallel",), 32),
        name="route",
    )(logits_t)


def _gather_kernel(slot_ref, aff_ref, h_ref, xe_ref, gate_ref, p_scr, *, cap, egroup):
    ne, seq = slot_ref.shape
    tn = h_ref.shape[1]

    @pl.when(pl.program_id(1) == 0)
    def _():
        cidx = lax.broadcasted_iota(jnp.int32, (cap, seq), 0)
        for e in range(ne):
            hit = cidx == slot_ref[e:e + 1, :]
            p_scr[e * cap:(e + 1) * cap, :] = jnp.where(hit, 1.0, 0.0).astype(BF16)
            gate_ref[e] = jnp.sum(jnp.where(hit, aff_ref[e:e + 1, :], 0.0), axis=1, keepdims=True)

    for g in range(ne // egroup):
        rows = slice(g * egroup * cap, (g + 1) * egroup * cap)
        xe = jnp.dot(p_scr[rows, :], h_ref[...], preferred_element_type=F32)
        xe_ref[g * egroup:(g + 1) * egroup] = xe.reshape(egroup, cap, tn).astype(xe_ref.dtype)


def gather_rows(slot, aff, h, cap, tn=512, egroup=4):
    bsz, ne, seq = slot.shape
    d = h.shape[1]
    return pl.pallas_call(
        functools.partial(_gather_kernel, cap=cap, egroup=egroup),
        grid=(bsz, d // tn),
        in_specs=[pl.BlockSpec((None, ne, seq), lambda b, n: (b, 0, 0)),
                  pl.BlockSpec((None, ne, seq), lambda b, n: (b, 0, 0)),
                  pl.BlockSpec((seq, tn), lambda b, n: (b, n))],
        out_specs=[pl.BlockSpec((ne, None, cap, tn), lambda b, n: (0, b, 0, n)),
                   pl.BlockSpec((ne, None, cap, 1), lambda b, n: (0, b, 0, 0))],
        out_shape=[jax.ShapeDtypeStruct((ne, bsz, cap, d), BF16),
                   jax.ShapeDtypeStruct((ne, bsz, cap, 1), F32)],
        scratch_shapes=[pltpu.VMEM((ne * cap, seq), BF16)],
        compiler_params=_params(("parallel", "arbitrary"), 48),
        name="moe_gather",
    )(slot, aff, h)


def _combine_kernel(slot_t_ref, ye_ref, x_ref, o_ref, pt_scr, *, cap):
    seq, ne = slot_t_ref.shape
    tn = x_ref.shape[1]

    @pl.when(pl.program_id(1) == 0)
    def _():
        cidx = lax.broadcasted_iota(jnp.int32, (seq, cap), 1)
        for e in range(ne):
            hit = cidx == slot_t_ref[:, e:e + 1]
            pt_scr[:, e * cap:(e + 1) * cap] = jnp.where(hit, 1.0, 0.0).astype(BF16)

    ye = ye_ref[...].reshape(ne * cap, tn)
    o_ref[...] = x_ref[...] + jnp.dot(pt_scr[...], ye, preferred_element_type=F32)


def combine(slot_t, ye, x2d, cap, tn=512):
    bsz, seq, ne = slot_t.shape
    d = x2d.shape[1]
    return pl.pallas_call(
        functools.partial(_combine_kernel, cap=cap),
        grid=(bsz, d // tn),
        in_specs=[pl.BlockSpec((None, seq, ne), lambda b, n: (b, 0, 0)),
                  pl.BlockSpec((ne, None, cap, tn), lambda b, n: (0, b, 0, n)),
                  pl.BlockSpec((seq, tn), lambda b, n: (b, n))],
        out_specs=pl.BlockSpec((seq, tn), lambda b, n: (b, n)),
        out_shape=jax.ShapeDtypeStruct(x2d.shape, F32),
        scratch_shapes=[pltpu.VMEM((seq, ne * cap), BF16)],
        compiler_params=_params(("parallel", "arbitrary"), 48),
        name="moe_combine",
    )(slot_t, ye, x2d)


def _expert_ffn_kernel(xe_ref, wg_ref, wu_ref, wd_ref, gate_ref, o_ref, acc_ref):
    f = pl.program_id(2)
    xe = xe_ref[...]
    g = jnp.dot(xe, wg_ref[...].astype(BF16), preferred_element_type=F32)
    u = jnp.dot(xe, wu_ref[...].astype(BF16), preferred_element_type=F32)
    hid = (jax.nn.silu(g) * u).astype(BF16)

    @pl.when(f == 0)
    def _():
        acc_ref[...] = jnp.zeros_like(acc_ref)

    acc_ref[...] += jnp.dot(hid, wd_ref[...].astype(BF16), preferred_element_type=F32)

    @pl.when(f == pl.num_programs(2) - 1)
    def _():
        o_ref[...] = (acc_ref[...] * gate_ref[...]).astype(o_ref.dtype)


def expert_ffn(xe, w_gate, w_up, w_down, layer, gate, tm=1024, tf=256):
    ne, r, d = xe.shape
    fdim = w_gate.shape[3]
    return pl.pallas_call(
        _expert_ffn_kernel,
        grid=(ne, r // tm, fdim // tf),
        in_specs=[pl.BlockSpec((None, tm, d), lambda e, m, f: (e, m, 0)),
                  pl.BlockSpec((None, None, d, tf), lambda e, m, f: (layer, e, 0, f)),
                  pl.BlockSpec((None, None, d, tf), lambda e, m, f: (layer, e, 0, f)),
                  pl.BlockSpec((None, None, tf, d), lambda e, m, f: (layer, e, f, 0)),
                  pl.BlockSpec((None, tm, 1), lambda e, m, f: (e, m, 0))],
        out_specs=pl.BlockSpec((None, tm, d), lambda e, m, f: (e, m, 0)),
        out_shape=jax.ShapeDtypeStruct((ne, r, d), BF16),
        scratch_shapes=[pltpu.VMEM((tm, d), F32)],
        compiler_params=_params(("parallel", "parallel", "arbitrary"), 56),
        name="expert_ffn",
    )(xe, w_gate, w_up, w_down, gate)


def _final_norm_kernel(x_ref, g_ref, o_ref):
    x = x_ref[...]
    ms = jnp.mean(x * x, axis=-1, keepdims=True)
    o_ref[...] = x * lax.rsqrt(ms + RMS_EPS) * g_ref[...]


def final_norm(x2d, g, tm=512):
    t, d = x2d.shape
    return pl.pallas_call(
        _final_norm_kernel,
        grid=(t // tm,),
        in_specs=[pl.BlockSpec((tm, d), lambda i: (i, 0)), pl.BlockSpec((1, d), lambda i: (0, 0))],
        out_specs=pl.BlockSpec((tm, d), lambda i: (i, 0)),
        out_shape=jax.ShapeDtypeStruct((t, d), F32),
        compiler_params=_params(("parallel",), 32),
        name="final_norm",
    )(x2d, g.reshape(1, d))


def expert_choice_moe(x2d, h, logits, bsz, seq, layer, w_gate, w_up, w_down):
    d = x2d.shape[1]
    ne = logits.shape[1]
    cap = EC_CAPACITY_FACTOR * seq // ne
    slot, aff = route(logits.T, bsz, seq, cap)
    xe, gate = gather_rows(slot, aff, h, cap)
    ye = expert_ffn(xe.reshape(ne, bsz * cap, d), w_gate, w_up, w_down, layer,
                    gate.reshape(ne, bsz * cap, 1))
    return combine(slot.transpose(0, 2, 1), ye.reshape(ne, bsz, cap, d), x2d, cap)


def kernel(x, norm_mix, norm_ffn, norm_final, w_in_ab, a_v_norm, a_spatial_w, a_spatial_b, b_rpb, w_out_ab,
           w_in_cd, c_conv, w_out_cd, router, w_gate, w_up, w_down):
    bsz, seq, d = x.shape
    depth = norm_mix.shape[0]
    half = d // 2
    nh = half // HEAD_DIM
    x2d = x.reshape(bsz * seq, d)
    for layer in range(depth):
        i = layer // 2
        if layer % 2 == 0:
            p = rms_matmul(x2d, norm_mix[layer], w_in_ab[i].astype(BF16))
            p3 = p.reshape(bsz, seq, -1)
            ya = sgu(p, a_v_norm[i], a_spatial_w[i], a_spatial_b[i], half)
            yb = neighbourhood_attention(p3, b_rpb[i], 2 * half // HEAD_DIM, nh).reshape(bsz * seq, half)
            mix, w_out = (ya, yb), w_out_ab[i]
        else:
            p = rms_matmul(x2d, norm_mix[layer], w_in_cd[i].astype(BF16))
            p3 = p.reshape(bsz, seq, -1)
            yc = gated_short_conv(p3, c_conv[i], half).reshape(bsz * seq, half)
            yd = dilated_attention(p3, 3 * half // HEAD_DIM, nh).reshape(bsz * seq, half)
            mix, w_out = (yc, yd), w_out_cd[i]
        x2d, h, logits = out_proj_norm_router(mix[0], mix[1], w_out.astype(BF16), x2d,
                                              norm_ffn[layer], router[layer])
        x2d = expert_choice_moe(x2d, h, logits, bsz, seq, layer, w_gate, w_up, w_down)
    return final_norm(x2d, norm_final).reshape(bsz, seq, d)
```

```python
import functools

import numpy as np
import jax
import jax.numpy as jnp
from jax import lax
from jax.experimental import pallas as pl
from jax.experimental.pallas import tpu as pltpu

F32 = jnp.float32
BF16 = jnp.bfloat16

HEAD_DIM = 128
CHUNK = 128
GRID_W = 64
NA_ROWS = 8
NA_COLS = 16
NA_QROWS = 4
NA_KROWS = NA_QROWS + NA_ROWS
CONV_W = 3
DIL_PATTERNS = ((128, 1), (512, 4), (2048, 16))
DIL_QBLK = 128
N_EXPERTS = 16
EC_CAPACITY_FACTOR = 2
RMS_EPS = 1e-6
LN_EPS = 1e-5
NEG_INF = -1e30
MIB = 1024 * 1024


def _params(semantics, vmem_mib):
    return pltpu.CompilerParams(dimension_semantics=semantics,
                                vmem_limit_bytes=int(vmem_mib * MIB))


def _rms_to_scratch(x_ref, g_ref, h_scr, rows):
    n = x_ref.shape[0] // rows

    def body(i, carry):
        r = pl.multiple_of(i * rows, rows)
        x = x_ref[pl.ds(r, rows), :]
        ms = jnp.mean(x * x, axis=-1, keepdims=True)
        h_scr[pl.ds(r, rows), :] = (x * lax.rsqrt(ms + RMS_EPS) * g_ref[...]).astype(h_scr.dtype)
        return carry

    lax.fori_loop(0, n, body, 0)


def _rms_matmul_kernel(x_ref, g_ref, w_ref, o_ref, h_scr):
    @pl.when(pl.program_id(1) == 0)
    def _():
        _rms_to_scratch(x_ref, g_ref, h_scr, 256)

    o_ref[...] = jnp.dot(h_scr[...], w_ref[...].astype(BF16), preferred_element_type=F32).astype(o_ref.dtype)


def rms_matmul(x2d, g, w, tm=1024, tn=1024):
    t, d = x2d.shape
    n = w.shape[1]
    return pl.pallas_call(
        _rms_matmul_kernel,
        grid=(t // tm, n // tn),
        in_specs=[pl.BlockSpec((tm, d), lambda i, j: (i, 0)),
                  pl.BlockSpec((1, d), lambda i, j: (0, 0)),
                  pl.BlockSpec((d, tn), lambda i, j: (0, j))],
        out_specs=pl.BlockSpec((tm, tn), lambda i, j: (i, j)),
        out_shape=jax.ShapeDtypeStruct((t, n), BF16),
        scratch_shapes=[pltpu.VMEM((tm, d), BF16)],
        compiler_params=_params(("parallel", "arbitrary"), 56),
        name="rms_in_proj",
    )(x2d, g.reshape(1, d), w)


def _sgu_kernel(u_ref, v_ref, lng_ref, ws_ref, bias_ref, o_ref):
    tm, width = u_ref.shape
    groups = width // HEAD_DIM
    nchunks = tm // CHUNK
    vf = jax.nn.gelu(v_ref[...].astype(F32))
    mu = jnp.mean(vf, axis=-1, keepdims=True)
    dv = vf - mu
    var = jnp.mean(dv * dv, axis=-1, keepdims=True)
    vn = (dv * lax.rsqrt(var + LN_EPS) * lng_ref[...]).astype(BF16)
    for g in range(groups):
        cols = slice(g * HEAD_DIM, (g + 1) * HEAD_DIM)
        vg = jnp.concatenate([vn[n * CHUNK:(n + 1) * CHUNK, cols] for n in range(nchunks)], axis=1)
        mixed = jnp.dot(ws_ref[g], vg, preferred_element_type=F32)
        for n in range(nchunks):
            rows = slice(n * CHUNK, (n + 1) * CHUNK)
            u = jax.nn.gelu(u_ref[rows, cols].astype(F32))
            o_ref[rows, cols] = (u * (mixed[:, n * CHUNK:(n + 1) * CHUNK] + bias_ref[:, cols])).astype(o_ref.dtype)


def sgu(p2d, ln_g, w_s, b_s, width, tm=512):
    t = p2d.shape[0]
    groups = width // HEAD_DIM
    bias_full = jnp.repeat(b_s.T.astype(F32), HEAD_DIM, axis=1)
    return pl.pallas_call(
        _sgu_kernel,
        grid=(t // tm,),
        in_specs=[pl.BlockSpec((tm, width), lambda i: (i, 0)),
                  pl.BlockSpec((tm, width), lambda i: (i, 1)),
                  pl.BlockSpec((1, width), lambda i: (0, 0)),
                  pl.BlockSpec((groups, CHUNK, CHUNK), lambda i: (0, 0, 0)),
                  pl.BlockSpec((CHUNK, width), lambda i: (0, 0))],
        out_specs=pl.BlockSpec((tm, width), lambda i: (i, 0)),
        out_shape=jax.ShapeDtypeStruct((t, width), BF16),
        compiler_params=_params(("parallel",), 40),
        name="sgu",
    )(p2d, p2d, ln_g.reshape(1, width).astype(F32), w_s.astype(BF16), bias_full)


def _na_block_layout(rows):
    nblk = rows // NA_QROWS
    kh = min(NA_ROWS, rows)
    starts, variants, var_key = [], [], {}
    for qb in range(nblk):
        r0 = qb * NA_QROWS
        ws = int(np.clip(r0 - kh // 2, 0, rows - NA_KROWS))
        rs = np.clip(np.arange(r0, r0 + NA_QROWS) - kh // 2, 0, rows - kh)
        key = (tuple(rs - ws), r0 - ws)
        if key not in var_key:
            var_key[key] = len(var_key)
        starts.append(ws)
        variants.append(var_key[key])
    return starts, variants, list(var_key.keys()), kh


def _na_tiles(rpb, rows):
    nh = rpb.shape[0]
    _, _, keys, kh = _na_block_layout(rows)
    qc = np.arange(GRID_W)[:, None]
    kc = np.arange(GRID_W)[None, :]
    col_start = np.clip(qc - NA_COLS // 2, 0, GRID_W - NA_COLS)
    col_valid = (kc >= col_start) & (kc < col_start + NA_COLS)
    dc_idx = np.clip(kc - qc + NA_COLS - 1, 0, 2 * NA_COLS - 2)
    tiles = jnp.where(jnp.asarray(col_valid)[None, None], rpb.astype(F32)[:, :, dc_idx], NEG_INF)
    tiles = jnp.concatenate([tiles, jnp.full((nh, 1, GRID_W, GRID_W), NEG_INF, F32)], axis=1)
    invalid = 2 * NA_ROWS - 1
    dr = np.full((len(keys), NA_QROWS, NA_KROWS), invalid, np.int32)
    for v, (rs_rel, r0_rel) in enumerate(keys):
        for i in range(NA_QROWS):
            for j in range(NA_KROWS):
                if rs_rel[i] <= j < rs_rel[i] + kh:
                    dr[v, i, j] = j - (r0_rel + i) + NA_ROWS - 1
    return tiles, dr


def _na_kernel(q_ref, k_ref, v_ref, tiles_ref, o_ref, bias_scr, *, starts, variants, tile_idx, scale):
    qblk = NA_QROWS * GRID_W
    kblk = NA_KROWS * GRID_W

    @pl.when(pl.program_id(1) == 0)
    def _():
        for var in range(tile_idx.shape[0]):
            for i in range(NA_QROWS):
                strip = jnp.concatenate([tiles_ref[int(tile_idx[var, i, j])] for j in range(NA_KROWS)], axis=1)
                bias_scr[var, i * GRID_W:(i + 1) * GRID_W, :] = strip

    for qb, (ws, var) in enumerate(zip(starts, variants)):
        q = (q_ref[qb * qblk:(qb + 1) * qblk, :].astype(F32) * scale).astype(BF16)
        k = k_ref[ws * GRID_W:ws * GRID_W + kblk, :]
        v = v_ref[ws * GRID_W:ws * GRID_W + kblk, :]
        s = lax.dot_general(q, k, (((1,), (1,)), ((), ())), preferred_element_type=F32)
        s = s + bias_scr[var]
        m = jnp.max(s, axis=-1, keepdims=True)
        e = jnp.exp(s - m)
        l = jnp.sum(e, axis=-1, keepdims=True)
        o = jnp.dot(e.astype(BF16), v, preferred_element_type=F32)
        o_ref[qb * qblk:(qb + 1) * qblk, :] = (o / l).astype(o_ref.dtype)


def neighbourhood_attention(p3, rpb, col0, nh):
    bsz, seq, _ = p3.shape
    rows = seq // GRID_W
    starts, variants, keys, _ = _na_block_layout(rows)
    tiles, tile_idx = _na_tiles(rpb, rows)
    nvar = len(keys)
    ntiles = tiles.shape[1]
    qblk, kblk = NA_QROWS * GRID_W, NA_KROWS * GRID_W
    kern = functools.partial(_na_kernel, starts=starts, variants=variants, tile_idx=tile_idx,
                             scale=HEAD_DIM ** -0.5)
    head_spec = lambda off: pl.BlockSpec((None, seq, HEAD_DIM), lambda h, b: (b, 0, col0 + off + h))
    return pl.pallas_call(
        kern,
        grid=(nh, bsz),
        in_specs=[head_spec(0), head_spec(nh), head_spec(2 * nh),
                  pl.BlockSpec((None, ntiles, GRID_W, GRID_W), lambda h, b: (h, 0, 0, 0))],
        out_specs=pl.BlockSpec((None, seq, HEAD_DIM), lambda h, b: (b, 0, h)),
        out_shape=jax.ShapeDtypeStruct((bsz, seq, nh * HEAD_DIM), BF16),
        scratch_shapes=[pltpu.VMEM((nvar, qblk, kblk), F32)],
        compiler_params=_params(("parallel", "arbitrary"), 32),
        name="neighbourhood_attention",
    )(p3, p3, p3, tiles)


def _conv_kernel(b_ref, c_ref, x_ref, taps_ref, o_ref):
    seq = b_ref.shape[0]
    z = c_ref[...].astype(F32) * x_ref[...].astype(F32)
    row = lax.broadcasted_iota(jnp.int32, z.shape, 0)
    z_prev = jnp.where(row == 0, 0.0, pltpu.roll(z, 1, axis=0))
    z_next = jnp.where(row == seq - 1, 0.0, pltpu.roll(z, seq - 1, axis=0))
    y = taps_ref[0:1, :] * z_prev + taps_ref[1:2, :] * z + taps_ref[2:3, :] * z_next
    o_ref[...] = (b_ref[...].astype(F32) * y).astype(o_ref.dtype)


def gated_short_conv(p3, taps, width, tc=256):
    bsz, seq, _ = p3.shape
    nblk = width // tc
    spec = lambda off: pl.BlockSpec((None, seq, tc), lambda b, j: (b, 0, off * nblk + j))
    return pl.pallas_call(
        _conv_kernel,
        grid=(bsz, nblk),
        in_specs=[spec(0), spec(1), spec(2), pl.BlockSpec((CONV_W, tc), lambda b, j: (0, j))],
        out_specs=pl.BlockSpec((None, seq, tc), lambda b, j: (b, 0, j)),
        out_shape=jax.ShapeDtypeStruct((bsz, seq, width), BF16),
        compiler_params=_params(("parallel", "parallel"), 40),
        name="gated_short_conv",
    )(p3, p3, p3, taps.astype(F32))


def _dil_windows(length):
    nk = min(2 * DIL_QBLK, length)
    out = []
    for i in range(length // DIL_QBLK):
        q0 = i * DIL_QBLK
        k0 = int(np.clip(q0 - DIL_QBLK // 2, 0, length - nk))
        out.append((q0, k0, nk))
    return out


def _dil_tables(seq, nh):
    slopes = np.array([2.0 ** (-8.0 * (h + 1) / nh) for h in range(nh)], dtype=np.float32)
    var_index, absd_tabs, dil_of = {}, [], []
    plan = []
    for window, dil in DIL_PATTERNS:
        radius = window // (2 * dil)
        length = seq // dil
        blocks = []
        for q0, k0, nk in _dil_windows(length):
            key = (dil, radius, k0 - q0, nk)
            if key not in var_index:
                var_index[key] = len(absd_tabs)
                delta = (k0 + np.arange(2 * DIL_QBLK)[None, :]) - (q0 + np.arange(DIL_QBLK)[:, None])
                absd = np.abs(delta).astype(np.float32)
                valid = (np.abs(delta) <= radius) & (np.arange(2 * DIL_QBLK)[None, :] < nk)
                absd_tabs.append((absd * dil, valid))
            blocks.append((q0, k0, nk, var_index[key]))
        plan.append((dil, length, blocks))
    pen = np.stack([a for a, _ in absd_tabs])
    valid = np.stack([v for _, v in absd_tabs])
    bias = np.where(valid[None], -(slopes[:, None, None, None] * pen[None]).astype(np.float32), np.float32(NEG_INF))
    return plan, bias.astype(np.float32)


def _dil_kernel(q_ref, k_ref, v_ref, bias_ref, o_ref, qf, kf, vf, o_scr, l_scr, *, plan, scale):
    qf[...] = q_ref[...].astype(F32) * scale
    kf[...] = k_ref[...].astype(F32)
    vf[...] = v_ref[...].astype(F32)

    def attend(q, k, v, var, nk):
        s = lax.dot_general(q, k, (((1,), (1,)), ((), ())), preferred_element_type=F32)
        s = s + bias_ref[var][:, :nk]
        m = jnp.max(s, axis=-1, keepdims=True)
        e = jnp.exp(s - m)
        l = jnp.sum(e, axis=-1, keepdims=True)
        o = jnp.dot(e.astype(BF16), v, preferred_element_type=F32) / l
        lse = jnp.broadcast_to(m + jnp.log(l), o.shape)
        return o, lse

    for p, (dil, length, blocks) in enumerate(plan):
        if dil == 1:
            for q0, k0, nk, var in blocks:
                o, lse = attend(qf[q0:q0 + DIL_QBLK, :].astype(BF16), kf[k0:k0 + nk, :].astype(BF16),
                                vf[k0:k0 + nk, :].astype(BF16), var, nk)
                o_scr[p, q0:q0 + DIL_QBLK, :] = o
                l_scr[p, q0:q0 + DIL_QBLK, :] = lse
        else:
            for r in range(dil):
                qs = qf[pl.ds(r, length, stride=dil), :].astype(BF16)
                ks = kf[pl.ds(r, length, stride=dil), :].astype(BF16)
                vs = vf[pl.ds(r, length, stride=dil), :].astype(BF16)
                for q0, k0, nk, var in blocks:
                    o, lse = attend(qs[q0:q0 + DIL_QBLK], ks[k0:k0 + nk], vs[k0:k0 + nk], var, nk)
                    o_scr[p, pl.ds(q0 * dil + r, DIL_QBLK, stride=dil), :] = o
                    l_scr[p, pl.ds(q0 * dil + r, DIL_QBLK, stride=dil), :] = lse

    npat = len(plan)
    lmax = l_scr[0]
    for p in range(1, npat):
        lmax = jnp.maximum(lmax, l_scr[p])
    num = jnp.zeros_like(lmax)
    den = jnp.zeros_like(lmax)
    for p in range(npat):
        w = jnp.exp(l_scr[p] - lmax)
        num = num + w * o_scr[p]
        den = den + w
    o_ref[...] = (num / den).astype(o_ref.dtype)


def dilated_attention(p3, col0, nh):
    bsz, seq, _ = p3.shape
    plan, bias_np = _dil_tables(seq, nh)
    nvar = bias_np.shape[1]
    kern = functools.partial(_dil_kernel, plan=plan, scale=HEAD_DIM ** -0.5)
    head_spec = lambda off: pl.BlockSpec((None, seq, HEAD_DIM), lambda h, b: (b, 0, col0 + off + h))
    npat = len(plan)
    return pl.pallas_call(
        kern,
        grid=(nh, bsz),
        in_specs=[head_spec(0), head_spec(nh), head_spec(2 * nh),
                  pl.BlockSpec((None, nvar, DIL_QBLK, 2 * DIL_QBLK), lambda h, b: (h, 0, 0, 0))],
        out_specs=pl.BlockSpec((None, seq, HEAD_DIM), lambda h, b: (b, 0, h)),
        out_shape=jax.ShapeDtypeStruct((bsz, seq, nh * HEAD_DIM), BF16),
        scratch_shapes=[pltpu.VMEM((seq, HEAD_DIM), F32)] * 3
                       + [pltpu.VMEM((npat, seq, HEAD_DIM), F32)] * 2,
        compiler_params=_params(("parallel", "parallel"), 32),
        name="dilated_attention",
    )(p3, p3, p3, jnp.asarray(bias_np))


def _out_proj_router_kernel(ya_ref, yb_ref, wa_ref, wb_ref, x_ref, g_ref, r_ref, xo_ref, h_ref, lg_ref):
    acc = jnp.dot(ya_ref[...], wa_ref[...], preferred_element_type=F32)
    acc = acc + jnp.dot(yb_ref[...], wb_ref[...], preferred_element_type=F32)
    xn = x_ref[...] + acc
    xo_ref[...] = xn
    ms = jnp.mean(xn * xn, axis=-1, keepdims=True)
    h = xn * lax.rsqrt(ms + RMS_EPS) * g_ref[...]
    h_hi = h.astype(BF16)
    h_ref[...] = h_hi
    h_lo = (h - h_hi.astype(F32)).astype(BF16)
    s = jnp.dot(h_hi, r_ref[...], preferred_element_type=F32) + jnp.dot(h_lo, r_ref[...], preferred_element_type=F32)
    ne = lg_ref.shape[1]
    lg_ref[...] = s[:, :ne] + s[:, ne:]


def out_proj_norm_router(ya, yb, w_bf16, x2d, g, router, tm=512):
    t, d = x2d.shape
    ka = ya.shape[1]
    kb = yb.shape[1]
    assert ka == kb
    ne = router.shape[1]
    r_hi = router.astype(BF16)
    r_lo = (router - r_hi.astype(F32)).astype(BF16)
    r_split = jnp.concatenate([r_hi, r_lo], axis=1)
    return pl.pallas_call(
        _out_proj_router_kernel,
        grid=(t // tm,),
        in_specs=[pl.BlockSpec((tm, ka), lambda i: (i, 0)),
                  pl.BlockSpec((tm, kb), lambda i: (i, 0)),
                  pl.BlockSpec((ka, d), lambda i: (0, 0)),
                  pl.BlockSpec((kb, d), lambda i: (1, 0)),
                  pl.BlockSpec((tm, d), lambda i: (i, 0)),
                  pl.BlockSpec((1, d), lambda i: (0, 0)),
                  pl.BlockSpec((d, 2 * ne), lambda i: (0, 0))],
        out_specs=[pl.BlockSpec((tm, d), lambda i: (i, 0)),
                   pl.BlockSpec((tm, d), lambda i: (i, 0)),
                   pl.BlockSpec((tm, ne), lambda i: (i, 0))],
        out_shape=[jax.ShapeDtypeStruct((t, d), F32),
                   jax.ShapeDtypeStruct((t, d), BF16),
                   jax.ShapeDtypeStruct((t, ne), F32)],
        compiler_params=_params(("parallel",), 56),
        name="out_proj_norm_router",
    )(ya, yb, w_bf16, w_bf16, x2d, g.reshape(1, d), r_split)


def _route_kernel(lg_ref, slot_ref, aff_ref, *, cap):
    logits = lg_ref[...]
    mx = jnp.max(logits, axis=0, keepdims=True)
    ex = jnp.exp(logits - mx)
    a = ex / jnp.sum(ex, axis=0, keepdims=True)
    aff_ref[...] = a
    ne, seq = a.shape
    lanes = 128
    r = lax.broadcasted_iota(jnp.int32, (lanes, lanes), 0)
    c = lax.broadcasted_iota(jnp.int32, (lanes, lanes), 1)
    tri = jnp.where(r <= c, 1.0, 0.0).astype(BF16)

    def count(mask_f):
        return jnp.sum(mask_f, axis=1, keepdims=True)

    def prefix(x):
        parts = []
        run = jnp.zeros((ne, 1), F32)
        for j in range(seq // lanes):
            blk = x[:, j * lanes:(j + 1) * lanes]
            inc = jnp.dot(blk.astype(BF16), tri, preferred_element_type=F32)
            parts.append(inc - blk + run)
            run = run + count(blk)
        return jnp.concatenate(parts, axis=1)

    def body(i, ans):
        cand = ans | jnp.left_shift(jnp.int32(1), 30 - i)
        cnt = count(jnp.where(a >= lax.bitcast_convert_type(cand, F32), 1.0, 0.0))
        return jnp.where(cnt >= cap, cand, ans)

    ans = lax.fori_loop(0, 31, body, jnp.zeros((ne, 1), jnp.int32))
    thr = lax.bitcast_convert_type(ans, F32)
    gt = jnp.where(a > thr, 1.0, 0.0)
    eq = jnp.where(a >= thr, 1.0, 0.0) - gt
    need = cap - count(gt)
    sel = gt + eq * jnp.where(prefix(eq) < need, 1.0, 0.0)
    slot_ref[...] = jnp.where(sel > 0.5, prefix(sel), -1.0).astype(jnp.int32)


def route(logits_t, bsz, seq, cap):
    ne = logits_t.shape[0]
    return pl.pallas_call(
        functools.partial(_route_kernel, cap=cap),
        grid=(bsz,),
        in_specs=[pl.BlockSpec((ne, seq), lambda b: (0, b))],
        out_specs=[pl.BlockSpec((None, ne, seq), lambda b: (b, 0, 0)),
                   pl.BlockSpec((None, ne, seq), lambda b: (b, 0, 0))],
        out_shape=[jax.ShapeDtypeStruct((bsz, ne, seq), jnp.int32),
                   jax.ShapeDtypeStruct((bsz, ne, seq), F32)],
        compiler_params=_params(("parallel",), 32),
        name="route",
    )(logits_t)


def _gather_kernel(slot_ref, aff_ref, h_ref, xe_ref, gate_ref, *, cap, ncol):
    eg, seq = slot_ref.shape
    d = h_ref.shape[1]
    cidx = lax.broadcasted_iota(jnp.int32, (cap, seq), 0)
    rows = []
    for j in range(eg):
        hit = cidx == slot_ref[j:j + 1, :]
        rows.append(jnp.where(hit, 1.0, 0.0).astype(BF16))
        gate_ref[j] = jnp.sum(jnp.where(hit, aff_ref[j:j + 1, :], 0.0), axis=1, keepdims=True)
    p = jnp.concatenate(rows, axis=0)
    dc = d // ncol
    for c in range(ncol):
        xe = jnp.dot(p, h_ref[:, c * dc:(c + 1) * dc], preferred_element_type=F32)
        xe_ref[:, :, c * dc:(c + 1) * dc] = xe.reshape(eg, cap, dc).astype(xe_ref.dtype)


def gather_rows(slot, aff, h, cap, egroup=4, ncol=2):
    bsz, ne, seq = slot.shape
    d = h.shape[1]
    ngroups = ne // egroup
    grouped = lambda a: a.reshape(bsz, ngroups, egroup, seq)
    return pl.pallas_call(
        functools.partial(_gather_kernel, cap=cap, ncol=ncol),
        grid=(bsz, ngroups),
        in_specs=[pl.BlockSpec((None, None, egroup, seq), lambda b, g: (b, g, 0, 0)),
                  pl.BlockSpec((None, None, egroup, seq), lambda b, g: (b, g, 0, 0)),
                  pl.BlockSpec((seq, d), lambda b, g: (b, 0))],
        out_specs=[pl.BlockSpec((egroup, None, cap, d), lambda b, g: (g, b, 0, 0)),
                   pl.BlockSpec((egroup, None, cap, 1), lambda b, g: (g, b, 0, 0))],
        out_shape=[jax.ShapeDtypeStruct((ne, bsz, cap, d), BF16),
                   jax.ShapeDtypeStruct((ne, bsz, cap, 1), F32)],
        compiler_params=_params(("parallel", "parallel"), 48),
        name="moe_gather",
    )(grouped(slot), grouped(aff), h)


def _combine_kernel(slot_t_ref, ye_ref, x_ref, o_ref, *, cap):
    tb, ne = slot_t_ref.shape
    tn = x_ref.shape[1]
    cidx = lax.broadcasted_iota(jnp.int32, (tb, cap), 1)
    pt = jnp.concatenate([jnp.where(cidx == slot_t_ref[:, e:e + 1], 1.0, 0.0).astype(BF16) for e in range(ne)],
                         axis=1)
    ye = ye_ref[...].reshape(ne * cap, tn)
    o_ref[...] = x_ref[...] + jnp.dot(pt, ye, preferred_element_type=F32)


def combine(slot_t, ye, x2d, cap, tn=1024, tb=512):
    bsz, seq, ne = slot_t.shape
    d = x2d.shape[1]
    nt = seq // tb
    return pl.pallas_call(
        functools.partial(_combine_kernel, cap=cap),
        grid=(bsz, d // tn, nt),
        in_specs=[pl.BlockSpec((None, tb, ne), lambda b, n, t: (b, t, 0)),
                  pl.BlockSpec((ne, None, cap, tn), lambda b, n, t: (0, b, 0, n)),
                  pl.BlockSpec((tb, tn), lambda b, n, t: (b * nt + t, n))],
        out_specs=pl.BlockSpec((tb, tn), lambda b, n, t: (b * nt + t, n)),
        out_shape=jax.ShapeDtypeStruct(x2d.shape, F32),
        compiler_params=_params(("parallel", "parallel", "parallel"), 48),
        name="moe_combine",
    )(slot_t, ye, x2d)


def _expert_ffn_kernel(xe_ref, wg_ref, wu_ref, wd_ref, gate_ref, o_ref, acc_ref):
    f = pl.program_id(2)
    xe = xe_ref[...]
    g = jnp.dot(xe, wg_ref[...].astype(BF16), preferred_element_type=F32)
    u = jnp.dot(xe, wu_ref[...].astype(BF16), preferred_element_type=F32)
    hid = (jax.nn.silu(g) * u).astype(BF16)

    @pl.when(f == 0)
    def _():
        acc_ref[...] = jnp.zeros_like(acc_ref)

    acc_ref[...] += jnp.dot(hid, wd_ref[...].astype(BF16), preferred_element_type=F32)

    @pl.when(f == pl.num_programs(2) - 1)
    def _():
        o_ref[...] = (acc_ref[...] * gate_ref[...]).astype(o_ref.dtype)


def expert_ffn(xe, w_gate, w_up, w_down, layer, gate, tm=1024, tf=256):
    ne, r, d = xe.shape
    fdim = w_gate.shape[3]
    return pl.pallas_call(
        _expert_ffn_kernel,
        grid=(ne, r // tm, fdim // tf),
        in_specs=[pl.BlockSpec((None, tm, d), lambda e, m, f: (e, m, 0)),
                  pl.BlockSpec((None, None, d, tf), lambda e, m, f: (layer, e, 0, f)),
                  pl.BlockSpec((None, None, d, tf), lambda e, m, f: (layer, e, 0, f)),
                  pl.BlockSpec((None, None, tf, d), lambda e, m, f: (layer, e, f, 0)),
                  pl.BlockSpec((None, tm, 1), lambda e, m, f: (e, m, 0))],
        out_specs=pl.BlockSpec((None, tm, d), lambda e, m, f: (e, m, 0)),
        out_shape=jax.ShapeDtypeStruct((ne, r, d), BF16),
        scratch_shapes=[pltpu.VMEM((tm, d), F32)],
        compiler_params=_params(("parallel", "parallel", "arbitrary"), 56),
        name="expert_ffn",
    )(xe, w_gate, w_up, w_down, gate)


def _final_norm_kernel(x_ref, g_ref, o_ref):
    x = x_ref[...]
    ms = jnp.mean(x * x, axis=-1, keepdims=True)
    o_ref[...] = x * lax.rsqrt(ms + RMS_EPS) * g_ref[...]


def final_norm(x2d, g, tm=512):
    t, d = x2d.shape
    return pl.pallas_call(
        _final_norm_kernel,
        grid=(t // tm,),
        in_specs=[pl.BlockSpec((tm, d), lambda i: (i, 0)), pl.BlockSpec((1, d), lambda i: (0, 0))],
        out_specs=pl.BlockSpec((tm, d), lambda i: (i, 0)),
        out_shape=jax.ShapeDtypeStruct((t, d), F32),
        compiler_params=_params(("parallel",), 32),
        name="final_norm",
    )(x2d, g.reshape(1, d))


def expert_choice_moe(x2d, h, logits, bsz, seq, layer, w_gate, w_up, w_down):
    d = x2d.shape[1]
    ne = logits.shape[1]
    cap = EC_CAPACITY_FACTOR * seq // ne
    slot, aff = route(logits.T, bsz, seq, cap)
    xe, gate = gather_rows(slot, aff, h, cap)
    ye = expert_ffn(xe.reshape(ne, bsz * cap, d), w_gate, w_up, w_down, layer,
                    gate.reshape(ne, bsz * cap, 1))
    return combine(slot.transpose(0, 2, 1), ye.reshape(ne, bsz, cap, d), x2d, cap)


def kernel(x, norm_mix, norm_ffn, norm_final, w_in_ab, a_v_norm, a_spatial_w, a_spatial_b, b_rpb, w_out_ab,
           w_in_cd, c_conv, w_out_cd, router, w_gate, w_up, w_down):
    bsz, seq, d = x.shape
    depth = norm_mix.shape[0]
    half = d // 2
    nh = half // HEAD_DIM
    x2d = x.reshape(bsz * seq, d)
    for layer in range(depth):
        i = layer // 2
        if layer % 2 == 0:
            p = rms_matmul(x2d, norm_mix[layer], w_in_ab[i])
            p3 = p.reshape(bsz, seq, -1)
            ya = sgu(p, a_v_norm[i], a_spatial_w[i], a_spatial_b[i], half)
            yb = neighbourhood_attention(p3, b_rpb[i], 2 * half // HEAD_DIM, nh).reshape(bsz * seq, half)
            mix, w_out = (ya, yb), w_out_ab[i]
        else:
            p = rms_matmul(x2d, norm_mix[layer], w_in_cd[i])
            p3 = p.reshape(bsz, seq, -1)
            yc = gated_short_conv(p3, c_conv[i], half).reshape(bsz * seq, half)
            yd = dilated_attention(p3, 3 * half // HEAD_DIM, nh).reshape(bsz * seq, half)
            mix, w_out = (yc, yd), w_out_cd[i]
        x2d, h, logits = out_proj_norm_router(mix[0], mix[1], w_out.astype(BF16), x2d,
                                              norm_ffn[layer], router[layer])
        x2d = expert_choice_moe(x2d, h, logits, bsz, seq, layer, w_gate, w_up, w_down)
    return final_norm(x2d, norm_final).reshape(bsz, seq, d)
```

```python
import functools

import numpy as np
import jax
import jax.numpy as jnp
from jax import lax
from jax.experimental import pallas as pl
from jax.experimental.pallas import tpu as pltpu

F32 = jnp.float32
BF16 = jnp.bfloat16

HEAD_DIM = 128
CHUNK = 128
GRID_W = 64
NA_ROWS = 8
NA_COLS = 16
NA_QROWS = 4
NA_KROWS = NA_QROWS + NA_ROWS
CONV_W = 3
DIL_PATTERNS = ((128, 1), (512, 4), (2048, 16))
DIL_QBLK = 128
EPILOGUE_ROWS = 128
N_EXPERTS = 16
EC_CAPACITY_FACTOR = 2
RMS_EPS = 1e-6
LN_EPS = 1e-5
NEG_INF = -1e30
LOG2E = 1.4426950408889634
MIB = 1024 * 1024


def _params(semantics, vmem_mib):
    return pltpu.CompilerParams(dimension_semantics=semantics,
                                vmem_limit_bytes=int(vmem_mib * MIB))


def _rms_to_scratch(x_ref, g_ref, h_scr, rows):
    n = x_ref.shape[0] // rows

    def body(i, carry):
        r = pl.multiple_of(i * rows, rows)
        x = x_ref[pl.ds(r, rows), :]
        ms = jnp.mean(x * x, axis=-1, keepdims=True)
        h_scr[pl.ds(r, rows), :] = (x * lax.rsqrt(ms + RMS_EPS) * g_ref[...]).astype(h_scr.dtype)
        return carry

    lax.fori_loop(0, n, body, 0)


def _rms_matmul_kernel(x_ref, g_ref, w_ref, o_ref, h_scr):
    @pl.when(pl.program_id(1) == 0)
    def _():
        _rms_to_scratch(x_ref, g_ref, h_scr, 256)

    o_ref[...] = jnp.dot(h_scr[...], w_ref[...].astype(BF16), preferred_element_type=F32).astype(o_ref.dtype)


def rms_matmul(x2d, g, w, tm=1024, tn=1024):
    t, d = x2d.shape
    n = w.shape[1]
    return pl.pallas_call(
        _rms_matmul_kernel,
        grid=(t // tm, n // tn),
        in_specs=[pl.BlockSpec((tm, d), lambda i, j: (i, 0)),
                  pl.BlockSpec((1, d), lambda i, j: (0, 0)),
                  pl.BlockSpec((d, tn), lambda i, j: (0, j))],
        out_specs=pl.BlockSpec((tm, tn), lambda i, j: (i, j)),
        out_shape=jax.ShapeDtypeStruct((t, n), BF16),
        scratch_shapes=[pltpu.VMEM((tm, d), BF16)],
        compiler_params=_params(("parallel", "arbitrary"), 56),
        name="rms_in_proj",
    )(x2d, g.reshape(1, d), w)


def _sgu_kernel(u_ref, v_ref, lng_ref, ws_ref, bias_ref, o_ref):
    tm, width = u_ref.shape
    groups = width // HEAD_DIM
    nchunks = tm // CHUNK
    vf = jax.nn.gelu(v_ref[...].astype(F32))
    mu = jnp.mean(vf, axis=-1, keepdims=True)
    dv = vf - mu
    var = jnp.mean(dv * dv, axis=-1, keepdims=True)
    vn = (dv * lax.rsqrt(var + LN_EPS) * lng_ref[...]).astype(BF16)
    for g in range(groups):
        cols = slice(g * HEAD_DIM, (g + 1) * HEAD_DIM)
        vg = jnp.concatenate([vn[n * CHUNK:(n + 1) * CHUNK, cols] for n in range(nchunks)], axis=1)
        mixed = jnp.dot(ws_ref[g], vg, preferred_element_type=F32)
        for n in range(nchunks):
            rows = slice(n * CHUNK, (n + 1) * CHUNK)
            u = jax.nn.gelu(u_ref[rows, cols].astype(F32))
            o_ref[rows, cols] = (u * (mixed[:, n * CHUNK:(n + 1) * CHUNK] + bias_ref[:, cols])).astype(o_ref.dtype)


def sgu(p2d, ln_g, w_s, b_s, width, tm=512):
    t = p2d.shape[0]
    groups = width // HEAD_DIM
    bias_full = jnp.repeat(b_s.T.astype(F32), HEAD_DIM, axis=1)
    return pl.pallas_call(
        _sgu_kernel,
        grid=(t // tm,),
        in_specs=[pl.BlockSpec((tm, width), lambda i: (i, 0)),
                  pl.BlockSpec((tm, width), lambda i: (i, 1)),
                  pl.BlockSpec((1, width), lambda i: (0, 0)),
                  pl.BlockSpec((groups, CHUNK, CHUNK), lambda i: (0, 0, 0)),
                  pl.BlockSpec((CHUNK, width), lambda i: (0, 0))],
        out_specs=pl.BlockSpec((tm, width), lambda i: (i, 0)),
        out_shape=jax.ShapeDtypeStruct((t, width), BF16),
        compiler_params=_params(("parallel",), 40),
        name="sgu",
    )(p2d, p2d, ln_g.reshape(1, width).astype(F32), w_s.astype(BF16), bias_full)


def _na_block_layout(rows):
    nblk = rows // NA_QROWS
    kh = min(NA_ROWS, rows)
    starts, variants, var_key = [], [], {}
    for qb in range(nblk):
        r0 = qb * NA_QROWS
        ws = int(np.clip(r0 - kh // 2, 0, rows - NA_KROWS))
        rs = np.clip(np.arange(r0, r0 + NA_QROWS) - kh // 2, 0, rows - kh)
        key = (tuple(rs - ws), r0 - ws)
        if key not in var_key:
            var_key[key] = len(var_key)
        starts.append(ws)
        variants.append(var_key[key])
    return starts, variants, list(var_key.keys()), kh


def _na_tiles(rpb, rows):
    nh = rpb.shape[0]
    _, _, keys, kh = _na_block_layout(rows)
    qc = np.arange(GRID_W)[:, None]
    kc = np.arange(GRID_W)[None, :]
    col_start = np.clip(qc - NA_COLS // 2, 0, GRID_W - NA_COLS)
    col_valid = (kc >= col_start) & (kc < col_start + NA_COLS)
    dc_idx = np.clip(kc - qc + NA_COLS - 1, 0, 2 * NA_COLS - 2)
    tiles = jnp.where(jnp.asarray(col_valid)[None, None], (rpb.astype(F32) * LOG2E)[:, :, dc_idx], NEG_INF)
    tiles = jnp.concatenate([tiles, jnp.full((nh, 1, GRID_W, GRID_W), NEG_INF, F32)], axis=1)
    invalid = 2 * NA_ROWS - 1
    dr = np.full((len(keys), NA_QROWS, NA_KROWS), invalid, np.int32)
    for v, (rs_rel, r0_rel) in enumerate(keys):
        for i in range(NA_QROWS):
            for j in range(NA_KROWS):
                if rs_rel[i] <= j < rs_rel[i] + kh:
                    dr[v, i, j] = j - (r0_rel + i) + NA_ROWS - 1
    return tiles, dr


def _na_kernel(q_ref, k_ref, v_ref, tiles_ref, o_ref, bias_scr, *, starts, variants, tile_idx, scale):
    qblk = NA_QROWS * GRID_W
    kblk = NA_KROWS * GRID_W

    @pl.when(pl.program_id(1) == 0)
    def _():
        for var in range(tile_idx.shape[0]):
            for i in range(NA_QROWS):
                strip = jnp.concatenate([tiles_ref[int(tile_idx[var, i, j])] for j in range(NA_KROWS)], axis=1)
                bias_scr[var, i * GRID_W:(i + 1) * GRID_W, :] = strip

    for qb, (ws, var) in enumerate(zip(starts, variants)):
        q = (q_ref[qb * qblk:(qb + 1) * qblk, :].astype(F32) * scale).astype(BF16)
        k = k_ref[ws * GRID_W:ws * GRID_W + kblk, :]
        v = v_ref[ws * GRID_W:ws * GRID_W + kblk, :]
        s = lax.dot_general(q, k, (((1,), (1,)), ((), ())), preferred_element_type=F32)
        s = s + bias_scr[var]
        m = jnp.max(s, axis=-1, keepdims=True)
        e = jnp.exp2(s - m)
        l = jnp.sum(e, axis=-1, keepdims=True)
        o = jnp.dot(e.astype(BF16), v, preferred_element_type=F32)
        o_ref[qb * qblk:(qb + 1) * qblk, :] = (o / l).astype(o_ref.dtype)


def neighbourhood_attention(p3, rpb, col0, nh):
    bsz, seq, _ = p3.shape
    rows = seq // GRID_W
    starts, variants, keys, _ = _na_block_layout(rows)
    tiles, tile_idx = _na_tiles(rpb, rows)
    nvar = len(keys)
    ntiles = tiles.shape[1]
    qblk, kblk = NA_QROWS * GRID_W, NA_KROWS * GRID_W
    kern = functools.partial(_na_kernel, starts=starts, variants=variants, tile_idx=tile_idx,
                             scale=HEAD_DIM ** -0.5 * LOG2E)
    head_spec = lambda off: pl.BlockSpec((None, seq, HEAD_DIM), lambda h, b: (b, 0, col0 + off + h))
    return pl.pallas_call(
        kern,
        grid=(nh, bsz),
        in_specs=[head_spec(0), head_spec(nh), head_spec(2 * nh),
                  pl.BlockSpec((None, ntiles, GRID_W, GRID_W), lambda h, b: (h, 0, 0, 0))],
        out_specs=pl.BlockSpec((None, seq, HEAD_DIM), lambda h, b: (b, 0, h)),
        out_shape=jax.ShapeDtypeStruct((bsz, seq, nh * HEAD_DIM), BF16),
        scratch_shapes=[pltpu.VMEM((nvar, qblk, kblk), F32)],
        compiler_params=_params(("parallel", "arbitrary"), 32),
        name="neighbourhood_attention",
    )(p3, p3, p3, tiles)


def _conv_kernel(b_ref, c_ref, x_ref, taps_ref, o_ref):
    seq = b_ref.shape[0]
    z = c_ref[...].astype(F32) * x_ref[...].astype(F32)
    row = lax.broadcasted_iota(jnp.int32, z.shape, 0)
    z_prev = jnp.where(row == 0, 0.0, pltpu.roll(z, 1, axis=0))
    z_next = jnp.where(row == seq - 1, 0.0, pltpu.roll(z, seq - 1, axis=0))
    y = taps_ref[0:1, :] * z_prev + taps_ref[1:2, :] * z + taps_ref[2:3, :] * z_next
    o_ref[...] = (b_ref[...].astype(F32) * y).astype(o_ref.dtype)


def gated_short_conv(p3, taps, width, tc=256):
    bsz, seq, _ = p3.shape
    nblk = width // tc
    spec = lambda off: pl.BlockSpec((None, seq, tc), lambda b, j: (b, 0, off * nblk + j))
    return pl.pallas_call(
        _conv_kernel,
        grid=(bsz, nblk),
        in_specs=[spec(0), spec(1), spec(2), pl.BlockSpec((CONV_W, tc), lambda b, j: (0, j))],
        out_specs=pl.BlockSpec((None, seq, tc), lambda b, j: (b, 0, j)),
        out_shape=jax.ShapeDtypeStruct((bsz, seq, width), BF16),
        compiler_params=_params(("parallel", "parallel"), 40),
        name="gated_short_conv",
    )(p3, p3, p3, taps.astype(F32))


def _dil_windows(length):
    nk = min(2 * DIL_QBLK, length)
    out = []
    for i in range(length // DIL_QBLK):
        q0 = i * DIL_QBLK
        k0 = int(np.clip(q0 - DIL_QBLK // 2, 0, length - nk))
        out.append((q0, k0, nk))
    return out


def _dil_tables(seq, nh):
    slopes = np.array([2.0 ** (-8.0 * (h + 1) / nh) for h in range(nh)], dtype=np.float32)
    var_index, absd_tabs, dil_of = {}, [], []
    plan = []
    for window, dil in DIL_PATTERNS:
        radius = window // (2 * dil)
        length = seq // dil
        blocks = []
        for q0, k0, nk in _dil_windows(length):
            key = (dil, radius, k0 - q0, nk)
            if key not in var_index:
                var_index[key] = len(absd_tabs)
                delta = (k0 + np.arange(2 * DIL_QBLK)[None, :]) - (q0 + np.arange(DIL_QBLK)[:, None])
                absd = np.abs(delta).astype(np.float32)
                valid = (np.abs(delta) <= radius) & (np.arange(2 * DIL_QBLK)[None, :] < nk)
                absd_tabs.append((absd * dil, valid))
            blocks.append((q0, k0, nk, var_index[key]))
        plan.append((dil, length, blocks))
    pen = np.stack([a for a, _ in absd_tabs])
    valid = np.stack([v for _, v in absd_tabs])
    bias = np.where(valid[None], -(slopes[:, None, None, None] * pen[None]).astype(np.float32), np.float32(NEG_INF))
    return plan, bias.astype(np.float32)


def _merge_lse2(oa, la, ob, lb):
    m = jnp.maximum(la, lb)
    wa = jnp.exp2(la - m)
    wb = jnp.exp2(lb - m)
    den = wa + wb
    return (wa * oa + wb * ob) / den, m + jnp.log2(den)


def _dil_kernel(q_ref, k_ref, v_ref, bias_ref, o_ref, qf, kf, vf, qd, kd, vd, o1, l1, od, ld, oc, lc, on, ln, *,
                plan, scale):
    (_, _, blocks1), (d, len_d, blocks_d), (_, len_c, blocks_c) = plan
    qf[...] = q_ref[...].astype(F32) * scale
    kf[...] = k_ref[...].astype(F32)
    vf[...] = v_ref[...].astype(F32)

    def attend(q, k, v, var, nk):
        s = lax.dot_general(q, k, (((1,), (1,)), ((), ())), preferred_element_type=F32)
        s = s + bias_ref[var][:, :nk]
        m = jnp.max(s, axis=-1, keepdims=True)
        e = jnp.exp2(s - m)
        l = jnp.sum(e, axis=-1, keepdims=True)
        o = jnp.dot(e.astype(BF16), v, preferred_element_type=F32) / l
        return o, jnp.broadcast_to(m + jnp.log2(l), o.shape)

    for q0, k0, nk, var in blocks1:
        o, lse = attend(qf[q0:q0 + DIL_QBLK, :].astype(BF16), kf[k0:k0 + nk, :].astype(BF16),
                        vf[k0:k0 + nk, :].astype(BF16), var, nk)
        o1[q0:q0 + DIL_QBLK, :] = o
        l1[q0:q0 + DIL_QBLK, :] = lse

    for r in range(d):
        qd[r] = qf[pl.ds(r, len_d, stride=d), :]
        kd[r] = kf[pl.ds(r, len_d, stride=d), :]
        vd[r] = vf[pl.ds(r, len_d, stride=d), :]

    for r in range(d):
        qs, ks, vs = qd[r].astype(BF16), kd[r].astype(BF16), vd[r].astype(BF16)
        for q0, k0, nk, var in blocks_d:
            o, lse = attend(qs[q0:q0 + DIL_QBLK], ks[k0:k0 + nk], vs[k0:k0 + nk], var, nk)
            od[r, q0:q0 + DIL_QBLK, :] = o
            ld[r, q0:q0 + DIL_QBLK, :] = lse
        for a in range(d):
            qs = qd[r, pl.ds(a, len_c, stride=d), :].astype(BF16)
            ks = kd[r, pl.ds(a, len_c, stride=d), :].astype(BF16)
            vs = vd[r, pl.ds(a, len_c, stride=d), :].astype(BF16)
            for q0, k0, nk, var in blocks_c:
                o, lse = attend(qs[q0:q0 + DIL_QBLK], ks[k0:k0 + nk], vs[k0:k0 + nk], var, nk)
                oc[r, pl.ds(q0 * d + a, DIL_QBLK, stride=d), :] = o
                lc[r, pl.ds(q0 * d + a, DIL_QBLK, stride=d), :] = lse

    for r in range(d):
        o, lse = _merge_lse2(od[r], ld[r], oc[r], lc[r])
        on[pl.ds(r, len_d, stride=d), :] = o
        ln[pl.ds(r, len_d, stride=d), :] = lse

    o, _ = _merge_lse2(o1[...], l1[...], on[...], ln[...])
    o_ref[...] = o.astype(o_ref.dtype)


def dilated_attention(p3, col0, nh):
    bsz, seq, _ = p3.shape
    plan, bias_np = _dil_tables(seq, nh)
    dils = [dil for dil, _, _ in plan]
    assert len(dils) == 3 and dils[0] == 1 and dils[2] == dils[1] * dils[1], dils
    d = dils[1]
    nvar = bias_np.shape[1]
    kern = functools.partial(_dil_kernel, plan=plan, scale=HEAD_DIM ** -0.5 * LOG2E)
    head_spec = lambda off: pl.BlockSpec((None, seq, HEAD_DIM), lambda h, b: (b, 0, col0 + off + h))
    natural = pltpu.VMEM((seq, HEAD_DIM), F32)
    by_residue = pltpu.VMEM((d, seq // d, HEAD_DIM), F32)
    return pl.pallas_call(
        kern,
        grid=(nh, bsz),
        in_specs=[head_spec(0), head_spec(nh), head_spec(2 * nh),
                  pl.BlockSpec((None, nvar, DIL_QBLK, 2 * DIL_QBLK), lambda h, b: (h, 0, 0, 0))],
        out_specs=pl.BlockSpec((None, seq, HEAD_DIM), lambda h, b: (b, 0, h)),
        out_shape=jax.ShapeDtypeStruct((bsz, seq, nh * HEAD_DIM), BF16),
        scratch_shapes=[natural] * 3 + [by_residue] * 3 + [natural] * 2 + [by_residue] * 4 + [natural] * 2,
        compiler_params=_params(("parallel", "parallel"), 40),
        name="dilated_attention",
    )(p3, p3, p3, jnp.asarray(bias_np * np.float32(LOG2E)))


def _out_proj_router_kernel(ya_ref, yb_ref, wa_ref, wb_ref, x_ref, g_ref, r_ref, xo_ref, h_ref, lg_ref):
    acc = jnp.dot(ya_ref[...], wa_ref[...], preferred_element_type=F32)
    acc = acc + jnp.dot(yb_ref[...], wb_ref[...], preferred_element_type=F32)
    ne = lg_ref.shape[1]
    for c in range(acc.shape[0] // EPILOGUE_ROWS):
        rows = slice(c * EPILOGUE_ROWS, (c + 1) * EPILOGUE_ROWS)
        xn = x_ref[rows, :] + acc[rows, :]
        xo_ref[rows, :] = xn
        ms = jnp.mean(xn * xn, axis=-1, keepdims=True)
        h = xn * lax.rsqrt(ms + RMS_EPS) * g_ref[...]
        h_hi = h.astype(BF16)
        h_ref[rows, :] = h_hi
        h_lo = (h - h_hi.astype(F32)).astype(BF16)
        s = (jnp.dot(h_hi, r_ref[...], preferred_element_type=F32)
             + jnp.dot(h_lo, r_ref[...], preferred_element_type=F32))
        lg_ref[rows, :] = s[:, :ne] + s[:, ne:]


def out_proj_norm_router(ya, yb, w_bf16, x2d, g, router, tm=512):
    t, d = x2d.shape
    ka = ya.shape[1]
    kb = yb.shape[1]
    assert ka == kb
    ne = router.shape[1]
    r_hi = router.astype(BF16)
    r_lo = (router - r_hi.astype(F32)).astype(BF16)
    r_split = jnp.concatenate([r_hi, r_lo], axis=1)
    return pl.pallas_call(
        _out_proj_router_kernel,
        grid=(t // tm,),
        in_specs=[pl.BlockSpec((tm, ka), lambda i: (i, 0)),
                  pl.BlockSpec((tm, kb), lambda i: (i, 0)),
                  pl.BlockSpec((ka, d), lambda i: (0, 0)),
                  pl.BlockSpec((kb, d), lambda i: (1, 0)),
                  pl.BlockSpec((tm, d), lambda i: (i, 0)),
                  pl.BlockSpec((1, d), lambda i: (0, 0)),
                  pl.BlockSpec((d, 2 * ne), lambda i: (0, 0))],
        out_specs=[pl.BlockSpec((tm, d), lambda i: (i, 0)),
                   pl.BlockSpec((tm, d), lambda i: (i, 0)),
                   pl.BlockSpec((tm, ne), lambda i: (i, 0))],
        out_shape=[jax.ShapeDtypeStruct((t, d), F32),
                   jax.ShapeDtypeStruct((t, d), BF16),
                   jax.ShapeDtypeStruct((t, ne), F32)],
        compiler_params=_params(("parallel",), 56),
        name="out_proj_norm_router",
    )(ya, yb, w_bf16, w_bf16, x2d, g.reshape(1, d), r_split)


def _route_kernel(lg_ref, slot_ref, aff_ref, *, cap):
    logits = lg_ref[...]
    mx = jnp.max(logits, axis=0, keepdims=True)
    ex = jnp.exp(logits - mx)
    a = ex / jnp.sum(ex, axis=0, keepdims=True)
    aff_ref[...] = a
    ne, seq = a.shape
    lanes = 128
    r = lax.broadcasted_iota(jnp.int32, (lanes, lanes), 0)
    c = lax.broadcasted_iota(jnp.int32, (lanes, lanes), 1)
    tri = jnp.where(r <= c, 1.0, 0.0).astype(BF16)

    def count(mask_f):
        return jnp.sum(mask_f, axis=1, keepdims=True)

    def prefix(x):
        parts = []
        run = jnp.zeros((ne, 1), F32)
        for j in range(seq // lanes):
            blk = x[:, j * lanes:(j + 1) * lanes]
            inc = jnp.dot(blk.astype(BF16), tri, preferred_element_type=F32)
            parts.append(inc - blk + run)
            run = run + count(blk)
        return jnp.concatenate(parts, axis=1)

    def body(i, ans):
        cand = ans | jnp.left_shift(jnp.int32(1), 30 - i)
        cnt = count(jnp.where(a >= lax.bitcast_convert_type(cand, F32), 1.0, 0.0))
        return jnp.where(cnt >= cap, cand, ans)

    ans = lax.fori_loop(0, 31, body, jnp.zeros((ne, 1), jnp.int32))
    thr = lax.bitcast_convert_type(ans, F32)
    gt = jnp.where(a > thr, 1.0, 0.0)
    eq = jnp.where(a >= thr, 1.0, 0.0) - gt
    need = cap - count(gt)
    sel = gt + eq * jnp.where(prefix(eq) < need, 1.0, 0.0)
    slot_ref[...] = jnp.where(sel > 0.5, prefix(sel), -1.0).astype(jnp.int32)


def route(logits_t, bsz, seq, cap):
    ne = logits_t.shape[0]
    return pl.pallas_call(
        functools.partial(_route_kernel, cap=cap),
        grid=(bsz,),
        in_specs=[pl.BlockSpec((ne, seq), lambda b: (0, b))],
        out_specs=[pl.BlockSpec((None, ne, seq), lambda b: (b, 0, 0)),
                   pl.BlockSpec((None, ne, seq), lambda b: (b, 0, 0))],
        out_shape=[jax.ShapeDtypeStruct((bsz, ne, seq), jnp.int32),
                   jax.ShapeDtypeStruct((bsz, ne, seq), F32)],
        compiler_params=_params(("parallel",), 32),
        name="route",
    )(logits_t)


def _gather_kernel(slot_ref, aff_ref, h_ref, xe_ref, gate_ref, *, cap, ncol):
    eg, seq = slot_ref.shape
    d = h_ref.shape[1]
    cidx = lax.broadcasted_iota(jnp.int32, (cap, seq), 0)
    rows = []
    for j in range(eg):
        hit = cidx == slot_ref[j:j + 1, :]
        rows.append(jnp.where(hit, 1.0, 0.0).astype(BF16))
        gate_ref[j] = jnp.sum(jnp.where(hit, aff_ref[j:j + 1, :], 0.0), axis=1, keepdims=True)
    p = jnp.concatenate(rows, axis=0)
    dc = d // ncol
    for c in range(ncol):
        xe = jnp.dot(p, h_ref[:, c * dc:(c + 1) * dc], preferred_element_type=F32)
        xe_ref[:, :, c * dc:(c + 1) * dc] = xe.reshape(eg, cap, dc).astype(xe_ref.dtype)


def gather_rows(slot, aff, h, cap, egroup=4, ncol=2):
    bsz, ne, seq = slot.shape
    d = h.shape[1]
    ngroups = ne // egroup
    grouped = lambda a: a.reshape(bsz, ngroups, egroup, seq)
    return pl.pallas_call(
        functools.partial(_gather_kernel, cap=cap, ncol=ncol),
        grid=(bsz, ngroups),
        in_specs=[pl.BlockSpec((None, None, egroup, seq), lambda b, g: (b, g, 0, 0)),
                  pl.BlockSpec((None, None, egroup, seq), lambda b, g: (b, g, 0, 0)),
                  pl.BlockSpec((seq, d), lambda b, g: (b, 0))],
        out_specs=[pl.BlockSpec((egroup, None, cap, d), lambda b, g: (g, b, 0, 0)),
                   pl.BlockSpec((egroup, None, cap, 1), lambda b, g: (g, b, 0, 0))],
        out_shape=[jax.ShapeDtypeStruct((ne, bsz, cap, d), BF16),
                   jax.ShapeDtypeStruct((ne, bsz, cap, 1), F32)],
        compiler_params=_params(("parallel", "parallel"), 48),
        name="moe_gather",
    )(grouped(slot), grouped(aff), h)


def _combine_kernel(slot_t_ref, ye_ref, x_ref, o_ref, *, cap):
    tb, ne = slot_t_ref.shape
    tn = x_ref.shape[1]
    cidx = lax.broadcasted_iota(jnp.int32, (tb, cap), 1)
    pt = jnp.concatenate([jnp.where(cidx == slot_t_ref[:, e:e + 1], 1.0, 0.0).astype(BF16) for e in range(ne)],
                         axis=1)
    ye = ye_ref[...].reshape(ne * cap, tn)
    o_ref[...] = x_ref[...] + jnp.dot(pt, ye, preferred_element_type=F32)


def combine(slot_t, ye, x2d, cap, tn=1024, tb=512):
    bsz, seq, ne = slot_t.shape
    d = x2d.shape[1]
    nt = seq // tb
    return pl.pallas_call(
        functools.partial(_combine_kernel, cap=cap),
        grid=(bsz, d // tn, nt),
        in_specs=[pl.BlockSpec((None, tb, ne), lambda b, n, t: (b, t, 0)),
                  pl.BlockSpec((ne, None, cap, tn), lambda b, n, t: (0, b, 0, n)),
                  pl.BlockSpec((tb, tn), lambda b, n, t: (b * nt + t, n))],
        out_specs=pl.BlockSpec((tb, tn), lambda b, n, t: (b * nt + t, n)),
        out_shape=jax.ShapeDtypeStruct(x2d.shape, F32),
        compiler_params=_params(("parallel", "parallel", "parallel"), 48),
        name="moe_combine",
    )(slot_t, ye, x2d)


def _expert_ffn_kernel(xe_ref, wg_ref, wu_ref, wd_ref, gate_ref, o_ref, acc_ref):
    f = pl.program_id(2)
    xe = xe_ref[...]
    g = jnp.dot(xe, wg_ref[...].astype(BF16), preferred_element_type=F32)
    u = jnp.dot(xe, wu_ref[...].astype(BF16), preferred_element_type=F32)
    hid = (jax.nn.silu(g) * u).astype(BF16)

    @pl.when(f == 0)
    def _():
        acc_ref[...] = jnp.zeros_like(acc_ref)

    acc_ref[...] += jnp.dot(hid, wd_ref[...].astype(BF16), preferred_element_type=F32)

    @pl.when(f == pl.num_programs(2) - 1)
    def _():
        o_ref[...] = (acc_ref[...] * gate_ref[...]).astype(o_ref.dtype)


def expert_ffn(xe, w_gate, w_up, w_down, layer, gate, tm=1024, tf=256):
    ne, r, d = xe.shape
    fdim = w_gate.shape[3]
    return pl.pallas_call(
        _expert_ffn_kernel,
        grid=(ne, r // tm, fdim // tf),
        in_specs=[pl.BlockSpec((None, tm, d), lambda e, m, f: (e, m, 0)),
                  pl.BlockSpec((None, None, d, tf), lambda e, m, f: (layer, e, 0, f)),
                  pl.BlockSpec((None, None, d, tf), lambda e, m, f: (layer, e, 0, f)),
                  pl.BlockSpec((None, None, tf, d), lambda e, m, f: (layer, e, f, 0)),
                  pl.BlockSpec((None, tm, 1), lambda e, m, f: (e, m, 0))],
        out_specs=pl.BlockSpec((None, tm, d), lambda e, m, f: (e, m, 0)),
        out_shape=jax.ShapeDtypeStruct((ne, r, d), BF16),
        scratch_shapes=[pltpu.VMEM((tm, d), F32)],
        compiler_params=_params(("parallel", "parallel", "arbitrary"), 56),
        name="expert_ffn",
    )(xe, w_gate, w_up, w_down, gate)


def _final_norm_kernel(x_ref, g_ref, o_ref):
    x = x_ref[...]
    ms = jnp.mean(x * x, axis=-1, keepdims=True)
    o_ref[...] = x * lax.rsqrt(ms + RMS_EPS) * g_ref[...]


def final_norm(x2d, g, tm=512):
    t, d = x2d.shape
    return pl.pallas_call(
        _final_norm_kernel,
        grid=(t // tm,),
        in_specs=[pl.BlockSpec((tm, d), lambda i: (i, 0)), pl.BlockSpec((1, d), lambda i: (0, 0))],
        out_specs=pl.BlockSpec((tm, d), lambda i: (i, 0)),
        out_shape=jax.ShapeDtypeStruct((t, d), F32),
        compiler_params=_params(("parallel",), 32),
        name="final_norm",
    )(x2d, g.reshape(1, d))


def expert_choice_moe(x2d, h, logits, bsz, seq, layer, w_gate, w_up, w_down):
    d = x2d.shape[1]
    ne = logits.shape[1]
    cap = EC_CAPACITY_FACTOR * seq // ne
    slot, aff = route(logits.T, bsz, seq, cap)
    xe, gate = gather_rows(slot, aff, h, cap)
    ye = expert_ffn(xe.reshape(ne, bsz * cap, d), w_gate, w_up, w_down, layer,
                    gate.reshape(ne, bsz * cap, 1))
    return combine(slot.transpose(0, 2, 1), ye.reshape(ne, bsz, cap, d), x2d, cap)


def kernel(x, norm_mix, norm_ffn, norm_final, w_in_ab, a_v_norm, a_spatial_w, a_spatial_b, b_rpb, w_out_ab,
           w_in_cd, c_conv, w_out_cd, router, w_gate, w_up, w_down):
    bsz, seq, d = x.shape
    depth = norm_mix.shape[0]
    half = d // 2
    nh = half // HEAD_DIM
    x2d = x.reshape(bsz * seq, d)
    for layer in range(depth):
        i = layer // 2
        if layer % 2 == 0:
            p = rms_matmul(x2d, norm_mix[layer], w_in_ab[i])
            p3 = p.reshape(bsz, seq, -1)
            ya = sgu(p, a_v_norm[i], a_spatial_w[i], a_spatial_b[i], half)
            yb = neighbourhood_attention(p3, b_rpb[i], 2 * half // HEAD_DIM, nh).reshape(bsz * seq, half)
            mix, w_out = (ya, yb), w_out_ab[i]
        else:
            p = rms_matmul(x2d, norm_mix[layer], w_in_cd[i])
            p3 = p.reshape(bsz, seq, -1)
            yc = gated_short_conv(p3, c_conv[i], half).reshape(bsz * seq, half)
            yd = dilated_attention(p3, 3 * half // HEAD_DIM, nh).reshape(bsz * seq, half)
            mix, w_out = (yc, yd), w_out_cd[i]
        x2d, h, logits = out_proj_norm_router(mix[0], mix[1], w_out.astype(BF16), x2d,
                                              norm_ffn[layer], router[layer])
        x2d = expert_choice_moe(x2d, h, logits, bsz, seq, layer, w_gate, w_up, w_down)
    return final_norm(x2d, norm_final).reshape(bsz, seq, d)
```

```python
import functools

import numpy as np
import jax
import jax.numpy as jnp
from jax import lax
from jax.experimental import pallas as pl
from jax.experimental.pallas import tpu as pltpu

F32 = jnp.float32
BF16 = jnp.bfloat16

HEAD_DIM = 128
CHUNK = 128
GRID_W = 64
NA_ROWS = 8
NA_COLS = 16
NA_QROWS = 4
NA_KROWS = NA_QROWS + NA_ROWS
CONV_W = 3
DIL_PATTERNS = ((128, 1), (512, 4), (2048, 16))
DIL_QBLK = 128
EPILOGUE_ROWS = 128
FFN_ROW_CHUNKS = 2
N_EXPERTS = 16
EC_CAPACITY_FACTOR = 2
RMS_EPS = 1e-6
LN_EPS = 1e-5
NEG_INF = -1e30
LOG2E = 1.4426950408889634
MIB = 1024 * 1024


def _params(semantics, vmem_mib):
    return pltpu.CompilerParams(dimension_semantics=semantics,
                                vmem_limit_bytes=int(vmem_mib * MIB))


def _rms_to_scratch(x_ref, g_ref, h_scr, rows):
    n = x_ref.shape[0] // rows

    def body(i, carry):
        r = pl.multiple_of(i * rows, rows)
        x = x_ref[pl.ds(r, rows), :]
        ms = jnp.mean(x * x, axis=-1, keepdims=True)
        h_scr[pl.ds(r, rows), :] = (x * lax.rsqrt(ms + RMS_EPS) * g_ref[...]).astype(h_scr.dtype)
        return carry

    lax.fori_loop(0, n, body, 0)


def _rms_matmul_kernel(x_ref, g_ref, w_ref, o_ref, h_scr):
    @pl.when(pl.program_id(1) == 0)
    def _():
        _rms_to_scratch(x_ref, g_ref, h_scr, 256)

    o_ref[...] = jnp.dot(h_scr[...], w_ref[...].astype(BF16), preferred_element_type=F32).astype(o_ref.dtype)


def rms_matmul(x2d, g, w, tm=1024, tn=1024):
    t, d = x2d.shape
    n = w.shape[1]
    return pl.pallas_call(
        _rms_matmul_kernel,
        grid=(t // tm, n // tn),
        in_specs=[pl.BlockSpec((tm, d), lambda i, j: (i, 0)),
                  pl.BlockSpec((1, d), lambda i, j: (0, 0)),
                  pl.BlockSpec((d, tn), lambda i, j: (0, j))],
        out_specs=pl.BlockSpec((tm, tn), lambda i, j: (i, j)),
        out_shape=jax.ShapeDtypeStruct((t, n), BF16),
        scratch_shapes=[pltpu.VMEM((tm, d), BF16)],
        compiler_params=_params(("parallel", "arbitrary"), 56),
        name="rms_in_proj",
    )(x2d, g.reshape(1, d), w)


def _sgu_kernel(u_ref, v_ref, lng_ref, ws_ref, bias_ref, o_ref):
    tm, width = u_ref.shape
    groups = width // HEAD_DIM
    nchunks = tm // CHUNK
    vf = jax.nn.gelu(v_ref[...].astype(F32))
    mu = jnp.mean(vf, axis=-1, keepdims=True)
    dv = vf - mu
    var = jnp.mean(dv * dv, axis=-1, keepdims=True)
    vn = (dv * lax.rsqrt(var + LN_EPS) * lng_ref[...]).astype(BF16)
    for g in range(groups):
        cols = slice(g * HEAD_DIM, (g + 1) * HEAD_DIM)
        vg = jnp.concatenate([vn[n * CHUNK:(n + 1) * CHUNK, cols] for n in range(nchunks)], axis=1)
        mixed = jnp.dot(ws_ref[g], vg, preferred_element_type=F32)
        for n in range(nchunks):
            rows = slice(n * CHUNK, (n + 1) * CHUNK)
            u = jax.nn.gelu(u_ref[rows, cols].astype(F32))
            o_ref[rows, cols] = (u * (mixed[:, n * CHUNK:(n + 1) * CHUNK] + bias_ref[:, cols])).astype(o_ref.dtype)


def sgu(p2d, ln_g, w_s, b_s, width, tm=512):
    t = p2d.shape[0]
    groups = width // HEAD_DIM
    bias_full = jnp.repeat(b_s.T.astype(F32), HEAD_DIM, axis=1)
    return pl.pallas_call(
        _sgu_kernel,
        grid=(t // tm,),
        in_specs=[pl.BlockSpec((tm, width), lambda i: (i, 0)),
                  pl.BlockSpec((tm, width), lambda i: (i, 1)),
                  pl.BlockSpec((1, width), lambda i: (0, 0)),
                  pl.BlockSpec((groups, CHUNK, CHUNK), lambda i: (0, 0, 0)),
                  pl.BlockSpec((CHUNK, width), lambda i: (0, 0))],
        out_specs=pl.BlockSpec((tm, width), lambda i: (i, 0)),
        out_shape=jax.ShapeDtypeStruct((t, width), BF16),
        compiler_params=_params(("parallel",), 40),
        name="sgu",
    )(p2d, p2d, ln_g.reshape(1, width).astype(F32), w_s.astype(BF16), bias_full)


def _na_block_layout(rows):
    nblk = rows // NA_QROWS
    kh = min(NA_ROWS, rows)
    starts, variants, var_key = [], [], {}
    for qb in range(nblk):
        r0 = qb * NA_QROWS
        ws = int(np.clip(r0 - kh // 2, 0, rows - NA_KROWS))
        rs = np.clip(np.arange(r0, r0 + NA_QROWS) - kh // 2, 0, rows - kh)
        key = (tuple(rs - ws), r0 - ws)
        if key not in var_key:
            var_key[key] = len(var_key)
        starts.append(ws)
        variants.append(var_key[key])
    return starts, variants, list(var_key.keys()), kh


def _na_tiles(rpb, rows):
    nh = rpb.shape[0]
    _, _, keys, kh = _na_block_layout(rows)
    qc = np.arange(GRID_W)[:, None]
    kc = np.arange(GRID_W)[None, :]
    col_start = np.clip(qc - NA_COLS // 2, 0, GRID_W - NA_COLS)
    col_valid = (kc >= col_start) & (kc < col_start + NA_COLS)
    dc_idx = np.clip(kc - qc + NA_COLS - 1, 0, 2 * NA_COLS - 2)
    onehot = (dc_idx[None] == np.arange(2 * NA_COLS - 1)[:, None, None]).astype(np.float32)
    expanded = jnp.einsum('hrc,cqk->hrqk', rpb.astype(F32) * LOG2E, jnp.asarray(onehot),
                          precision=lax.Precision.HIGHEST)
    tiles = jnp.where(jnp.asarray(col_valid)[None, None], expanded, NEG_INF)
    tiles = jnp.concatenate([tiles, jnp.full((nh, 1, GRID_W, GRID_W), NEG_INF, F32)], axis=1)
    invalid = 2 * NA_ROWS - 1
    dr = np.full((len(keys), NA_QROWS, NA_KROWS), invalid, np.int32)
    for v, (rs_rel, r0_rel) in enumerate(keys):
        for i in range(NA_QROWS):
            for j in range(NA_KROWS):
                if rs_rel[i] <= j < rs_rel[i] + kh:
                    dr[v, i, j] = j - (r0_rel + i) + NA_ROWS - 1
    return tiles, dr


def _na_kernel(q_ref, k_ref, v_ref, tiles_ref, o_ref, bias_scr, *, starts, variants, tile_idx, scale):
    qblk = NA_QROWS * GRID_W
    kblk = NA_KROWS * GRID_W

    @pl.when(pl.program_id(1) == 0)
    def _():
        for var in range(tile_idx.shape[0]):
            for i in range(NA_QROWS):
                strip = jnp.concatenate([tiles_ref[int(tile_idx[var, i, j])] for j in range(NA_KROWS)], axis=1)
                bias_scr[var, i * GRID_W:(i + 1) * GRID_W, :] = strip

    for qb, (ws, var) in enumerate(zip(starts, variants)):
        q = (q_ref[qb * qblk:(qb + 1) * qblk, :].astype(F32) * scale).astype(BF16)
        k = k_ref[ws * GRID_W:ws * GRID_W + kblk, :]
        v = v_ref[ws * GRID_W:ws * GRID_W + kblk, :]
        s = lax.dot_general(q, k, (((1,), (1,)), ((), ())), preferred_element_type=F32)
        s = s + bias_scr[var]
        m = jnp.max(s, axis=-1, keepdims=True)
        e = jnp.exp2(s - m)
        l = jnp.sum(e, axis=-1, keepdims=True)
        o = jnp.dot(e.astype(BF16), v, preferred_element_type=F32)
        o_ref[qb * qblk:(qb + 1) * qblk, :] = (o / l).astype(o_ref.dtype)


def neighbourhood_attention(p3, rpb, col0, nh):
    bsz, seq, _ = p3.shape
    rows = seq // GRID_W
    starts, variants, keys, _ = _na_block_layout(rows)
    tiles, tile_idx = _na_tiles(rpb, rows)
    nvar = len(keys)
    ntiles = tiles.shape[1]
    qblk, kblk = NA_QROWS * GRID_W, NA_KROWS * GRID_W
    kern = functools.partial(_na_kernel, starts=starts, variants=variants, tile_idx=tile_idx,
                             scale=HEAD_DIM ** -0.5 * LOG2E)
    head_spec = lambda off: pl.BlockSpec((None, seq, HEAD_DIM), lambda h, b: (b, 0, col0 + off + h))
    return pl.pallas_call(
        kern,
        grid=(nh, bsz),
        in_specs=[head_spec(0), head_spec(nh), head_spec(2 * nh),
                  pl.BlockSpec((None, ntiles, GRID_W, GRID_W), lambda h, b: (h, 0, 0, 0))],
        out_specs=pl.BlockSpec((None, seq, HEAD_DIM), lambda h, b: (b, 0, h)),
        out_shape=jax.ShapeDtypeStruct((bsz, seq, nh * HEAD_DIM), BF16),
        scratch_shapes=[pltpu.VMEM((nvar, qblk, kblk), F32)],
        compiler_params=_params(("parallel", "arbitrary"), 32),
        name="neighbourhood_attention",
    )(p3, p3, p3, tiles)


def _conv_kernel(b_ref, c_ref, x_ref, taps_ref, o_ref):
    seq = b_ref.shape[0]
    z = c_ref[...].astype(F32) * x_ref[...].astype(F32)
    row = lax.broadcasted_iota(jnp.int32, z.shape, 0)
    z_prev = jnp.where(row == 0, 0.0, pltpu.roll(z, 1, axis=0))
    z_next = jnp.where(row == seq - 1, 0.0, pltpu.roll(z, seq - 1, axis=0))
    y = taps_ref[0:1, :] * z_prev + taps_ref[1:2, :] * z + taps_ref[2:3, :] * z_next
    o_ref[...] = (b_ref[...].astype(F32) * y).astype(o_ref.dtype)


def gated_short_conv(p3, taps, width, tc=256):
    bsz, seq, _ = p3.shape
    nblk = width // tc
    spec = lambda off: pl.BlockSpec((None, seq, tc), lambda b, j: (b, 0, off * nblk + j))
    return pl.pallas_call(
        _conv_kernel,
        grid=(bsz, nblk),
        in_specs=[spec(0), spec(1), spec(2), pl.BlockSpec((CONV_W, tc), lambda b, j: (0, j))],
        out_specs=pl.BlockSpec((None, seq, tc), lambda b, j: (b, 0, j)),
        out_shape=jax.ShapeDtypeStruct((bsz, seq, width), BF16),
        compiler_params=_params(("parallel", "parallel"), 40),
        name="gated_short_conv",
    )(p3, p3, p3, taps.astype(F32))


def _dil_windows(length):
    nk = min(2 * DIL_QBLK, length)
    out = []
    for i in range(length // DIL_QBLK):
        q0 = i * DIL_QBLK
        k0 = int(np.clip(q0 - DIL_QBLK // 2, 0, length - nk))
        out.append((q0, k0, nk))
    return out


def _dil_tables(seq, nh):
    slopes = np.array([2.0 ** (-8.0 * (h + 1) / nh) for h in range(nh)], dtype=np.float32)
    var_index, absd_tabs, dil_of = {}, [], []
    plan = []
    for window, dil in DIL_PATTERNS:
        radius = window // (2 * dil)
        length = seq // dil
        blocks = []
        for q0, k0, nk in _dil_windows(length):
            key = (dil, radius, k0 - q0, nk)
            if key not in var_index:
                var_index[key] = len(absd_tabs)
                delta = (k0 + np.arange(2 * DIL_QBLK)[None, :]) - (q0 + np.arange(DIL_QBLK)[:, None])
                absd = np.abs(delta).astype(np.float32)
                valid = (np.abs(delta) <= radius) & (np.arange(2 * DIL_QBLK)[None, :] < nk)
                absd_tabs.append((absd * dil, valid))
            blocks.append((q0, k0, nk, var_index[key]))
        plan.append((dil, length, blocks))
    pen = np.stack([a for a, _ in absd_tabs])
    valid = np.stack([v for _, v in absd_tabs])
    bias = np.where(valid[None], -(slopes[:, None, None, None] * pen[None]).astype(np.float32), np.float32(NEG_INF))
    return plan, bias.astype(np.float32)


def _merge_lse2(oa, la, ob, lb):
    m = jnp.maximum(la, lb)
    wa = jnp.exp2(la - m)
    wb = jnp.exp2(lb - m)
    den = wa + wb
    return (wa * oa + wb * ob) / den, m + jnp.log2(den)


def _dil_kernel(q_ref, k_ref, v_ref, bias_ref, o_ref, qf, kf, vf, qd, kd, vd, o1, l1, od, ld, oc, lc, on, ln, *,
                plan, scale):
    (_, _, blocks1), (d, len_d, blocks_d), (_, len_c, blocks_c) = plan
    qf[...] = q_ref[...].astype(F32) * scale
    kf[...] = k_ref[...].astype(F32)
    vf[...] = v_ref[...].astype(F32)

    def attend(q, k, v, var, nk):
        s = lax.dot_general(q, k, (((1,), (1,)), ((), ())), preferred_element_type=F32)
        s = s + bias_ref[var][:, :nk]
        m = jnp.max(s, axis=-1, keepdims=True)
        e = jnp.exp2(s - m)
        l = jnp.sum(e, axis=-1, keepdims=True)
        o = jnp.dot(e.astype(BF16), v, preferred_element_type=F32) / l
        return o, jnp.broadcast_to(m + jnp.log2(l), o.shape)

    for q0, k0, nk, var in blocks1:
        o, lse = attend(qf[q0:q0 + DIL_QBLK, :].astype(BF16), kf[k0:k0 + nk, :].astype(BF16),
                        vf[k0:k0 + nk, :].astype(BF16), var, nk)
        o1[q0:q0 + DIL_QBLK, :] = o
        l1[q0:q0 + DIL_QBLK, :] = lse

    for r in range(d):
        qd[r] = qf[pl.ds(r, len_d, stride=d), :]
        kd[r] = kf[pl.ds(r, len_d, stride=d), :]
        vd[r] = vf[pl.ds(r, len_d, stride=d), :]

    for r in range(d):
        qs, ks, vs = qd[r].astype(BF16), kd[r].astype(BF16), vd[r].astype(BF16)
        for q0, k0, nk, var in blocks_d:
            o, lse = attend(qs[q0:q0 + DIL_QBLK], ks[k0:k0 + nk], vs[k0:k0 + nk], var, nk)
            od[r, q0:q0 + DIL_QBLK, :] = o
            ld[r, q0:q0 + DIL_QBLK, :] = lse
        for a in range(d):
            qs = qd[r, pl.ds(a, len_c, stride=d), :].astype(BF16)
            ks = kd[r, pl.ds(a, len_c, stride=d), :].astype(BF16)
            vs = vd[r, pl.ds(a, len_c, stride=d), :].astype(BF16)
            for q0, k0, nk, var in blocks_c:
                o, lse = attend(qs[q0:q0 + DIL_QBLK], ks[k0:k0 + nk], vs[k0:k0 + nk], var, nk)
                oc[r, pl.ds(q0 * d + a, DIL_QBLK, stride=d), :] = o
                lc[r, pl.ds(q0 * d + a, DIL_QBLK, stride=d), :] = lse

    for r in range(d):
        o, lse = _merge_lse2(od[r], ld[r], oc[r], lc[r])
        on[pl.ds(r, len_d, stride=d), :] = o
        ln[pl.ds(r, len_d, stride=d), :] = lse

    o, _ = _merge_lse2(o1[...], l1[...], on[...], ln[...])
    o_ref[...] = o.astype(o_ref.dtype)


def dilated_attention(p3, col0, nh):
    bsz, seq, _ = p3.shape
    plan, bias_np = _dil_tables(seq, nh)
    dils = [dil for dil, _, _ in plan]
    assert len(dils) == 3 and dils[0] == 1 and dils[2] == dils[1] * dils[1], dils
    d = dils[1]
    nvar = bias_np.shape[1]
    kern = functools.partial(_dil_kernel, plan=plan, scale=HEAD_DIM ** -0.5 * LOG2E)
    head_spec = lambda off: pl.BlockSpec((None, seq, HEAD_DIM), lambda h, b: (b, 0, col0 + off + h))
    natural = pltpu.VMEM((seq, HEAD_DIM), F32)
    by_residue = pltpu.VMEM((d, seq // d, HEAD_DIM), F32)
    return pl.pallas_call(
        kern,
        grid=(nh, bsz),
        in_specs=[head_spec(0), head_spec(nh), head_spec(2 * nh),
                  pl.BlockSpec((None, nvar, DIL_QBLK, 2 * DIL_QBLK), lambda h, b: (h, 0, 0, 0))],
        out_specs=pl.BlockSpec((None, seq, HEAD_DIM), lambda h, b: (b, 0, h)),
        out_shape=jax.ShapeDtypeStruct((bsz, seq, nh * HEAD_DIM), BF16),
        scratch_shapes=[natural] * 3 + [by_residue] * 3 + [natural] * 2 + [by_residue] * 4 + [natural] * 2,
        compiler_params=_params(("parallel", "parallel"), 40),
        name="dilated_attention",
    )(p3, p3, p3, jnp.asarray(bias_np * np.float32(LOG2E)))


def _out_proj_router_kernel(ya_ref, yb_ref, wa_ref, wb_ref, x_ref, g_ref, r_ref, xo_ref, h_ref, lg_ref):
    acc = jnp.dot(ya_ref[...], wa_ref[...], preferred_element_type=F32)
    acc = acc + jnp.dot(yb_ref[...], wb_ref[...], preferred_element_type=F32)
    ne = lg_ref.shape[1]
    for c in range(acc.shape[0] // EPILOGUE_ROWS):
        rows = slice(c * EPILOGUE_ROWS, (c + 1) * EPILOGUE_ROWS)
        xn = x_ref[rows, :] + acc[rows, :]
        xo_ref[rows, :] = xn
        ms = jnp.mean(xn * xn, axis=-1, keepdims=True)
        h = xn * lax.rsqrt(ms + RMS_EPS) * g_ref[...]
        h_hi = h.astype(BF16)
        h_ref[rows, :] = h_hi
        h_lo = (h - h_hi.astype(F32)).astype(BF16)
        s = (jnp.dot(h_hi, r_ref[...], preferred_element_type=F32)
             + jnp.dot(h_lo, r_ref[...], preferred_element_type=F32))
        lg_ref[rows, :] = s[:, :ne] + s[:, ne:]


def out_proj_norm_router(ya, yb, w_bf16, x2d, g, router, tm=512):
    t, d = x2d.shape
    ka = ya.shape[1]
    kb = yb.shape[1]
    assert ka == kb
    ne = router.shape[1]
    r_hi = router.astype(BF16)
    r_lo = (router - r_hi.astype(F32)).astype(BF16)
    r_split = jnp.concatenate([r_hi, r_lo], axis=1)
    return pl.pallas_call(
        _out_proj_router_kernel,
        grid=(t // tm,),
        in_specs=[pl.BlockSpec((tm, ka), lambda i: (i, 0)),
                  pl.BlockSpec((tm, kb), lambda i: (i, 0)),
                  pl.BlockSpec((ka, d), lambda i: (0, 0)),
                  pl.BlockSpec((kb, d), lambda i: (1, 0)),
                  pl.BlockSpec((tm, d), lambda i: (i, 0)),
                  pl.BlockSpec((1, d), lambda i: (0, 0)),
                  pl.BlockSpec((d, 2 * ne), lambda i: (0, 0))],
        out_specs=[pl.BlockSpec((tm, d), lambda i: (i, 0)),
                   pl.BlockSpec((tm, d), lambda i: (i, 0)),
                   pl.BlockSpec((tm, ne), lambda i: (i, 0))],
        out_shape=[jax.ShapeDtypeStruct((t, d), F32),
                   jax.ShapeDtypeStruct((t, d), BF16),
                   jax.ShapeDtypeStruct((t, ne), F32)],
        compiler_params=_params(("parallel",), 56),
        name="out_proj_norm_router",
    )(ya, yb, w_bf16, w_bf16, x2d, g.reshape(1, d), r_split)


def _route_kernel(lg_ref, slot_ref, aff_ref, *, cap):
    logits = lg_ref[...]
    mx = jnp.max(logits, axis=0, keepdims=True)
    ex = jnp.exp(logits - mx)
    a = ex / jnp.sum(ex, axis=0, keepdims=True)
    aff_ref[...] = a
    ne, seq = a.shape
    lanes = 128
    r = lax.broadcasted_iota(jnp.int32, (lanes, lanes), 0)
    c = lax.broadcasted_iota(jnp.int32, (lanes, lanes), 1)
    tri = jnp.where(r <= c, 1.0, 0.0).astype(BF16)

    def count(mask_f):
        return jnp.sum(mask_f, axis=1, keepdims=True)

    def prefix(x):
        parts = []
        run = jnp.zeros((ne, 1), F32)
        for j in range(seq // lanes):
            blk = x[:, j * lanes:(j + 1) * lanes]
            inc = jnp.dot(blk.astype(BF16), tri, preferred_element_type=F32)
            parts.append(inc - blk + run)
            run = run + count(blk)
        return jnp.concatenate(parts, axis=1)

    def body(i, ans):
        cand = ans | jnp.left_shift(jnp.int32(1), 30 - i)
        cnt = count(jnp.where(a >= lax.bitcast_convert_type(cand, F32), 1.0, 0.0))
        return jnp.where(cnt >= cap, cand, ans)

    ans = lax.fori_loop(0, 31, body, jnp.zeros((ne, 1), jnp.int32))
    thr = lax.bitcast_convert_type(ans, F32)
    gt = jnp.where(a > thr, 1.0, 0.0)
    eq = jnp.where(a >= thr, 1.0, 0.0) - gt
    need = cap - count(gt)
    sel = gt + eq * jnp.where(prefix(eq) < need, 1.0, 0.0)
    slot_ref[...] = jnp.where(sel > 0.5, prefix(sel), -1.0).astype(jnp.int32)


def route(logits_t, bsz, seq, cap):
    ne = logits_t.shape[0]
    return pl.pallas_call(
        functools.partial(_route_kernel, cap=cap),
        grid=(bsz,),
        in_specs=[pl.BlockSpec((ne, seq), lambda b: (0, b))],
        out_specs=[pl.BlockSpec((None, ne, seq), lambda b: (b, 0, 0)),
                   pl.BlockSpec((None, ne, seq), lambda b: (b, 0, 0))],
        out_shape=[jax.ShapeDtypeStruct((bsz, ne, seq), jnp.int32),
                   jax.ShapeDtypeStruct((bsz, ne, seq), F32)],
        compiler_params=_params(("parallel",), 32),
        name="route",
    )(logits_t)


def _gather_kernel(slot_ref, aff_ref, h_ref, xe_ref, gate_ref, *, cap, ncol):
    eg, seq = slot_ref.shape
    d = h_ref.shape[1]
    cidx = lax.broadcasted_iota(jnp.int32, (cap, seq), 0)
    rows = []
    for j in range(eg):
        hit = cidx == slot_ref[j:j + 1, :]
        rows.append(jnp.where(hit, 1.0, 0.0).astype(BF16))
        gate_ref[j] = jnp.sum(jnp.where(hit, aff_ref[j:j + 1, :], 0.0), axis=1, keepdims=True)
    p = jnp.concatenate(rows, axis=0)
    dc = d // ncol
    for c in range(ncol):
        xe = jnp.dot(p, h_ref[:, c * dc:(c + 1) * dc], preferred_element_type=F32)
        xe_ref[:, :, c * dc:(c + 1) * dc] = xe.reshape(eg, cap, dc).astype(xe_ref.dtype)


def gather_rows(slot, aff, h, cap, egroup=4, ncol=2):
    bsz, ne, seq = slot.shape
    d = h.shape[1]
    ngroups = ne // egroup
    grouped = lambda a: a.reshape(bsz, ngroups, egroup, seq)
    return pl.pallas_call(
        functools.partial(_gather_kernel, cap=cap, ncol=ncol),
        grid=(bsz, ngroups),
        in_specs=[pl.BlockSpec((None, None, egroup, seq), lambda b, g: (b, g, 0, 0)),
                  pl.BlockSpec((None, None, egroup, seq), lambda b, g: (b, g, 0, 0)),
                  pl.BlockSpec((seq, d), lambda b, g: (b, 0))],
        out_specs=[pl.BlockSpec((egroup, None, cap, d), lambda b, g: (g, b, 0, 0)),
                   pl.BlockSpec((egroup, None, cap, 1), lambda b, g: (g, b, 0, 0))],
        out_shape=[jax.ShapeDtypeStruct((ne, bsz, cap, d), BF16),
                   jax.ShapeDtypeStruct((ne, bsz, cap, 1), F32)],
        compiler_params=_params(("parallel", "parallel"), 48),
        name="moe_gather",
    )(grouped(slot), grouped(aff), h)


def _combine_kernel(slot_t_ref, ye_ref, x_ref, o_ref, *, cap):
    tb, ne = slot_t_ref.shape
    tn = x_ref.shape[1]
    cidx = lax.broadcasted_iota(jnp.int32, (tb, cap), 1)
    pt = jnp.concatenate([jnp.where(cidx == slot_t_ref[:, e:e + 1], 1.0, 0.0).astype(BF16) for e in range(ne)],
                         axis=1)
    ye = ye_ref[...].reshape(ne * cap, tn)
    o_ref[...] = x_ref[...] + jnp.dot(pt, ye, preferred_element_type=F32)


def combine(slot_t, ye, x2d, cap, tn=1024, tb=512):
    bsz, seq, ne = slot_t.shape
    d = x2d.shape[1]
    nt = seq // tb
    return pl.pallas_call(
        functools.partial(_combine_kernel, cap=cap),
        grid=(bsz, d // tn, nt),
        in_specs=[pl.BlockSpec((None, tb, ne), lambda b, n, t: (b, t, 0)),
                  pl.BlockSpec((ne, None, cap, tn), lambda b, n, t: (0, b, 0, n)),
                  pl.BlockSpec((tb, tn), lambda b, n, t: (b * nt + t, n))],
        out_specs=pl.BlockSpec((tb, tn), lambda b, n, t: (b * nt + t, n)),
        out_shape=jax.ShapeDtypeStruct(x2d.shape, F32),
        compiler_params=_params(("parallel", "parallel", "parallel"), 48),
        name="moe_combine",
    )(slot_t, ye, x2d)


def _expert_ffn_kernel(xe_ref, wg_ref, wu_ref, wd_ref, gate_ref, o_ref, acc_ref):
    f = pl.program_id(2)

    @pl.when(f == 0)
    def _():
        acc_ref[...] = jnp.zeros_like(acc_ref)

    wg = wg_ref[...].astype(BF16)
    wu = wu_ref[...].astype(BF16)
    wd = wd_ref[...].astype(BF16)
    rows_per = xe_ref.shape[0] // FFN_ROW_CHUNKS
    for c in range(FFN_ROW_CHUNKS):
        rows = slice(c * rows_per, (c + 1) * rows_per)
        xe = xe_ref[rows, :]
        g = jnp.dot(xe, wg, preferred_element_type=F32)
        u = jnp.dot(xe, wu, preferred_element_type=F32)
        hid = (jax.nn.silu(g) * u).astype(BF16)
        acc_ref[rows, :] += jnp.dot(hid, wd, preferred_element_type=F32)

    @pl.when(f == pl.num_programs(2) - 1)
    def _():
        o_ref[...] = (acc_ref[...] * gate_ref[...]).astype(o_ref.dtype)


def expert_ffn(xe, w_gate, w_up, w_down, layer, gate, tm=1024, tf=256):
    ne, r, d = xe.shape
    fdim = w_gate.shape[3]
    return pl.pallas_call(
        _expert_ffn_kernel,
        grid=(ne, r // tm, fdim // tf),
        in_specs=[pl.BlockSpec((None, tm, d), lambda e, m, f: (e, m, 0)),
                  pl.BlockSpec((None, None, d, tf), lambda e, m, f: (layer, e, 0, f)),
                  pl.BlockSpec((None, None, d, tf), lambda e, m, f: (layer, e, 0, f)),
                  pl.BlockSpec((None, None, tf, d), lambda e, m, f: (layer, e, f, 0)),
                  pl.BlockSpec((None, tm, 1), lambda e, m, f: (e, m, 0))],
        out_specs=pl.BlockSpec((None, tm, d), lambda e, m, f: (e, m, 0)),
        out_shape=jax.ShapeDtypeStruct((ne, r, d), BF16),
        scratch_shapes=[pltpu.VMEM((tm, d), F32)],
        compiler_params=_params(("parallel", "parallel", "arbitrary"), 56),
        name="expert_ffn",
    )(xe, w_gate, w_up, w_down, gate)


def _final_norm_kernel(x_ref, g_ref, o_ref):
    x = x_ref[...]
    ms = jnp.mean(x * x, axis=-1, keepdims=True)
    o_ref[...] = x * lax.rsqrt(ms + RMS_EPS) * g_ref[...]


def final_norm(x2d, g, tm=512):
    t, d = x2d.shape
    return pl.pallas_call(
        _final_norm_kernel,
        grid=(t // tm,),
        in_specs=[pl.BlockSpec((tm, d), lambda i: (i, 0)), pl.BlockSpec((1, d), lambda i: (0, 0))],
        out_specs=pl.BlockSpec((tm, d), lambda i: (i, 0)),
        out_shape=jax.ShapeDtypeStruct((t, d), F32),
        compiler_params=_params(("parallel",), 32),
        name="final_norm",
    )(x2d, g.reshape(1, d))


def expert_choice_moe(x2d, h, logits, bsz, seq, layer, w_gate, w_up, w_down):
    d = x2d.shape[1]
    ne = logits.shape[1]
    cap = EC_CAPACITY_FACTOR * seq // ne
    slot, aff = route(logits.T, bsz, seq, cap)
    xe, gate = gather_rows(slot, aff, h, cap)
    ye = expert_ffn(xe.reshape(ne, bsz * cap, d), w_gate, w_up, w_down, layer,
                    gate.reshape(ne, bsz * cap, 1))
    return combine(slot.transpose(0, 2, 1), ye.reshape(ne, bsz, cap, d), x2d, cap)


def kernel(x, norm_mix, norm_ffn, norm_final, w_in_ab, a_v_norm, a_spatial_w, a_spatial_b, b_rpb, w_out_ab,
           w_in_cd, c_conv, w_out_cd, router, w_gate, w_up, w_down):
    bsz, seq, d = x.shape
    depth = norm_mix.shape[0]
    half = d // 2
    nh = half // HEAD_DIM
    x2d = x.reshape(bsz * seq, d)
    for layer in range(depth):
        i = layer // 2
        if layer % 2 == 0:
            p = rms_matmul(x2d, norm_mix[layer], w_in_ab[i])
            p3 = p.reshape(bsz, seq, -1)
            ya = sgu(p, a_v_norm[i], a_spatial_w[i], a_spatial_b[i], half)
            yb = neighbourhood_attention(p3, b_rpb[i], 2 * half // HEAD_DIM, nh).reshape(bsz * seq, half)
            mix, w_out = (ya, yb), w_out_ab[i]
        else:
            p = rms_matmul(x2d, norm_mix[layer], w_in_cd[i])
            p3 = p.reshape(bsz, seq, -1)
            yc = gated_short_conv(p3, c_conv[i], half).reshape(bsz * seq, half)
            yd = dilated_attention(p3, 3 * half // HEAD_DIM, nh).reshape(bsz * seq, half)
            mix, w_out = (yc, yd), w_out_cd[i]
        x2d, h, logits = out_proj_norm_router(mix[0], mix[1], w_out.astype(BF16), x2d,
                                              norm_ffn[layer], router[layer])
        x2d = expert_choice_moe(x2d, h, logits, bsz, seq, layer, w_gate, w_up, w_down)
    return final_norm(x2d, norm_final).reshape(bsz, seq, d)
```

```python
import functools

import numpy as np
import jax
import jax.numpy as jnp
from jax import lax
from jax.experimental import pallas as pl
from jax.experimental.pallas import tpu as pltpu

F32 = jnp.float32
BF16 = jnp.bfloat16

HEAD_DIM = 128
CHUNK = 128
GRID_W = 64
NA_ROWS = 8
NA_COLS = 16
NA_QROWS = 4
NA_KROWS = NA_QROWS + NA_ROWS
CONV_W = 3
DIL_PATTERNS = ((128, 1), (512, 4), (2048, 16))
DIL_QBLK = 128
EPILOGUE_ROWS = 128
FFN_ROW_CHUNKS = 2
N_EXPERTS = 16
EC_CAPACITY_FACTOR = 2
RMS_EPS = 1e-6
LN_EPS = 1e-5
NEG_INF = -1e30
LOG2E = 1.4426950408889634
MIB = 1024 * 1024


def _params(semantics, vmem_mib):
    return pltpu.CompilerParams(dimension_semantics=semantics,
                                vmem_limit_bytes=int(vmem_mib * MIB))


def _rms_to_scratch(x_ref, g_ref, h_scr, rows):
    n = x_ref.shape[0] // rows

    def body(i, carry):
        r = pl.multiple_of(i * rows, rows)
        x = x_ref[pl.ds(r, rows), :]
        ms = jnp.mean(x * x, axis=-1, keepdims=True)
        h_scr[pl.ds(r, rows), :] = (x * lax.rsqrt(ms + RMS_EPS) * g_ref[...]).astype(h_scr.dtype)
        return carry

    lax.fori_loop(0, n, body, 0)


def _rms_matmul_kernel(x_ref, g_ref, w_ref, o_ref, h_scr):
    @pl.when(pl.program_id(1) == 0)
    def _():
        _rms_to_scratch(x_ref, g_ref, h_scr, 256)

    o_ref[...] = jnp.dot(h_scr[...], w_ref[...].astype(BF16), preferred_element_type=F32).astype(o_ref.dtype)


def rms_matmul(x2d, g, w, tm=1024, tn=1024):
    t, d = x2d.shape
    n = w.shape[1]
    return pl.pallas_call(
        _rms_matmul_kernel,
        grid=(t // tm, n // tn),
        in_specs=[pl.BlockSpec((tm, d), lambda i, j: (i, 0)),
                  pl.BlockSpec((1, d), lambda i, j: (0, 0)),
                  pl.BlockSpec((d, tn), lambda i, j: (0, j))],
        out_specs=pl.BlockSpec((tm, tn), lambda i, j: (i, j)),
        out_shape=jax.ShapeDtypeStruct((t, n), BF16),
        scratch_shapes=[pltpu.VMEM((tm, d), BF16)],
        compiler_params=_params(("parallel", "arbitrary"), 56),
        name="rms_in_proj",
    )(x2d, g.reshape(1, d), w)


def _sgu_kernel(u_ref, v_ref, lng_ref, ws_ref, bias_ref, o_ref):
    tm, width = u_ref.shape
    groups = width // HEAD_DIM
    nchunks = tm // CHUNK
    vf = jax.nn.gelu(v_ref[...].astype(F32))
    mu = jnp.mean(vf, axis=-1, keepdims=True)
    dv = vf - mu
    var = jnp.mean(dv * dv, axis=-1, keepdims=True)
    vn = (dv * lax.rsqrt(var + LN_EPS) * lng_ref[...]).astype(BF16)
    for g in range(groups):
        cols = slice(g * HEAD_DIM, (g + 1) * HEAD_DIM)
        vg = jnp.concatenate([vn[n * CHUNK:(n + 1) * CHUNK, cols] for n in range(nchunks)], axis=1)
        mixed = jnp.dot(ws_ref[g], vg, preferred_element_type=F32)
        for n in range(nchunks):
            rows = slice(n * CHUNK, (n + 1) * CHUNK)
            u = jax.nn.gelu(u_ref[rows, cols].astype(F32))
            o_ref[rows, cols] = (u * (mixed[:, n * CHUNK:(n + 1) * CHUNK] + bias_ref[:, cols])).astype(o_ref.dtype)


def sgu(p2d, ln_g, w_s, b_s, width, tm=512):
    t = p2d.shape[0]
    groups = width // HEAD_DIM
    bias_full = jnp.repeat(b_s.T.astype(F32), HEAD_DIM, axis=1)
    return pl.pallas_call(
        _sgu_kernel,
        grid=(t // tm,),
        in_specs=[pl.BlockSpec((tm, width), lambda i: (i, 0)),
                  pl.BlockSpec((tm, width), lambda i: (i, 1)),
                  pl.BlockSpec((1, width), lambda i: (0, 0)),
                  pl.BlockSpec((groups, CHUNK, CHUNK), lambda i: (0, 0, 0)),
                  pl.BlockSpec((CHUNK, width), lambda i: (0, 0))],
        out_specs=pl.BlockSpec((tm, width), lambda i: (i, 0)),
        out_shape=jax.ShapeDtypeStruct((t, width), BF16),
        compiler_params=_params(("parallel",), 40),
        name="sgu",
    )(p2d, p2d, ln_g.reshape(1, width).astype(F32), w_s.astype(BF16), bias_full)


def _na_block_layout(rows):
    nblk = rows // NA_QROWS
    kh = min(NA_ROWS, rows)
    starts, variants, var_key = [], [], {}
    for qb in range(nblk):
        r0 = qb * NA_QROWS
        ws = int(np.clip(r0 - kh // 2, 0, rows - NA_KROWS))
        rs = np.clip(np.arange(r0, r0 + NA_QROWS) - kh // 2, 0, rows - kh)
        key = (tuple(rs - ws), r0 - ws)
        if key not in var_key:
            var_key[key] = len(var_key)
        starts.append(ws)
        variants.append(var_key[key])
    return starts, variants, list(var_key.keys()), kh


def _na_tiles(rpb, rows):
    nh = rpb.shape[0]
    _, _, keys, kh = _na_block_layout(rows)
    qc = np.arange(GRID_W)[:, None]
    kc = np.arange(GRID_W)[None, :]
    col_start = np.clip(qc - NA_COLS // 2, 0, GRID_W - NA_COLS)
    col_valid = (kc >= col_start) & (kc < col_start + NA_COLS)
    dc_idx = np.clip(kc - qc + NA_COLS - 1, 0, 2 * NA_COLS - 2)
    onehot = (dc_idx[None] == np.arange(2 * NA_COLS - 1)[:, None, None]).astype(np.float32)
    expanded = jnp.einsum('hrc,cqk->hrqk', rpb.astype(F32) * LOG2E, jnp.asarray(onehot),
                          precision=lax.Precision.HIGHEST)
    tiles = jnp.where(jnp.asarray(col_valid)[None, None], expanded, NEG_INF)
    tiles = jnp.concatenate([tiles, jnp.full((nh, 1, GRID_W, GRID_W), NEG_INF, F32)], axis=1)
    invalid = 2 * NA_ROWS - 1
    dr = np.full((len(keys), NA_QROWS, NA_KROWS), invalid, np.int32)
    for v, (rs_rel, r0_rel) in enumerate(keys):
        for i in range(NA_QROWS):
            for j in range(NA_KROWS):
                if rs_rel[i] <= j < rs_rel[i] + kh:
                    dr[v, i, j] = j - (r0_rel + i) + NA_ROWS - 1
    return tiles, dr


def _na_kernel(q_ref, k_ref, v_ref, tiles_ref, o_ref, bias_scr, *, starts, variants, tile_idx, scale):
    qblk = NA_QROWS * GRID_W
    kblk = NA_KROWS * GRID_W

    @pl.when(pl.program_id(1) == 0)
    def _():
        for var in range(tile_idx.shape[0]):
            for i in range(NA_QROWS):
                strip = jnp.concatenate([tiles_ref[int(tile_idx[var, i, j])] for j in range(NA_KROWS)], axis=1)
                bias_scr[var, i * GRID_W:(i + 1) * GRID_W, :] = strip

    for qb, (ws, var) in enumerate(zip(starts, variants)):
        q = (q_ref[qb * qblk:(qb + 1) * qblk, :].astype(F32) * scale).astype(BF16)
        k = k_ref[ws * GRID_W:ws * GRID_W + kblk, :]
        v = v_ref[ws * GRID_W:ws * GRID_W + kblk, :]
        s = lax.dot_general(q, k, (((1,), (1,)), ((), ())), preferred_element_type=F32)
        s = s + bias_scr[var]
        m = jnp.max(s, axis=-1, keepdims=True)
        e = jnp.exp2(s - m)
        l = jnp.sum(e, axis=-1, keepdims=True)
        o = jnp.dot(e.astype(BF16), v, preferred_element_type=F32)
        o_ref[qb * qblk:(qb + 1) * qblk, :] = (o / l).astype(o_ref.dtype)


def neighbourhood_attention(p3, rpb, col0, nh):
    bsz, seq, _ = p3.shape
    rows = seq // GRID_W
    starts, variants, keys, _ = _na_block_layout(rows)
    tiles, tile_idx = _na_tiles(rpb, rows)
    nvar = len(keys)
    ntiles = tiles.shape[1]
    qblk, kblk = NA_QROWS * GRID_W, NA_KROWS * GRID_W
    kern = functools.partial(_na_kernel, starts=starts, variants=variants, tile_idx=tile_idx,
                             scale=HEAD_DIM ** -0.5 * LOG2E)
    head_spec = lambda off: pl.BlockSpec((None, seq, HEAD_DIM), lambda h, b: (b, 0, col0 + off + h))
    return pl.pallas_call(
        kern,
        grid=(nh, bsz),
        in_specs=[head_spec(0), head_spec(nh), head_spec(2 * nh),
                  pl.BlockSpec((None, ntiles, GRID_W, GRID_W), lambda h, b: (h, 0, 0, 0))],
        out_specs=pl.BlockSpec((None, seq, HEAD_DIM), lambda h, b: (b, 0, h)),
        out_shape=jax.ShapeDtypeStruct((bsz, seq, nh * HEAD_DIM), BF16),
        scratch_shapes=[pltpu.VMEM((nvar, qblk, kblk), F32)],
        compiler_params=_params(("parallel", "arbitrary"), 32),
        name="neighbourhood_attention",
    )(p3, p3, p3, tiles)


def _conv_kernel(b_ref, c_ref, x_ref, taps_ref, o_ref):
    seq = b_ref.shape[0]
    z = c_ref[...].astype(F32) * x_ref[...].astype(F32)
    row = lax.broadcasted_iota(jnp.int32, z.shape, 0)
    z_prev = jnp.where(row == 0, 0.0, pltpu.roll(z, 1, axis=0))
    z_next = jnp.where(row == seq - 1, 0.0, pltpu.roll(z, seq - 1, axis=0))
    y = taps_ref[0:1, :] * z_prev + taps_ref[1:2, :] * z + taps_ref[2:3, :] * z_next
    o_ref[...] = (b_ref[...].astype(F32) * y).astype(o_ref.dtype)


def gated_short_conv(p3, taps, width, tc=256):
    bsz, seq, _ = p3.shape
    nblk = width // tc
    spec = lambda off: pl.BlockSpec((None, seq, tc), lambda b, j: (b, 0, off * nblk + j))
    return pl.pallas_call(
        _conv_kernel,
        grid=(bsz, nblk),
        in_specs=[spec(0), spec(1), spec(2), pl.BlockSpec((CONV_W, tc), lambda b, j: (0, j))],
        out_specs=pl.BlockSpec((None, seq, tc), lambda b, j: (b, 0, j)),
        out_shape=jax.ShapeDtypeStruct((bsz, seq, width), BF16),
        compiler_params=_params(("parallel", "parallel"), 40),
        name="gated_short_conv",
    )(p3, p3, p3, taps.astype(F32))


def _dil_windows(length):
    nk = min(2 * DIL_QBLK, length)
    out = []
    for i in range(length // DIL_QBLK):
        q0 = i * DIL_QBLK
        k0 = int(np.clip(q0 - DIL_QBLK // 2, 0, length - nk))
        out.append((q0, k0, nk))
    return out


def _dil_tables(seq, nh):
    slopes = np.array([2.0 ** (-8.0 * (h + 1) / nh) for h in range(nh)], dtype=np.float32)
    var_index, absd_tabs, dil_of = {}, [], []
    plan = []
    for window, dil in DIL_PATTERNS:
        radius = window // (2 * dil)
        length = seq // dil
        blocks = []
        for q0, k0, nk in _dil_windows(length):
            key = (dil, radius, k0 - q0, nk)
            if key not in var_index:
                var_index[key] = len(absd_tabs)
                delta = (k0 + np.arange(2 * DIL_QBLK)[None, :]) - (q0 + np.arange(DIL_QBLK)[:, None])
                absd = np.abs(delta).astype(np.float32)
                valid = (np.abs(delta) <= radius) & (np.arange(2 * DIL_QBLK)[None, :] < nk)
                absd_tabs.append((absd * dil, valid))
            blocks.append((q0, k0, nk, var_index[key]))
        plan.append((dil, length, blocks))
    pen = np.stack([a for a, _ in absd_tabs])
    valid = np.stack([v for _, v in absd_tabs])
    bias = np.where(valid[None], -(slopes[:, None, None, None] * pen[None]).astype(np.float32), np.float32(NEG_INF))
    return plan, bias.astype(np.float32)


def _merge_lse2(oa, la, ob, lb):
    m = jnp.maximum(la, lb)
    wa = jnp.exp2(la - m)
    wb = jnp.exp2(lb - m)
    den = wa + wb
    return (wa * oa + wb * ob) / den, m + jnp.log2(den)


def _dil_kernel(q_ref, k_ref, v_ref, bias_ref, o_ref, qf, kf, vf, qd, kd, vd, o1, l1, od, ld, oc, lc, on, ln, *,
                plan, scale):
    (_, _, blocks1), (d, len_d, blocks_d), (_, len_c, blocks_c) = plan
    qf[...] = q_ref[...].astype(F32) * scale
    kf[...] = k_ref[...].astype(F32)
    vf[...] = v_ref[...].astype(F32)

    def attend(q, k, v, var, nk):
        s = lax.dot_general(q, k, (((1,), (1,)), ((), ())), preferred_element_type=F32)
        s = s + bias_ref[var][:, :nk]
        m = jnp.max(s, axis=-1, keepdims=True)
        e = jnp.exp2(s - m)
        l = jnp.sum(e, axis=-1, keepdims=True)
        o = jnp.dot(e.astype(BF16), v, preferred_element_type=F32) / l
        return o, jnp.broadcast_to(m + jnp.log2(l), o.shape)

    for q0, k0, nk, var in blocks1:
        o, lse = attend(qf[q0:q0 + DIL_QBLK, :].astype(BF16), kf[k0:k0 + nk, :].astype(BF16),
                        vf[k0:k0 + nk, :].astype(BF16), var, nk)
        o1[q0:q0 + DIL_QBLK, :] = o
        l1[q0:q0 + DIL_QBLK, :] = lse

    for r in range(d):
        qd[r] = qf[pl.ds(r, len_d, stride=d), :]
        kd[r] = kf[pl.ds(r, len_d, stride=d), :]
        vd[r] = vf[pl.ds(r, len_d, stride=d), :]

    for r in range(d):
        qs, ks, vs = qd[r].astype(BF16), kd[r].astype(BF16), vd[r].astype(BF16)
        for q0, k0, nk, var in blocks_d:
            o, lse = attend(qs[q0:q0 + DIL_QBLK], ks[k0:k0 + nk], vs[k0:k0 + nk], var, nk)
            od[r, q0:q0 + DIL_QBLK, :] = o
            ld[r, q0:q0 + DIL_QBLK, :] = lse
        for a in range(d):
            qs = qd[r, pl.ds(a, len_c, stride=d), :].astype(BF16)
            ks = kd[r, pl.ds(a, len_c, stride=d), :].astype(BF16)
            vs = vd[r, pl.ds(a, len_c, stride=d), :].astype(BF16)
            for q0, k0, nk, var in blocks_c:
                o, lse = attend(qs[q0:q0 + DIL_QBLK], ks[k0:k0 + nk], vs[k0:k0 + nk], var, nk)
                oc[r, pl.ds(q0 * d + a, DIL_QBLK, stride=d), :] = o
                lc[r, pl.ds(q0 * d + a, DIL_QBLK, stride=d), :] = lse

    for r in range(d):
        o, lse = _merge_lse2(od[r], ld[r], oc[r], lc[r])
        on[pl.ds(r, len_d, stride=d), :] = o
        ln[pl.ds(r, len_d, stride=d), :] = lse

    o, _ = _merge_lse2(o1[...], l1[...], on[...], ln[...])
    o_ref[...] = o.astype(o_ref.dtype)


def dilated_attention(p3, col0, nh):
    bsz, seq, _ = p3.shape
    plan, bias_np = _dil_tables(seq, nh)
    dils = [dil for dil, _, _ in plan]
    assert len(dils) == 3 and dils[0] == 1 and dils[2] == dils[1] * dils[1], dils
    d = dils[1]
    nvar = bias_np.shape[1]
    kern = functools.partial(_dil_kernel, plan=plan, scale=HEAD_DIM ** -0.5 * LOG2E)
    head_spec = lambda off: pl.BlockSpec((None, seq, HEAD_DIM), lambda h, b: (b, 0, col0 + off + h))
    natural = pltpu.VMEM((seq, HEAD_DIM), F32)
    by_residue = pltpu.VMEM((d, seq // d, HEAD_DIM), F32)
    return pl.pallas_call(
        kern,
        grid=(nh, bsz),
        in_specs=[head_spec(0), head_spec(nh), head_spec(2 * nh),
                  pl.BlockSpec((None, nvar, DIL_QBLK, 2 * DIL_QBLK), lambda h, b: (h, 0, 0, 0))],
        out_specs=pl.BlockSpec((None, seq, HEAD_DIM), lambda h, b: (b, 0, h)),
        out_shape=jax.ShapeDtypeStruct((bsz, seq, nh * HEAD_DIM), BF16),
        scratch_shapes=[natural] * 3 + [by_residue] * 3 + [natural] * 2 + [by_residue] * 4 + [natural] * 2,
        compiler_params=_params(("parallel", "parallel"), 40),
        name="dilated_attention",
    )(p3, p3, p3, jnp.asarray(bias_np * np.float32(LOG2E)))


def _out_proj_router_kernel(ya_ref, yb_ref, wa_ref, wb_ref, x_ref, g_ref, r_ref, xo_ref, h_ref, lg_ref):
    acc = jnp.dot(ya_ref[...], wa_ref[...], preferred_element_type=F32)
    acc = acc + jnp.dot(yb_ref[...], wb_ref[...], preferred_element_type=F32)
    ne = lg_ref.shape[1]
    for c in range(acc.shape[0] // EPILOGUE_ROWS):
        rows = slice(c * EPILOGUE_ROWS, (c + 1) * EPILOGUE_ROWS)
        xn = x_ref[rows, :] + acc[rows, :]
        xo_ref[rows, :] = xn
        ms = jnp.mean(xn * xn, axis=-1, keepdims=True)
        h = xn * lax.rsqrt(ms + RMS_EPS) * g_ref[...]
        h_hi = h.astype(BF16)
        h_ref[rows, :] = h_hi
        h_lo = (h - h_hi.astype(F32)).astype(BF16)
        s = (jnp.dot(h_hi, r_ref[...], preferred_element_type=F32)
             + jnp.dot(h_lo, r_ref[...], preferred_element_type=F32))
        lg_ref[rows, :] = s[:, :ne] + s[:, ne:]


def out_proj_norm_router(ya, yb, w_bf16, x2d, g, router, tm=512):
    t, d = x2d.shape
    ka = ya.shape[1]
    kb = yb.shape[1]
    assert ka == kb
    ne = router.shape[1]
    r_hi = router.astype(BF16)
    r_lo = (router - r_hi.astype(F32)).astype(BF16)
    r_split = jnp.concatenate([r_hi, r_lo], axis=1)
    return pl.pallas_call(
        _out_proj_router_kernel,
        grid=(t // tm,),
        in_specs=[pl.BlockSpec((tm, ka), lambda i: (i, 0)),
                  pl.BlockSpec((tm, kb), lambda i: (i, 0)),
                  pl.BlockSpec((ka, d), lambda i: (0, 0)),
                  pl.BlockSpec((kb, d), lambda i: (1, 0)),
                  pl.BlockSpec((tm, d), lambda i: (i, 0)),
                  pl.BlockSpec((1, d), lambda i: (0, 0)),
                  pl.BlockSpec((d, 2 * ne), lambda i: (0, 0))],
        out_specs=[pl.BlockSpec((tm, d), lambda i: (i, 0)),
                   pl.BlockSpec((tm, d), lambda i: (i, 0)),
                   pl.BlockSpec((tm, ne), lambda i: (i, 0))],
        out_shape=[jax.ShapeDtypeStruct((t, d), F32),
                   jax.ShapeDtypeStruct((t, d), BF16),
                   jax.ShapeDtypeStruct((t, ne), F32)],
        compiler_params=_params(("parallel",), 56),
        name="out_proj_norm_router",
    )(ya, yb, w_bf16, w_bf16, x2d, g.reshape(1, d), r_split)


def _route_kernel(lg_ref, slot_ref, aff_ref, *, cap):
    logits = lg_ref[...]
    mx = jnp.max(logits, axis=0, keepdims=True)
    ex = jnp.exp(logits - mx)
    a = ex / jnp.sum(ex, axis=0, keepdims=True)
    aff_ref[...] = a
    ne, seq = a.shape
    lanes = 128
    r = lax.broadcasted_iota(jnp.int32, (lanes, lanes), 0)
    c = lax.broadcasted_iota(jnp.int32, (lanes, lanes), 1)
    tri = jnp.where(r <= c, 1.0, 0.0).astype(BF16)

    def count(mask_f):
        return jnp.sum(mask_f, axis=1, keepdims=True)

    def prefix(x):
        parts = []
        run = jnp.zeros((ne, 1), F32)
        for j in range(seq // lanes):
            blk = x[:, j * lanes:(j + 1) * lanes]
            inc = jnp.dot(blk.astype(BF16), tri, preferred_element_type=F32)
            parts.append(inc - blk + run)
            run = run + count(blk)
        return jnp.concatenate(parts, axis=1)

    def body(i, ans):
        cand = ans | jnp.left_shift(jnp.int32(1), 30 - i)
        cnt = count(jnp.where(a >= lax.bitcast_convert_type(cand, F32), 1.0, 0.0))
        return jnp.where(cnt >= cap, cand, ans)

    ans = lax.fori_loop(0, 31, body, jnp.zeros((ne, 1), jnp.int32))
    thr = lax.bitcast_convert_type(ans, F32)
    gt = jnp.where(a > thr, 1.0, 0.0)
    eq = jnp.where(a >= thr, 1.0, 0.0) - gt
    need = cap - count(gt)
    sel = gt + eq * jnp.where(prefix(eq) < need, 1.0, 0.0)
    slot_ref[...] = jnp.where(sel > 0.5, prefix(sel), -1.0).astype(jnp.int32)


def route(logits_t, bsz, seq, cap):
    ne = logits_t.shape[0]
    return pl.pallas_call(
        functools.partial(_route_kernel, cap=cap),
        grid=(bsz,),
        in_specs=[pl.BlockSpec((ne, seq), lambda b: (0, b))],
        out_specs=[pl.BlockSpec((None, ne, seq), lambda b: (b, 0, 0)),
                   pl.BlockSpec((None, ne, seq), lambda b: (b, 0, 0))],
        out_shape=[jax.ShapeDtypeStruct((bsz, ne, seq), jnp.int32),
                   jax.ShapeDtypeStruct((bsz, ne, seq), F32)],
        compiler_params=_params(("parallel",), 32),
        name="route",
    )(logits_t)


def _gather_kernel(slot_ref, aff_ref, h_ref, xe_ref, gate_ref, *, cap, ncol):
    eg, seq = slot_ref.shape
    d = h_ref.shape[1]
    cidx = lax.broadcasted_iota(jnp.int32, (cap, seq), 0)
    rows = []
    for j in range(eg):
        hit = cidx == slot_ref[j:j + 1, :]
        rows.append(jnp.where(hit, 1.0, 0.0).astype(BF16))
        gate_ref[j] = jnp.sum(jnp.where(hit, aff_ref[j:j + 1, :], 0.0), axis=1, keepdims=True)
    p = jnp.concatenate(rows, axis=0)
    dc = d // ncol
    for c in range(ncol):
        xe = jnp.dot(p, h_ref[:, c * dc:(c + 1) * dc], preferred_element_type=F32)
        xe_ref[:, :, c * dc:(c + 1) * dc] = xe.reshape(eg, cap, dc).astype(xe_ref.dtype)


def gather_rows(slot, aff, h, cap, egroup=4, ncol=2):
    bsz, ne, seq = slot.shape
    d = h.shape[1]
    ngroups = ne // egroup
    grouped = lambda a: a.reshape(bsz, ngroups, egroup, seq)
    return pl.pallas_call(
        functools.partial(_gather_kernel, cap=cap, ncol=ncol),
        grid=(bsz, ngroups),
        in_specs=[pl.BlockSpec((None, None, egroup, seq), lambda b, g: (b, g, 0, 0)),
                  pl.BlockSpec((None, None, egroup, seq), lambda b, g: (b, g, 0, 0)),
                  pl.BlockSpec((seq, d), lambda b, g: (b, 0))],
        out_specs=[pl.BlockSpec((egroup, None, cap, d), lambda b, g: (g, b, 0, 0)),
                   pl.BlockSpec((egroup, None, cap, 1), lambda b, g: (g, b, 0, 0))],
        out_shape=[jax.ShapeDtypeStruct((ne, bsz, cap, d), BF16),
                   jax.ShapeDtypeStruct((ne, bsz, cap, 1), F32)],
        compiler_params=_params(("parallel", "parallel"), 48),
        name="moe_gather",
    )(grouped(slot), grouped(aff), h)


def _combine_kernel(slot_t_ref, ye_ref, x_ref, o_ref, *, cap):
    tb, ne = slot_t_ref.shape
    tn = x_ref.shape[1]
    cidx = lax.broadcasted_iota(jnp.int32, (tb, cap), 1)
    pt = jnp.concatenate([jnp.where(cidx == slot_t_ref[:, e:e + 1], 1.0, 0.0).astype(BF16) for e in range(ne)],
                         axis=1)
    ye = ye_ref[...].reshape(ne * cap, tn)
    o_ref[...] = x_ref[...] + jnp.dot(pt, ye, preferred_element_type=F32)


def combine(slot_t, ye, x2d, cap, tn=1024, tb=512):
    bsz, seq, ne = slot_t.shape
    d = x2d.shape[1]
    nt = seq // tb
    return pl.pallas_call(
        functools.partial(_combine_kernel, cap=cap),
        grid=(bsz, d // tn, nt),
        in_specs=[pl.BlockSpec((None, tb, ne), lambda b, n, t: (b, t, 0)),
                  pl.BlockSpec((ne, None, cap, tn), lambda b, n, t: (0, b, 0, n)),
                  pl.BlockSpec((tb, tn), lambda b, n, t: (b * nt + t, n))],
        out_specs=pl.BlockSpec((tb, tn), lambda b, n, t: (b * nt + t, n)),
        out_shape=jax.ShapeDtypeStruct(x2d.shape, F32),
        compiler_params=_params(("parallel", "parallel", "parallel"), 48),
        name="moe_combine",
    )(slot_t, ye, x2d)


def _expert_ffn_kernel(xe_ref, wg_ref, wu_ref, wd_ref, gate_ref, o_ref, acc_ref):
    f = pl.program_id(2)
    last = pl.num_programs(2) - 1
    rows_per = xe_ref.shape[0] // FFN_ROW_CHUNKS

    def step(first, final):
        wg = wg_ref[...].astype(BF16)
        wu = wu_ref[...].astype(BF16)
        wd = wd_ref[...].astype(BF16)
        for c in range(FFN_ROW_CHUNKS):
            rows = slice(c * rows_per, (c + 1) * rows_per)
            xe = xe_ref[rows, :]
            g = jnp.dot(xe, wg, preferred_element_type=F32)
            u = jnp.dot(xe, wu, preferred_element_type=F32)
            hid = (jax.nn.silu(g) * u).astype(BF16)
            part = jnp.dot(hid, wd, preferred_element_type=F32)
            total = part if first else acc_ref[rows, :] + part
            if final:
                o_ref[rows, :] = (total * gate_ref[rows, :]).astype(o_ref.dtype)
            else:
                acc_ref[rows, :] = total

    pl.when(f == 0)(lambda: step(True, False))
    pl.when(jnp.logical_and(f > 0, f < last))(lambda: step(False, False))
    pl.when(f == last)(lambda: step(False, True))


def expert_ffn(xe, w_gate, w_up, w_down, layer, gate, tm=1024, tf=256):
    ne, r, d = xe.shape
    fdim = w_gate.shape[3]
    assert fdim // tf >= 2, "the kernel distinguishes first / middle / last hidden tiles"
    return pl.pallas_call(
        _expert_ffn_kernel,
        grid=(ne, r // tm, fdim // tf),
        in_specs=[pl.BlockSpec((None, tm, d), lambda e, m, f: (e, m, 0)),
                  pl.BlockSpec((None, None, d, tf), lambda e, m, f: (layer, e, 0, f)),
                  pl.BlockSpec((None, None, d, tf), lambda e, m, f: (layer, e, 0, f)),
                  pl.BlockSpec((None, None, tf, d), lambda e, m, f: (layer, e, f, 0)),
                  pl.BlockSpec((None, tm, 1), lambda e, m, f: (e, m, 0))],
        out_specs=pl.BlockSpec((None, tm, d), lambda e, m, f: (e, m, 0)),
        out_shape=jax.ShapeDtypeStruct((ne, r, d), BF16),
        scratch_shapes=[pltpu.VMEM((tm, d), F32)],
        compiler_params=_params(("parallel", "parallel", "arbitrary"), 56),
        name="expert_ffn",
    )(xe, w_gate, w_up, w_down, gate)


def _final_norm_kernel(x_ref, g_ref, o_ref):
    x = x_ref[...]
    ms = jnp.mean(x * x, axis=-1, keepdims=True)
    o_ref[...] = x * lax.rsqrt(ms + RMS_EPS) * g_ref[...]


def final_norm(x2d, g, tm=512):
    t, d = x2d.shape
    return pl.pallas_call(
        _final_norm_kernel,
        grid=(t // tm,),
        in_specs=[pl.BlockSpec((tm, d), lambda i: (i, 0)), pl.BlockSpec((1, d), lambda i: (0, 0))],
        out_specs=pl.BlockSpec((tm, d), lambda i: (i, 0)),
        out_shape=jax.ShapeDtypeStruct((t, d), F32),
        compiler_params=_params(("parallel",), 32),
        name="final_norm",
    )(x2d, g.reshape(1, d))


def expert_choice_moe(x2d, h, logits, bsz, seq, layer, w_gate, w_up, w_down):
    d = x2d.shape[1]
    ne = logits.shape[1]
    cap = EC_CAPACITY_FACTOR * seq // ne
    slot, aff = route(logits.T, bsz, seq, cap)
    xe, gate = gather_rows(slot, aff, h, cap)
    ye = expert_ffn(xe.reshape(ne, bsz * cap, d), w_gate, w_up, w_down, layer,
                    gate.reshape(ne, bsz * cap, 1))
    return combine(slot.transpose(0, 2, 1), ye.reshape(ne, bsz, cap, d), x2d, cap)


def kernel(x, norm_mix, norm_ffn, norm_final, w_in_ab, a_v_norm, a_spatial_w, a_spatial_b, b_rpb, w_out_ab,
           w_in_cd, c_conv, w_out_cd, router, w_gate, w_up, w_down):
    bsz, seq, d = x.shape
    depth = norm_mix.shape[0]
    half = d // 2
    nh = half // HEAD_DIM
    x2d = x.reshape(bsz * seq, d)
    for layer in range(depth):
        i = layer // 2
        if layer % 2 == 0:
            p = rms_matmul(x2d, norm_mix[layer], w_in_ab[i])
            p3 = p.reshape(bsz, seq, -1)
            ya = sgu(p, a_v_norm[i], a_spatial_w[i], a_spatial_b[i], half)
            yb = neighbourhood_attention(p3, b_rpb[i], 2 * half // HEAD_DIM, nh).reshape(bsz * seq, half)
            mix, w_out = (ya, yb), w_out_ab[i]
        else:
            p = rms_matmul(x2d, norm_mix[layer], w_in_cd[i])
            p3 = p.reshape(bsz, seq, -1)
            yc = gated_short_conv(p3, c_conv[i], half).reshape(bsz * seq, half)
            yd = dilated_attention(p3, 3 * half // HEAD_DIM, nh).reshape(bsz * seq, half)
            mix, w_out = (yc, yd), w_out_cd[i]
        x2d, h, logits = out_proj_norm_router(mix[0], mix[1], w_out.astype(BF16), x2d,
                                              norm_ffn[layer], router[layer])
        x2d = expert_choice_moe(x2d, h, logits, bsz, seq, layer, w_gate, w_up, w_down)
    return final_norm(x2d, norm_final).reshape(bsz, seq, d)
```

```python
import functools

import numpy as np
import jax
import jax.numpy as jnp
from jax import lax
from jax.experimental import pallas as pl
from jax.experimental.pallas import tpu as pltpu

F32 = jnp.float32
BF16 = jnp.bfloat16

HEAD_DIM = 128
CHUNK = 128
GRID_W = 64
NA_ROWS = 8
NA_COLS = 16
NA_QROWS = 4
NA_KROWS = NA_QROWS + NA_ROWS
CONV_W = 3
DIL_PATTERNS = ((128, 1), (512, 4), (2048, 16))
DIL_QBLK = 128
IN_PROJ_NORM_CHUNKS = 4
EPILOGUE_ROWS = 128
FFN_ROW_CHUNKS = 2
N_EXPERTS = 16
EC_CAPACITY_FACTOR = 2
RMS_EPS = 1e-6
LN_EPS = 1e-5
NEG_INF = -1e30
LOG2E = 1.4426950408889634
MIB = 1024 * 1024


def _params(semantics, vmem_mib):
    return pltpu.CompilerParams(dimension_semantics=semantics,
                                vmem_limit_bytes=int(vmem_mib * MIB))


def _rms_matmul_kernel(x_ref, g_ref, w_ref, o_ref, h_scr):
    j = pl.program_id(1)

    def first_column_tile():
        w = w_ref[...].astype(BF16)
        rows_per = x_ref.shape[0] // IN_PROJ_NORM_CHUNKS
        for c in range(IN_PROJ_NORM_CHUNKS):
            rows = slice(c * rows_per, (c + 1) * rows_per)
            x = x_ref[rows, :]
            ms = jnp.mean(x * x, axis=-1, keepdims=True)
            h = (x * lax.rsqrt(ms + RMS_EPS) * g_ref[...]).astype(BF16)
            h_scr[rows, :] = h
            o_ref[rows, :] = jnp.dot(h, w, preferred_element_type=F32).astype(o_ref.dtype)

    def later_column_tile():
        o_ref[...] = jnp.dot(h_scr[...], w_ref[...].astype(BF16), preferred_element_type=F32).astype(o_ref.dtype)

    pl.when(j == 0)(first_column_tile)
    pl.when(j > 0)(later_column_tile)


def rms_matmul(x2d, g, w, tm=1024, tn=1024):
    t, d = x2d.shape
    n = w.shape[1]
    return pl.pallas_call(
        _rms_matmul_kernel,
        grid=(t // tm, n // tn),
        in_specs=[pl.BlockSpec((tm, d), lambda i, j: (i, 0)),
                  pl.BlockSpec((1, d), lambda i, j: (0, 0)),
                  pl.BlockSpec((d, tn), lambda i, j: (0, j))],
        out_specs=pl.BlockSpec((tm, tn), lambda i, j: (i, j)),
        out_shape=jax.ShapeDtypeStruct((t, n), BF16),
        scratch_shapes=[pltpu.VMEM((tm, d), BF16)],
        compiler_params=_params(("parallel", "arbitrary"), 56),
        name="rms_in_proj",
    )(x2d, g.reshape(1, d), w)


def _sgu_kernel(u_ref, v_ref, lng_ref, ws_ref, bias_ref, o_ref):
    tm, width = u_ref.shape
    groups = width // HEAD_DIM
    nchunks = tm // CHUNK
    vf = jax.nn.gelu(v_ref[...].astype(F32))
    mu = jnp.mean(vf, axis=-1, keepdims=True)
    dv = vf - mu
    var = jnp.mean(dv * dv, axis=-1, keepdims=True)
    vn = (dv * lax.rsqrt(var + LN_EPS) * lng_ref[...]).astype(BF16)
    for g in range(groups):
        cols = slice(g * HEAD_DIM, (g + 1) * HEAD_DIM)
        vg = jnp.concatenate([vn[n * CHUNK:(n + 1) * CHUNK, cols] for n in range(nchunks)], axis=1)
        mixed = jnp.dot(ws_ref[g], vg, preferred_element_type=F32)
        for n in range(nchunks):
            rows = slice(n * CHUNK, (n + 1) * CHUNK)
            u = jax.nn.gelu(u_ref[rows, cols].astype(F32))
            o_ref[rows, cols] = (u * (mixed[:, n * CHUNK:(n + 1) * CHUNK] + bias_ref[:, cols])).astype(o_ref.dtype)


def sgu(p2d, ln_g, w_s, b_s, width, tm=512):
    t = p2d.shape[0]
    groups = width // HEAD_DIM
    bias_full = jnp.repeat(b_s.T.astype(F32), HEAD_DIM, axis=1)
    return pl.pallas_call(
        _sgu_kernel,
        grid=(t // tm,),
        in_specs=[pl.BlockSpec((tm, width), lambda i: (i, 0)),
                  pl.BlockSpec((tm, width), lambda i: (i, 1)),
                  pl.BlockSpec((1, width), lambda i: (0, 0)),
                  pl.BlockSpec((groups, CHUNK, CHUNK), lambda i: (0, 0, 0)),
                  pl.BlockSpec((CHUNK, width), lambda i: (0, 0))],
        out_specs=pl.BlockSpec((tm, width), lambda i: (i, 0)),
        out_shape=jax.ShapeDtypeStruct((t, width), BF16),
        compiler_params=_params(("parallel",), 40),
        name="sgu",
    )(p2d, p2d, ln_g.reshape(1, width).astype(F32), w_s.astype(BF16), bias_full)


def _na_block_layout(rows):
    nblk = rows // NA_QROWS
    kh = min(NA_ROWS, rows)
    starts, variants, var_key = [], [], {}
    for qb in range(nblk):
        r0 = qb * NA_QROWS
        ws = int(np.clip(r0 - kh // 2, 0, rows - NA_KROWS))
        rs = np.clip(np.arange(r0, r0 + NA_QROWS) - kh // 2, 0, rows - kh)
        key = (tuple(rs - ws), r0 - ws)
        if key not in var_key:
            var_key[key] = len(var_key)
        starts.append(ws)
        variants.append(var_key[key])
    return starts, variants, list(var_key.keys()), kh


def _na_tiles(rpb, rows):
    nh = rpb.shape[0]
    _, _, keys, kh = _na_block_layout(rows)
    qc = np.arange(GRID_W)[:, None]
    kc = np.arange(GRID_W)[None, :]
    col_start = np.clip(qc - NA_COLS // 2, 0, GRID_W - NA_COLS)
    col_valid = (kc >= col_start) & (kc < col_start + NA_COLS)
    dc_idx = np.clip(kc - qc + NA_COLS - 1, 0, 2 * NA_COLS - 2)
    onehot = (dc_idx[None] == np.arange(2 * NA_COLS - 1)[:, None, None]).astype(np.float32)
    expanded = jnp.einsum('hrc,cqk->hrqk', rpb.astype(F32) * LOG2E, jnp.asarray(onehot),
                          precision=lax.Precision.HIGHEST)
    tiles = jnp.where(jnp.asarray(col_valid)[None, None], expanded, NEG_INF)
    tiles = jnp.concatenate([tiles, jnp.full((nh, 1, GRID_W, GRID_W), NEG_INF, F32)], axis=1)
    invalid = 2 * NA_ROWS - 1
    dr = np.full((len(keys), NA_QROWS, NA_KROWS), invalid, np.int32)
    for v, (rs_rel, r0_rel) in enumerate(keys):
        for i in range(NA_QROWS):
            for j in range(NA_KROWS):
                if rs_rel[i] <= j < rs_rel[i] + kh:
                    dr[v, i, j] = j - (r0_rel + i) + NA_ROWS - 1
    return tiles, dr


def _na_kernel(q_ref, k_ref, v_ref, tiles_ref, o_ref, bias_scr, *, starts, variants, tile_idx, scale):
    qblk = NA_QROWS * GRID_W
    kblk = NA_KROWS * GRID_W

    @pl.when(pl.program_id(1) == 0)
    def _():
        for var in range(tile_idx.shape[0]):
            for i in range(NA_QROWS):
                strip = jnp.concatenate([tiles_ref[int(tile_idx[var, i, j])] for j in range(NA_KROWS)], axis=1)
                bias_scr[var, i * GRID_W:(i + 1) * GRID_W, :] = strip

    for qb, (ws, var) in enumerate(zip(starts, variants)):
        q = (q_ref[qb * qblk:(qb + 1) * qblk, :].astype(F32) * scale).astype(BF16)
        k = k_ref[ws * GRID_W:ws * GRID_W + kblk, :]
        v = v_ref[ws * GRID_W:ws * GRID_W + kblk, :]
        s = lax.dot_general(q, k, (((1,), (1,)), ((), ())), preferred_element_type=F32)
        s = s + bias_scr[var]
        m = jnp.max(s, axis=-1, keepdims=True)
        e = jnp.exp2(s - m)
        l = jnp.sum(e, axis=-1, keepdims=True)
        o = jnp.dot(e.astype(BF16), v, preferred_element_type=F32)
        o_ref[qb * qblk:(qb + 1) * qblk, :] = (o / l).astype(o_ref.dtype)


def neighbourhood_attention(p3, rpb, col0, nh):
    bsz, seq, _ = p3.shape
    rows = seq // GRID_W
    starts, variants, keys, _ = _na_block_layout(rows)
    tiles, tile_idx = _na_tiles(rpb, rows)
    nvar = len(keys)
    ntiles = tiles.shape[1]
    qblk, kblk = NA_QROWS * GRID_W, NA_KROWS * GRID_W
    kern = functools.partial(_na_kernel, starts=starts, variants=variants, tile_idx=tile_idx,
                             scale=HEAD_DIM ** -0.5 * LOG2E)
    head_spec = lambda off: pl.BlockSpec((None, seq, HEAD_DIM), lambda h, b: (b, 0, col0 + off + h))
    return pl.pallas_call(
        kern,
        grid=(nh, bsz),
        in_specs=[head_spec(0), head_spec(nh), head_spec(2 * nh),
                  pl.BlockSpec((None, ntiles, GRID_W, GRID_W), lambda h, b: (h, 0, 0, 0))],
        out_specs=pl.BlockSpec((None, seq, HEAD_DIM), lambda h, b: (b, 0, h)),
        out_shape=jax.ShapeDtypeStruct((bsz, seq, nh * HEAD_DIM), BF16),
        scratch_shapes=[pltpu.VMEM((nvar, qblk, kblk), F32)],
        compiler_params=_params(("parallel", "arbitrary"), 32),
        name="neighbourhood_attention",
    )(p3, p3, p3, tiles)


def _conv_kernel(b_ref, c_ref, x_ref, taps_ref, o_ref):
    seq = b_ref.shape[0]
    z = c_ref[...].astype(F32) * x_ref[...].astype(F32)
    row = lax.broadcasted_iota(jnp.int32, z.shape, 0)
    z_prev = jnp.where(row == 0, 0.0, pltpu.roll(z, 1, axis=0))
    z_next = jnp.where(row == seq - 1, 0.0, pltpu.roll(z, seq - 1, axis=0))
    y = taps_ref[0:1, :] * z_prev + taps_ref[1:2, :] * z + taps_ref[2:3, :] * z_next
    o_ref[...] = (b_ref[...].astype(F32) * y).astype(o_ref.dtype)


def gated_short_conv(p3, taps, width, tc=256):
    bsz, seq, _ = p3.shape
    nblk = width // tc
    spec = lambda off: pl.BlockSpec((None, seq, tc), lambda b, j: (b, 0, off * nblk + j))
    return pl.pallas_call(
        _conv_kernel,
        grid=(bsz, nblk),
        in_specs=[spec(0), spec(1), spec(2), pl.BlockSpec((CONV_W, tc), lambda b, j: (0, j))],
        out_specs=pl.BlockSpec((None, seq, tc), lambda b, j: (b, 0, j)),
        out_shape=jax.ShapeDtypeStruct((bsz, seq, width), BF16),
        compiler_params=_params(("parallel", "parallel"), 40),
        name="gated_short_conv",
    )(p3, p3, p3, taps.astype(F32))


def _dil_windows(length):
    nk = min(2 * DIL_QBLK, length)
    out = []
    for i in range(length // DIL_QBLK):
        q0 = i * DIL_QBLK
        k0 = int(np.clip(q0 - DIL_QBLK // 2, 0, length - nk))
        out.append((q0, k0, nk))
    return out


def _dil_tables(seq, nh):
    slopes = np.array([2.0 ** (-8.0 * (h + 1) / nh) for h in range(nh)], dtype=np.float32)
    var_index, absd_tabs, dil_of = {}, [], []
    plan = []
    for window, dil in DIL_PATTERNS:
        radius = window // (2 * dil)
        length = seq // dil
        blocks = []
        for q0, k0, nk in _dil_windows(length):
            key = (dil, radius, k0 - q0, nk)
            if key not in var_index:
                var_index[key] = len(absd_tabs)
                delta = (k0 + np.arange(2 * DIL_QBLK)[None, :]) - (q0 + np.arange(DIL_QBLK)[:, None])
                absd = np.abs(delta).astype(np.float32)
                valid = (np.abs(delta) <= radius) & (np.arange(2 * DIL_QBLK)[None, :] < nk)
                absd_tabs.append((absd * dil, valid))
            blocks.append((q0, k0, nk, var_index[key]))
        plan.append((dil, length, blocks))
    pen = np.stack([a for a, _ in absd_tabs])
    valid = np.stack([v for _, v in absd_tabs])
    bias = np.where(valid[None], -(slopes[:, None, None, None] * pen[None]).astype(np.float32), np.float32(NEG_INF))
    return plan, bias.astype(np.float32)


def _merge_lse2(oa, la, ob, lb):
    m = jnp.maximum(la, lb)
    wa = jnp.exp2(la - m)
    wb = jnp.exp2(lb - m)
    den = wa + wb
    return (wa * oa + wb * ob) / den, m + jnp.log2(den)


def _dil_kernel(q_ref, k_ref, v_ref, bias_ref, o_ref, qf, kf, vf, qd, kd, vd, o1, l1, od, ld, oc, lc, on, ln, *,
                plan, scale):
    (_, _, blocks1), (d, len_d, blocks_d), (_, len_c, blocks_c) = plan
    qf[...] = q_ref[...].astype(F32) * scale
    kf[...] = k_ref[...].astype(F32)
    vf[...] = v_ref[...].astype(F32)

    def attend(q, k, v, var, nk):
        s = lax.dot_general(q, k, (((1,), (1,)), ((), ())), preferred_element_type=F32)
        s = s + bias_ref[var][:, :nk]
        m = jnp.max(s, axis=-1, keepdims=True)
        e = jnp.exp2(s - m)
        l = jnp.sum(e, axis=-1, keepdims=True)
        o = jnp.dot(e.astype(BF16), v, preferred_element_type=F32) / l
        return o, jnp.broadcast_to(m + jnp.log2(l), o.shape)

    for q0, k0, nk, var in blocks1:
        o, lse = attend(qf[q0:q0 + DIL_QBLK, :].astype(BF16), kf[k0:k0 + nk, :].astype(BF16),
                        vf[k0:k0 + nk, :].astype(BF16), var, nk)
        o1[q0:q0 + DIL_QBLK, :] = o
        l1[q0:q0 + DIL_QBLK, :] = lse

    for r in range(d):
        qd[r] = qf[pl.ds(r, len_d, stride=d), :]
        kd[r] = kf[pl.ds(r, len_d, stride=d), :]
        vd[r] = vf[pl.ds(r, len_d, stride=d), :]

    for r in range(d):
        qs, ks, vs = qd[r].astype(BF16), kd[r].astype(BF16), vd[r].astype(BF16)
        for q0, k0, nk, var in blocks_d:
            o, lse = attend(qs[q0:q0 + DIL_QBLK], ks[k0:k0 + nk], vs[k0:k0 + nk], var, nk)
            od[r, q0:q0 + DIL_QBLK, :] = o
            ld[r, q0:q0 + DIL_QBLK, :] = lse
        for a in range(d):
            qs = qd[r, pl.ds(a, len_c, stride=d), :].astype(BF16)
            ks = kd[r, pl.ds(a, len_c, stride=d), :].astype(BF16)
            vs = vd[r, pl.ds(a, len_c, stride=d), :].astype(BF16)
            for q0, k0, nk, var in blocks_c:
                o, lse = attend(qs[q0:q0 + DIL_QBLK], ks[k0:k0 + nk], vs[k0:k0 + nk], var, nk)
                oc[r, pl.ds(q0 * d + a, DIL_QBLK, stride=d), :] = o
                lc[r, pl.ds(q0 * d + a, DIL_QBLK, stride=d), :] = lse

    for r in range(d):
        o, lse = _merge_lse2(od[r], ld[r], oc[r], lc[r])
        on[pl.ds(r, len_d, stride=d), :] = o
        ln[pl.ds(r, len_d, stride=d), :] = lse

    o, _ = _merge_lse2(o1[...], l1[...], on[...], ln[...])
    o_ref[...] = o.astype(o_ref.dtype)


def dilated_attention(p3, col0, nh):
    bsz, seq, _ = p3.shape
    plan, bias_np = _dil_tables(seq, nh)
    dils = [dil for dil, _, _ in plan]
    assert len(dils) == 3 and dils[0] == 1 and dils[2] == dils[1] * dils[1], dils
    d = dils[1]
    nvar = bias_np.shape[1]
    kern = functools.partial(_dil_kernel, plan=plan, scale=HEAD_DIM ** -0.5 * LOG2E)
    head_spec = lambda off: pl.BlockSpec((None, seq, HEAD_DIM), lambda h, b: (b, 0, col0 + off + h))
    natural = pltpu.VMEM((seq, HEAD_DIM), F32)
    by_residue = pltpu.VMEM((d, seq // d, HEAD_DIM), F32)
    return pl.pallas_call(
        kern,
        grid=(nh, bsz),
        in_specs=[head_spec(0), head_spec(nh), head_spec(2 * nh),
                  pl.BlockSpec((None, nvar, DIL_QBLK, 2 * DIL_QBLK), lambda h, b: (h, 0, 0, 0))],
        out_specs=pl.BlockSpec((None, seq, HEAD_DIM), lambda h, b: (b, 0, h)),
        out_shape=jax.ShapeDtypeStruct((bsz, seq, nh * HEAD_DIM), BF16),
        scratch_shapes=[natural] * 3 + [by_residue] * 3 + [natural] * 2 + [by_residue] * 4 + [natural] * 2,
        compiler_params=_params(("parallel", "parallel"), 40),
        name="dilated_attention",
    )(p3, p3, p3, jnp.asarray(bias_np * np.float32(LOG2E)))


def _out_proj_router_kernel(ya_ref, yb_ref, wa_ref, wb_ref, x_ref, g_ref, r_ref, xo_ref, h_ref, lg_ref):
    acc = jnp.dot(ya_ref[...], wa_ref[...], preferred_element_type=F32)
    acc = acc + jnp.dot(yb_ref[...], wb_ref[...], preferred_element_type=F32)
    ne = lg_ref.shape[1]
    for c in range(acc.shape[0] // EPILOGUE_ROWS):
        rows = slice(c * EPILOGUE_ROWS, (c + 1) * EPILOGUE_ROWS)
        xn = x_ref[rows, :] + acc[rows, :]
        xo_ref[rows, :] = xn
        ms = jnp.mean(xn * xn, axis=-1, keepdims=True)
        h = xn * lax.rsqrt(ms + RMS_EPS) * g_ref[...]
        h_hi = h.astype(BF16)
        h_ref[rows, :] = h_hi
        h_lo = (h - h_hi.astype(F32)).astype(BF16)
        s = (jnp.dot(h_hi, r_ref[...], preferred_element_type=F32)
             + jnp.dot(h_lo, r_ref[...], preferred_element_type=F32))
        lg_ref[rows, :] = s[:, :ne] + s[:, ne:]


def out_proj_norm_router(ya, yb, w_bf16, x2d, g, router, tm=512):
    t, d = x2d.shape
    ka = ya.shape[1]
    kb = yb.shape[1]
    assert ka == kb
    ne = router.shape[1]
    r_hi = router.astype(BF16)
    r_lo = (router - r_hi.astype(F32)).astype(BF16)
    r_split = jnp.concatenate([r_hi, r_lo], axis=1)
    return pl.pallas_call(
        _out_proj_router_kernel,
        grid=(t // tm,),
        in_specs=[pl.BlockSpec((tm, ka), lambda i: (i, 0)),
                  pl.BlockSpec((tm, kb), lambda i: (i, 0)),
                  pl.BlockSpec((ka, d), lambda i: (0, 0)),
                  pl.BlockSpec((kb, d), lambda i: (1, 0)),
                  pl.BlockSpec((tm, d), lambda i: (i, 0)),
                  pl.BlockSpec((1, d), lambda i: (0, 0)),
                  pl.BlockSpec((d, 2 * ne), lambda i: (0, 0))],
        out_specs=[pl.BlockSpec((tm, d), lambda i: (i, 0)),
                   pl.BlockSpec((tm, d), lambda i: (i, 0)),
                   pl.BlockSpec((tm, ne), lambda i: (i, 0))],
        out_shape=[jax.ShapeDtypeStruct((t, d), F32),
                   jax.ShapeDtypeStruct((t, d), BF16),
                   jax.ShapeDtypeStruct((t, ne), F32)],
        compiler_params=_params(("parallel",), 56),
        name="out_proj_norm_router",
    )(ya, yb, w_bf16, w_bf16, x2d, g.reshape(1, d), r_split)


def _route_kernel(lg_ref, slot_ref, aff_ref, *, cap):
    nb, ne, seq = aff_ref.shape
    logits = lg_ref[...].reshape(nb, ne, seq)
    mx = jnp.max(logits, axis=1, keepdims=True)
    ex = jnp.exp(logits - mx)
    a3 = ex / jnp.sum(ex, axis=1, keepdims=True)
    aff_ref[...] = a3
    a = a3.reshape(nb * ne, seq)
    nrows = nb * ne
    lanes = 128
    r = lax.broadcasted_iota(jnp.int32, (lanes, lanes), 0)
    c = lax.broadcasted_iota(jnp.int32, (lanes, lanes), 1)
    tri = jnp.where(r <= c, 1.0, 0.0).astype(BF16)

    def count(mask_f):
        return jnp.sum(mask_f, axis=1, keepdims=True)

    def prefix(x):
        parts = []
        run = jnp.zeros((nrows, 1), F32)
        for j in range(seq // lanes):
            blk = x[:, j * lanes:(j + 1) * lanes]
            inc = jnp.dot(blk.astype(BF16), tri, preferred_element_type=F32)
            parts.append(inc - blk + run)
            run = run + count(blk)
        return jnp.concatenate(parts, axis=1)

    def body(i, ans):
        cand = ans | jnp.left_shift(jnp.int32(1), 30 - i)
        cnt = count(jnp.where(a >= lax.bitcast_convert_type(cand, F32), 1.0, 0.0))
        return jnp.where(cnt >= cap, cand, ans)

    ans = lax.fori_loop(0, 31, body, jnp.zeros((nrows, 1), jnp.int32))
    thr = lax.bitcast_convert_type(ans, F32)
    gt = jnp.where(a > thr, 1.0, 0.0)
    eq = jnp.where(a >= thr, 1.0, 0.0) - gt
    need = cap - count(gt)
    sel = gt + eq * jnp.where(prefix(eq) < need, 1.0, 0.0)
    slot_ref[...] = jnp.where(sel > 0.5, prefix(sel), -1.0).astype(jnp.int32).reshape(nb, ne, seq)


def route(logits_rows, bsz, ne, seq, cap):
    return pl.pallas_call(
        functools.partial(_route_kernel, cap=cap),
        grid=(1,),
        in_specs=[pl.BlockSpec((bsz * ne, seq), lambda i: (0, 0))],
        out_specs=[pl.BlockSpec((bsz, ne, seq), lambda i: (0, 0, 0)),
                   pl.BlockSpec((bsz, ne, seq), lambda i: (0, 0, 0))],
        out_shape=[jax.ShapeDtypeStruct((bsz, ne, seq), jnp.int32),
                   jax.ShapeDtypeStruct((bsz, ne, seq), F32)],
        compiler_params=_params(("arbitrary",), 40),
        name="route",
    )(logits_rows)


def _gather_kernel(slot_ref, aff_ref, h_ref, xe_ref, gate_ref, *, cap, ncol):
    eg, seq = slot_ref.shape
    d = h_ref.shape[1]
    cidx = lax.broadcasted_iota(jnp.int32, (cap, seq), 0)
    rows = []
    for j in range(eg):
        hit = cidx == slot_ref[j:j + 1, :]
        rows.append(jnp.where(hit, 1.0, 0.0).astype(BF16))
        gate_ref[j] = jnp.sum(jnp.where(hit, aff_ref[j:j + 1, :], 0.0), axis=1, keepdims=True)
    p = jnp.concatenate(rows, axis=0)
    dc = d // ncol
    for c in range(ncol):
        xe = jnp.dot(p, h_ref[:, c * dc:(c + 1) * dc], preferred_element_type=F32)
        xe_ref[:, :, c * dc:(c + 1) * dc] = xe.reshape(eg, cap, dc).astype(xe_ref.dtype)


def gather_rows(slot, aff, h, cap, egroup=4, ncol=2):
    bsz, ne, seq = slot.shape
    d = h.shape[1]
    ngroups = ne // egroup
    grouped = lambda a: a.reshape(bsz, ngroups, egroup, seq)
    return pl.pallas_call(
        functools.partial(_gather_kernel, cap=cap, ncol=ncol),
        grid=(bsz, ngroups),
        in_specs=[pl.BlockSpec((None, None, egroup, seq), lambda b, g: (b, g, 0, 0)),
                  pl.BlockSpec((None, None, egroup, seq), lambda b, g: (b, g, 0, 0)),
                  pl.BlockSpec((seq, d), lambda b, g: (b, 0))],
        out_specs=[pl.BlockSpec((egroup, None, cap, d), lambda b, g: (g, b, 0, 0)),
                   pl.BlockSpec((egroup, None, cap, 1), lambda b, g: (g, b, 0, 0))],
        out_shape=[jax.ShapeDtypeStruct((ne, bsz, cap, d), BF16),
                   jax.ShapeDtypeStruct((ne, bsz, cap, 1), F32)],
        compiler_params=_params(("parallel", "parallel"), 48),
        name="moe_gather",
    )(grouped(slot), grouped(aff), h)


def _combine_kernel(slot_t_ref, ye_ref, x_ref, o_ref, *, cap):
    tb, ne = slot_t_ref.shape
    tn = x_ref.shape[1]
    cidx = lax.broadcasted_iota(jnp.int32, (tb, cap), 1)
    pt = jnp.concatenate([jnp.where(cidx == slot_t_ref[:, e:e + 1], 1.0, 0.0).astype(BF16) for e in range(ne)],
                         axis=1)
    ye = ye_ref[...].reshape(ne * cap, tn)
    o_ref[...] = x_ref[...] + jnp.dot(pt, ye, preferred_element_type=F32)


def combine(slot_t, ye, x2d, cap, tn=1024, tb=512):
    bsz, seq, ne = slot_t.shape
    d = x2d.shape[1]
    nt = seq // tb
    return pl.pallas_call(
        functools.partial(_combine_kernel, cap=cap),
        grid=(bsz, d // tn, nt),
        in_specs=[pl.BlockSpec((None, tb, ne), lambda b, n, t: (b, t, 0)),
                  pl.BlockSpec((ne, None, cap, tn), lambda b, n, t: (0, b, 0, n)),
                  pl.BlockSpec((tb, tn), lambda b, n, t: (b * nt + t, n))],
        out_specs=pl.BlockSpec((tb, tn), lambda b, n, t: (b * nt + t, n)),
        out_shape=jax.ShapeDtypeStruct(x2d.shape, F32),
        compiler_params=_params(("parallel", "parallel", "parallel"), 48),
        name="moe_combine",
    )(slot_t, ye, x2d)


def _expert_ffn_kernel(xe_ref, wg_ref, wu_ref, wd_ref, gate_ref, o_ref, acc_ref):
    f = pl.program_id(2)
    last = pl.num_programs(2) - 1
    rows_per = xe_ref.shape[0] // FFN_ROW_CHUNKS

    def step(first, final):
        wg = wg_ref[...].astype(BF16)
        wu = wu_ref[...].astype(BF16)
        wd = wd_ref[...].astype(BF16)
        for c in range(FFN_ROW_CHUNKS):
            rows = slice(c * rows_per, (c + 1) * rows_per)
            xe = xe_ref[rows, :]
            g = jnp.dot(xe, wg, preferred_element_type=F32)
            u = jnp.dot(xe, wu, preferred_element_type=F32)
            hid = (jax.nn.silu(g) * u).astype(BF16)
            part = jnp.dot(hid, wd, preferred_element_type=F32)
            total = part if first else acc_ref[rows, :] + part
            if final:
                o_ref[rows, :] = (total * gate_ref[rows, :]).astype(o_ref.dtype)
            else:
                acc_ref[rows, :] = total

    pl.when(f == 0)(lambda: step(True, False))
    pl.when(jnp.logical_and(f > 0, f < last))(lambda: step(False, False))
    pl.when(f == last)(lambda: step(False, True))


def expert_ffn(xe, w_gate, w_up, w_down, layer, gate, tm=1024, tf=256):
    ne, r, d = xe.shape
    fdim = w_gate.shape[3]
    assert fdim // tf >= 2, "the kernel distinguishes first / middle / last hidden tiles"
    return pl.pallas_call(
        _expert_ffn_kernel,
        grid=(ne, r // tm, fdim // tf),
        in_specs=[pl.BlockSpec((None, tm, d), lambda e, m, f: (e, m, 0)),
                  pl.BlockSpec((None, None, d, tf), lambda e, m, f: (layer, e, 0, f)),
                  pl.BlockSpec((None, None, d, tf), lambda e, m, f: (layer, e, 0, f)),
                  pl.BlockSpec((None, None, tf, d), lambda e, m, f: (layer, e, f, 0)),
                  pl.BlockSpec((None, tm, 1), lambda e, m, f: (e, m, 0))],
        out_specs=pl.BlockSpec((None, tm, d), lambda e, m, f: (e, m, 0)),
        out_shape=jax.ShapeDtypeStruct((ne, r, d), BF16),
        scratch_shapes=[pltpu.VMEM((tm, d), F32)],
        compiler_params=_params(("parallel", "parallel", "arbitrary"), 56),
        name="expert_ffn",
    )(xe, w_gate, w_up, w_down, gate)


def _final_norm_kernel(x_ref, g_ref, o_ref):
    x = x_ref[...]
    ms = jnp.mean(x * x, axis=-1, keepdims=True)
    o_ref[...] = x * lax.rsqrt(ms + RMS_EPS) * g_ref[...]


def final_norm(x2d, g, tm=512):
    t, d = x2d.shape
    return pl.pallas_call(
        _final_norm_kernel,
        grid=(t // tm,),
        in_specs=[pl.BlockSpec((tm, d), lambda i: (i, 0)), pl.BlockSpec((1, d), lambda i: (0, 0))],
        out_specs=pl.BlockSpec((tm, d), lambda i: (i, 0)),
        out_shape=jax.ShapeDtypeStruct((t, d), F32),
        compiler_params=_params(("parallel",), 32),
        name="final_norm",
    )(x2d, g.reshape(1, d))


def expert_choice_moe(x2d, h, logits, bsz, seq, layer, w_gate, w_up, w_down):
    d = x2d.shape[1]
    ne = logits.shape[1]
    cap = EC_CAPACITY_FACTOR * seq // ne
    logits_rows = logits.reshape(bsz, seq, ne).transpose(0, 2, 1).reshape(bsz * ne, seq)
    slot, aff = route(logits_rows, bsz, ne, seq, cap)
    xe, gate = gather_rows(slot, aff, h, cap)
    ye = expert_ffn(xe.reshape(ne, bsz * cap, d), w_gate, w_up, w_down, layer,
                    gate.reshape(ne, bsz * cap, 1))
    return combine(slot.transpose(0, 2, 1), ye.reshape(ne, bsz, cap, d), x2d, cap)


def kernel(x, norm_mix, norm_ffn, norm_final, w_in_ab, a_v_norm, a_spatial_w, a_spatial_b, b_rpb, w_out_ab,
           w_in_cd, c_conv, w_out_cd, router, w_gate, w_up, w_down):
    bsz, seq, d = x.shape
    depth = norm_mix.shape[0]
    half = d // 2
    nh = half // HEAD_DIM
    x2d = x.reshape(bsz * seq, d)
    for layer in range(depth):
        i = layer // 2
        if layer % 2 == 0:
            p = rms_matmul(x2d, norm_mix[layer], w_in_ab[i])
            p3 = p.reshape(bsz, seq, -1)
            ya = sgu(p, a_v_norm[i], a_spatial_w[i], a_spatial_b[i], half)
            yb = neighbourhood_attention(p3, b_rpb[i], 2 * half // HEAD_DIM, nh).reshape(bsz * seq, half)
            mix, w_out = (ya, yb), w_out_ab[i]
        else:
            p = rms_matmul(x2d, norm_mix[layer], w_in_cd[i])
            p3 = p.reshape(bsz, seq, -1)
            yc = gated_short_conv(p3, c_conv[i], half).reshape(bsz * seq, half)
            yd = dilated_attention(p3, 3 * half // HEAD_DIM, nh).reshape(bsz * seq, half)
            mix, w_out = (yc, yd), w_out_cd[i]
        x2d, h, logits = out_proj_norm_router(mix[0], mix[1], w_out.astype(BF16), x2d,
                                              norm_ffn[layer], router[layer])
        x2d = expert_choice_moe(x2d, h, logits, bsz, seq, layer, w_gate, w_up, w_down)
    return final_norm(x2d, norm_final).reshape(bsz, seq, d)
```

```python
import functools

import numpy as np
import jax
import jax.numpy as jnp
from jax import lax
from jax.experimental import pallas as pl
from jax.experimental.pallas import tpu as pltpu

F32 = jnp.float32
BF16 = jnp.bfloat16

HEAD_DIM = 128
CHUNK = 128
GRID_W = 64
NA_ROWS = 8
NA_COLS = 16
NA_QROWS = 4
NA_KROWS = NA_QROWS + NA_ROWS
CONV_W = 3
DIL_PATTERNS = ((128, 1), (512, 4), (2048, 16))
DIL_QBLK = 128
IN_PROJ_NORM_CHUNKS = 4
EPILOGUE_ROWS = 128
FFN_ROW_CHUNKS = 2
N_EXPERTS = 16
EC_CAPACITY_FACTOR = 2
RMS_EPS = 1e-6
LN_EPS = 1e-5
NEG_INF = -1e30
LOG2E = 1.4426950408889634
MIB = 1024 * 1024


def _params(semantics, vmem_mib):
    return pltpu.CompilerParams(dimension_semantics=semantics,
                                vmem_limit_bytes=int(vmem_mib * MIB))


def _rms_matmul_kernel(x_ref, g_ref, w_ref, o_ref, h_scr):
    j = pl.program_id(1)

    def first_column_tile():
        w = w_ref[...].astype(BF16)
        rows_per = x_ref.shape[0] // IN_PROJ_NORM_CHUNKS
        for c in range(IN_PROJ_NORM_CHUNKS):
            rows = slice(c * rows_per, (c + 1) * rows_per)
            x = x_ref[rows, :]
            ms = jnp.mean(x * x, axis=-1, keepdims=True)
            h = (x * lax.rsqrt(ms + RMS_EPS) * g_ref[...]).astype(BF16)
            h_scr[rows, :] = h
            o_ref[rows, :] = jnp.dot(h, w, preferred_element_type=F32).astype(o_ref.dtype)

    def later_column_tile():
        o_ref[...] = jnp.dot(h_scr[...], w_ref[...].astype(BF16), preferred_element_type=F32).astype(o_ref.dtype)

    pl.when(j == 0)(first_column_tile)
    pl.when(j > 0)(later_column_tile)


def rms_matmul(x2d, g, w, tm=1024, tn=1024):
    t, d = x2d.shape
    n = w.shape[1]
    return pl.pallas_call(
        _rms_matmul_kernel,
        grid=(t // tm, n // tn),
        in_specs=[pl.BlockSpec((tm, d), lambda i, j: (i, 0)),
                  pl.BlockSpec((1, d), lambda i, j: (0, 0)),
                  pl.BlockSpec((d, tn), lambda i, j: (0, j))],
        out_specs=pl.BlockSpec((tm, tn), lambda i, j: (i, j)),
        out_shape=jax.ShapeDtypeStruct((t, n), BF16),
        scratch_shapes=[pltpu.VMEM((tm, d), BF16)],
        compiler_params=_params(("parallel", "arbitrary"), 56),
        name="rms_in_proj",
    )(x2d, g.reshape(1, d), w)


def _sgu_kernel(u_ref, v_ref, lng_ref, ws_ref, bias_ref, o_ref):
    tm, width = u_ref.shape
    groups = width // HEAD_DIM
    nchunks = tm // CHUNK
    vf = jax.nn.gelu(v_ref[...].astype(F32))
    mu = jnp.mean(vf, axis=-1, keepdims=True)
    dv = vf - mu
    var = jnp.mean(dv * dv, axis=-1, keepdims=True)
    vn = (dv * lax.rsqrt(var + LN_EPS) * lng_ref[...]).astype(BF16)
    for g in range(groups):
        cols = slice(g * HEAD_DIM, (g + 1) * HEAD_DIM)
        vg = jnp.concatenate([vn[n * CHUNK:(n + 1) * CHUNK, cols] for n in range(nchunks)], axis=1)
        mixed = jnp.dot(ws_ref[g], vg, preferred_element_type=F32)
        for n in range(nchunks):
            rows = slice(n * CHUNK, (n + 1) * CHUNK)
            u = jax.nn.gelu(u_ref[rows, cols].astype(F32))
            o_ref[rows, cols] = (u * (mixed[:, n * CHUNK:(n + 1) * CHUNK] + bias_ref[:, cols])).astype(o_ref.dtype)


def sgu(p2d, ln_g, w_s, b_s, width, tm=512):
    t = p2d.shape[0]
    groups = width // HEAD_DIM
    bias_full = jnp.repeat(b_s.T.astype(F32), HEAD_DIM, axis=1)
    return pl.pallas_call(
        _sgu_kernel,
        grid=(t // tm,),
        in_specs=[pl.BlockSpec((tm, width), lambda i: (i, 0)),
                  pl.BlockSpec((tm, width), lambda i: (i, 1)),
                  pl.BlockSpec((1, width), lambda i: (0, 0)),
                  pl.BlockSpec((groups, CHUNK, CHUNK), lambda i: (0, 0, 0)),
                  pl.BlockSpec((CHUNK, width), lambda i: (0, 0))],
        out_specs=pl.BlockSpec((tm, width), lambda i: (i, 0)),
        out_shape=jax.ShapeDtypeStruct((t, width), BF16),
        compiler_params=_params(("parallel",), 40),
        name="sgu",
    )(p2d, p2d, ln_g.reshape(1, width).astype(F32), w_s.astype(BF16), bias_full)


def _na_block_layout(rows):
    nblk = rows // NA_QROWS
    kh = min(NA_ROWS, rows)
    starts, variants, var_key = [], [], {}
    for qb in range(nblk):
        r0 = qb * NA_QROWS
        ws = int(np.clip(r0 - kh // 2, 0, rows - NA_KROWS))
        rs = np.clip(np.arange(r0, r0 + NA_QROWS) - kh // 2, 0, rows - kh)
        key = (tuple(rs - ws), r0 - ws)
        if key not in var_key:
            var_key[key] = len(var_key)
        starts.append(ws)
        variants.append(var_key[key])
    return starts, variants, list(var_key.keys()), kh


def _na_tiles(rpb, rows):
    nh = rpb.shape[0]
    _, _, keys, kh = _na_block_layout(rows)
    qc = np.arange(GRID_W)[:, None]
    kc = np.arange(GRID_W)[None, :]
    col_start = np.clip(qc - NA_COLS // 2, 0, GRID_W - NA_COLS)
    col_valid = (kc >= col_start) & (kc < col_start + NA_COLS)
    dc_idx = np.clip(kc - qc + NA_COLS - 1, 0, 2 * NA_COLS - 2)
    onehot = (dc_idx[None] == np.arange(2 * NA_COLS - 1)[:, None, None]).astype(np.float32)
    expanded = jnp.einsum('hrc,cqk->hrqk', rpb.astype(F32) * LOG2E, jnp.asarray(onehot),
                          precision=lax.Precision.HIGHEST)
    tiles = jnp.where(jnp.asarray(col_valid)[None, None], expanded, NEG_INF)
    tiles = jnp.concatenate([tiles, jnp.full((nh, 1, GRID_W, GRID_W), NEG_INF, F32)], axis=1)
    invalid = 2 * NA_ROWS - 1
    dr = np.full((len(keys), NA_QROWS, NA_KROWS), invalid, np.int32)
    for v, (rs_rel, r0_rel) in enumerate(keys):
        for i in range(NA_QROWS):
            for j in range(NA_KROWS):
                if rs_rel[i] <= j < rs_rel[i] + kh:
                    dr[v, i, j] = j - (r0_rel + i) + NA_ROWS - 1
    return tiles, dr


def _na_kernel(q_ref, k_ref, v_ref, tiles_ref, o_ref, bias_scr, *, starts, variants, tile_idx, scale):
    qblk = NA_QROWS * GRID_W
    kblk = NA_KROWS * GRID_W

    @pl.when(pl.program_id(1) == 0)
    def _():
        for var in range(tile_idx.shape[0]):
            for i in range(NA_QROWS):
                strip = jnp.concatenate([tiles_ref[int(tile_idx[var, i, j])] for j in range(NA_KROWS)], axis=1)
                bias_scr[var, i * GRID_W:(i + 1) * GRID_W, :] = strip

    for qb, (ws, var) in enumerate(zip(starts, variants)):
        q = (q_ref[qb * qblk:(qb + 1) * qblk, :].astype(F32) * scale).astype(BF16)
        k = k_ref[ws * GRID_W:ws * GRID_W + kblk, :]
        v = v_ref[ws * GRID_W:ws * GRID_W + kblk, :]
        s = lax.dot_general(q, k, (((1,), (1,)), ((), ())), preferred_element_type=F32)
        s = s + bias_scr[var]
        m = jnp.max(s, axis=-1, keepdims=True)
        e = jnp.exp2(s - m).astype(BF16)
        ol = jnp.dot(e, jnp.concatenate([v, jnp.ones_like(v)], axis=1), preferred_element_type=F32)
        o_ref[qb * qblk:(qb + 1) * qblk, :] = (ol[:, :HEAD_DIM] / ol[:, HEAD_DIM:]).astype(o_ref.dtype)


def neighbourhood_attention(p3, rpb, col0, nh):
    bsz, seq, _ = p3.shape
    rows = seq // GRID_W
    starts, variants, keys, _ = _na_block_layout(rows)
    tiles, tile_idx = _na_tiles(rpb, rows)
    nvar = len(keys)
    ntiles = tiles.shape[1]
    qblk, kblk = NA_QROWS * GRID_W, NA_KROWS * GRID_W
    kern = functools.partial(_na_kernel, starts=starts, variants=variants, tile_idx=tile_idx,
                             scale=HEAD_DIM ** -0.5 * LOG2E)
    head_spec = lambda off: pl.BlockSpec((None, seq, HEAD_DIM), lambda h, b: (b, 0, col0 + off + h))
    return pl.pallas_call(
        kern,
        grid=(nh, bsz),
        in_specs=[head_spec(0), head_spec(nh), head_spec(2 * nh),
                  pl.BlockSpec((None, ntiles, GRID_W, GRID_W), lambda h, b: (h, 0, 0, 0))],
        out_specs=pl.BlockSpec((None, seq, HEAD_DIM), lambda h, b: (b, 0, h)),
        out_shape=jax.ShapeDtypeStruct((bsz, seq, nh * HEAD_DIM), BF16),
        scratch_shapes=[pltpu.VMEM((nvar, qblk, kblk), F32)],
        compiler_params=_params(("parallel", "arbitrary"), 32),
        name="neighbourhood_attention",
    )(p3, p3, p3, tiles)


def _conv_kernel(b_ref, c_ref, x_ref, taps_ref, o_ref):
    seq = b_ref.shape[0]
    z = c_ref[...].astype(F32) * x_ref[...].astype(F32)
    row = lax.broadcasted_iota(jnp.int32, z.shape, 0)
    z_prev = jnp.where(row == 0, 0.0, pltpu.roll(z, 1, axis=0))
    z_next = jnp.where(row == seq - 1, 0.0, pltpu.roll(z, seq - 1, axis=0))
    y = taps_ref[0:1, :] * z_prev + taps_ref[1:2, :] * z + taps_ref[2:3, :] * z_next
    o_ref[...] = (b_ref[...].astype(F32) * y).astype(o_ref.dtype)


def gated_short_conv(p3, taps, width, tc=256):
    bsz, seq, _ = p3.shape
    nblk = width // tc
    spec = lambda off: pl.BlockSpec((None, seq, tc), lambda b, j: (b, 0, off * nblk + j))
    return pl.pallas_call(
        _conv_kernel,
        grid=(bsz, nblk),
        in_specs=[spec(0), spec(1), spec(2), pl.BlockSpec((CONV_W, tc), lambda b, j: (0, j))],
        out_specs=pl.BlockSpec((None, seq, tc), lambda b, j: (b, 0, j)),
        out_shape=jax.ShapeDtypeStruct((bsz, seq, width), BF16),
        compiler_params=_params(("parallel", "parallel"), 40),
        name="gated_short_conv",
    )(p3, p3, p3, taps.astype(F32))


def _dil_windows(length):
    nk = min(2 * DIL_QBLK, length)
    out = []
    for i in range(length // DIL_QBLK):
        q0 = i * DIL_QBLK
        k0 = int(np.clip(q0 - DIL_QBLK // 2, 0, length - nk))
        out.append((q0, k0, nk))
    return out


def _dil_tables(seq, nh):
    slopes = np.array([2.0 ** (-8.0 * (h + 1) / nh) for h in range(nh)], dtype=np.float32)
    var_index, absd_tabs, dil_of = {}, [], []
    plan = []
    for window, dil in DIL_PATTERNS:
        radius = window // (2 * dil)
        length = seq // dil
        blocks = []
        for q0, k0, nk in _dil_windows(length):
            key = (dil, radius, k0 - q0, nk)
            if key not in var_index:
                var_index[key] = len(absd_tabs)
                delta = (k0 + np.arange(2 * DIL_QBLK)[None, :]) - (q0 + np.arange(DIL_QBLK)[:, None])
                absd = np.abs(delta).astype(np.float32)
                valid = (np.abs(delta) <= radius) & (np.arange(2 * DIL_QBLK)[None, :] < nk)
                absd_tabs.append((absd * dil, valid))
            blocks.append((q0, k0, nk, var_index[key]))
        plan.append((dil, length, blocks))
    pen = np.stack([a for a, _ in absd_tabs])
    valid = np.stack([v for _, v in absd_tabs])
    bias = np.where(valid[None], -(slopes[:, None, None, None] * pen[None]).astype(np.float32), np.float32(NEG_INF))
    return plan, bias.astype(np.float32)


def _merge_partial(a, b):
    (oa, ma, la), (ob, mb, lb) = a, b
    a_is_max = ma >= mb
    t = jnp.exp2(jnp.minimum(ma, mb) - jnp.maximum(ma, mb))
    ta = jnp.where(a_is_max, 1.0, t)
    tb = jnp.where(a_is_max, t, 1.0)
    return ta * oa + tb * ob, jnp.maximum(ma, mb), ta * la + tb * lb


def _dil_kernel(q_ref, k_ref, v_ref, bias_ref, o_ref, qf, kf, vf, qd, kd, vd, p1, pd, pc, pn, *, plan, scale):
    (_, _, blocks1), (d, len_d, blocks_d), (_, len_c, blocks_c) = plan
    qf[...] = q_ref[...].astype(F32) * scale
    kf[...] = k_ref[...].astype(F32)
    vf[...] = v_ref[...].astype(F32)

    def attend(q, k, v, var, nk):
        s = lax.dot_general(q, k, (((1,), (1,)), ((), ())), preferred_element_type=F32)
        s = s + bias_ref[var][:, :nk]
        m = jnp.max(s, axis=-1, keepdims=True)
        e = jnp.exp2(s - m).astype(BF16)
        ol = jnp.dot(e, jnp.concatenate([v, jnp.ones_like(v)], axis=1), preferred_element_type=F32)
        o = ol[:, :HEAD_DIM]
        return o, jnp.broadcast_to(m, o.shape), ol[:, HEAD_DIM:]

    for q0, k0, nk, var in blocks1:
        part = attend(qf[q0:q0 + DIL_QBLK, :].astype(BF16), kf[k0:k0 + nk, :].astype(BF16),
                      vf[k0:k0 + nk, :].astype(BF16), var, nk)
        for j in range(3):
            p1[j, q0:q0 + DIL_QBLK, :] = part[j]

    for r in range(d):
        qd[r] = qf[pl.ds(r, len_d, stride=d), :]
        kd[r] = kf[pl.ds(r, len_d, stride=d), :]
        vd[r] = vf[pl.ds(r, len_d, stride=d), :]

    for r in range(d):
        qs, ks, vs = qd[r].astype(BF16), kd[r].astype(BF16), vd[r].astype(BF16)
        for q0, k0, nk, var in blocks_d:
            part = attend(qs[q0:q0 + DIL_QBLK], ks[k0:k0 + nk], vs[k0:k0 + nk], var, nk)
            for j in range(3):
                pd[j, r, q0:q0 + DIL_QBLK, :] = part[j]
        for a in range(d):
            qs = qd[r, pl.ds(a, len_c, stride=d), :].astype(BF16)
            ks = kd[r, pl.ds(a, len_c, stride=d), :].astype(BF16)
            vs = vd[r, pl.ds(a, len_c, stride=d), :].astype(BF16)
            for q0, k0, nk, var in blocks_c:
                part = attend(qs[q0:q0 + DIL_QBLK], ks[k0:k0 + nk], vs[k0:k0 + nk], var, nk)
                for j in range(3):
                    pc[j, r, pl.ds(q0 * d + a, DIL_QBLK, stride=d), :] = part[j]

    for r in range(d):
        part = _merge_partial(tuple(pd[j, r] for j in range(3)), tuple(pc[j, r] for j in range(3)))
        for j in range(3):
            pn[j, pl.ds(r, len_d, stride=d), :] = part[j]

    o, _, l = _merge_partial(tuple(p1[j] for j in range(3)), tuple(pn[j] for j in range(3)))
    o_ref[...] = (o / l).astype(o_ref.dtype)


def dilated_attention(p3, col0, nh):
    bsz, seq, _ = p3.shape
    plan, bias_np = _dil_tables(seq, nh)
    dils = [dil for dil, _, _ in plan]
    assert len(dils) == 3 and dils[0] == 1 and dils[2] == dils[1] * dils[1], dils
    d = dils[1]
    nvar = bias_np.shape[1]
    kern = functools.partial(_dil_kernel, plan=plan, scale=HEAD_DIM ** -0.5 * LOG2E)
    head_spec = lambda off: pl.BlockSpec((None, seq, HEAD_DIM), lambda h, b: (b, 0, col0 + off + h))
    natural = pltpu.VMEM((seq, HEAD_DIM), F32)
    by_residue = pltpu.VMEM((d, seq // d, HEAD_DIM), F32)
    partial_natural = pltpu.VMEM((3, seq, HEAD_DIM), F32)
    partial_by_residue = pltpu.VMEM((3, d, seq // d, HEAD_DIM), F32)
    return pl.pallas_call(
        kern,
        grid=(nh, bsz),
        in_specs=[head_spec(0), head_spec(nh), head_spec(2 * nh),
                  pl.BlockSpec((None, nvar, DIL_QBLK, 2 * DIL_QBLK), lambda h, b: (h, 0, 0, 0))],
        out_specs=pl.BlockSpec((None, seq, HEAD_DIM), lambda h, b: (b, 0, h)),
        out_shape=jax.ShapeDtypeStruct((bsz, seq, nh * HEAD_DIM), BF16),
        scratch_shapes=[natural] * 3 + [by_residue] * 3 + [partial_natural, partial_by_residue,
                                                            partial_by_residue, partial_natural],
        compiler_params=_params(("parallel", "parallel"), 40),
        name="dilated_attention",
    )(p3, p3, p3, jnp.asarray(bias_np * np.float32(LOG2E)))


def _out_proj_router_kernel(ya_ref, yb_ref, wa_ref, wb_ref, x_ref, g_ref, r_ref, xo_ref, h_ref, lg_ref):
    acc = jnp.dot(ya_ref[...], wa_ref[...], preferred_element_type=F32)
    acc = acc + jnp.dot(yb_ref[...], wb_ref[...], preferred_element_type=F32)
    ne = lg_ref.shape[1]
    for c in range(acc.shape[0] // EPILOGUE_ROWS):
        rows = slice(c * EPILOGUE_ROWS, (c + 1) * EPILOGUE_ROWS)
        xn = x_ref[rows, :] + acc[rows, :]
        xo_ref[rows, :] = xn
        ms = jnp.mean(xn * xn, axis=-1, keepdims=True)
        h = xn * lax.rsqrt(ms + RMS_EPS) * g_ref[...]
        h_hi = h.astype(BF16)
        h_ref[rows, :] = h_hi
        h_lo = (h - h_hi.astype(F32)).astype(BF16)
        s = (jnp.dot(h_hi, r_ref[...], preferred_element_type=F32)
             + jnp.dot(h_lo, r_ref[...], preferred_element_type=F32))
        lg_ref[rows, :] = s[:, :ne] + s[:, ne:]


def out_proj_norm_router(ya, yb, w_bf16, x2d, g, router, tm=512):
    t, d = x2d.shape
    ka = ya.shape[1]
    kb = yb.shape[1]
    assert ka == kb
    ne = router.shape[1]
    r_hi = router.astype(BF16)
    r_lo = (router - r_hi.astype(F32)).astype(BF16)
    r_split = jnp.concatenate([r_hi, r_lo], axis=1)
    return pl.pallas_call(
        _out_proj_router_kernel,
        grid=(t // tm,),
        in_specs=[pl.BlockSpec((tm, ka), lambda i: (i, 0)),
                  pl.BlockSpec((tm, kb), lambda i: (i, 0)),
                  pl.BlockSpec((ka, d), lambda i: (0, 0)),
                  pl.BlockSpec((kb, d), lambda i: (1, 0)),
                  pl.BlockSpec((tm, d), lambda i: (i, 0)),
                  pl.BlockSpec((1, d), lambda i: (0, 0)),
                  pl.BlockSpec((d, 2 * ne), lambda i: (0, 0))],
        out_specs=[pl.BlockSpec((tm, d), lambda i: (i, 0)),
                   pl.BlockSpec((tm, d), lambda i: (i, 0)),
                   pl.BlockSpec((tm, ne), lambda i: (i, 0))],
        out_shape=[jax.ShapeDtypeStruct((t, d), F32),
                   jax.ShapeDtypeStruct((t, d), BF16),
                   jax.ShapeDtypeStruct((t, ne), F32)],
        compiler_params=_params(("parallel",), 56),
        name="out_proj_norm_router",
    )(ya, yb, w_bf16, w_bf16, x2d, g.reshape(1, d), r_split)


def _route_kernel(lg_ref, slot_ref, aff_ref, *, cap):
    nb, ne, seq = aff_ref.shape
    logits = lg_ref[...].reshape(nb, ne, seq)
    mx = jnp.max(logits, axis=1, keepdims=True)
    ex = jnp.exp(logits - mx)
    a3 = ex / jnp.sum(ex, axis=1, keepdims=True)
    aff_ref[...] = a3
    a = a3.reshape(nb * ne, seq)
    nrows = nb * ne
    lanes = 128
    r = lax.broadcasted_iota(jnp.int32, (lanes, lanes), 0)
    c = lax.broadcasted_iota(jnp.int32, (lanes, lanes), 1)
    tri = jnp.where(r <= c, 1.0, 0.0).astype(BF16)

    def count(mask_f):
        return jnp.sum(mask_f, axis=1, keepdims=True)

    def prefix(x):
        parts = []
        run = jnp.zeros((nrows, 1), F32)
        for j in range(seq // lanes):
            blk = x[:, j * lanes:(j + 1) * lanes]
            inc = jnp.dot(blk.astype(BF16), tri, preferred_element_type=F32)
            parts.append(inc - blk + run)
            run = run + count(blk)
        return jnp.concatenate(parts, axis=1)

    def body(i, ans):
        cand = ans | jnp.left_shift(jnp.int32(1), 30 - i)
        cnt = count(jnp.where(a >= lax.bitcast_convert_type(cand, F32), 1.0, 0.0))
        return jnp.where(cnt >= cap, cand, ans)

    ans = lax.fori_loop(0, 31, body, jnp.zeros((nrows, 1), jnp.int32))
    thr = lax.bitcast_convert_type(ans, F32)
    gt = jnp.where(a > thr, 1.0, 0.0)
    eq = jnp.where(a >= thr, 1.0, 0.0) - gt
    need = cap - count(gt)
    sel = gt + eq * jnp.where(prefix(eq) < need, 1.0, 0.0)
    slot_ref[...] = jnp.where(sel > 0.5, prefix(sel), -1.0).astype(jnp.int32).reshape(nb, ne, seq)


def route(logits_rows, bsz, ne, seq, cap):
    return pl.pallas_call(
        functools.partial(_route_kernel, cap=cap),
        grid=(1,),
        in_specs=[pl.BlockSpec((bsz * ne, seq), lambda i: (0, 0))],
        out_specs=[pl.BlockSpec((bsz, ne, seq), lambda i: (0, 0, 0)),
                   pl.BlockSpec((bsz, ne, seq), lambda i: (0, 0, 0))],
        out_shape=[jax.ShapeDtypeStruct((bsz, ne, seq), jnp.int32),
                   jax.ShapeDtypeStruct((bsz, ne, seq), F32)],
        compiler_params=_params(("arbitrary",), 40),
        name="route",
    )(logits_rows)


def _gather_kernel(slot_ref, aff_ref, h_ref, xe_ref, gate_ref, *, cap, ncol):
    eg, seq = slot_ref.shape
    d = h_ref.shape[1]
    cidx = lax.broadcasted_iota(jnp.int32, (cap, seq), 0)
    rows = []
    for j in range(eg):
        hit = cidx == slot_ref[j:j + 1, :]
        rows.append(jnp.where(hit, 1.0, 0.0).astype(BF16))
        gate_ref[j] = jnp.sum(jnp.where(hit, aff_ref[j:j + 1, :], 0.0), axis=1, keepdims=True)
    p = jnp.concatenate(rows, axis=0)
    dc = d // ncol
    for c in range(ncol):
        xe = jnp.dot(p, h_ref[:, c * dc:(c + 1) * dc], preferred_element_type=F32)
        xe_ref[:, :, c * dc:(c + 1) * dc] = xe.reshape(eg, cap, dc).astype(xe_ref.dtype)


def gather_rows(slot, aff, h, cap, egroup=4, ncol=2):
    bsz, ne, seq = slot.shape
    d = h.shape[1]
    ngroups = ne // egroup
    grouped = lambda a: a.reshape(bsz, ngroups, egroup, seq)
    return pl.pallas_call(
        functools.partial(_gather_kernel, cap=cap, ncol=ncol),
        grid=(bsz, ngroups),
        in_specs=[pl.BlockSpec((None, None, egroup, seq), lambda b, g: (b, g, 0, 0)),
                  pl.BlockSpec((None, None, egroup, seq), lambda b, g: (b, g, 0, 0)),
                  pl.BlockSpec((seq, d), lambda b, g: (b, 0))],
        out_specs=[pl.BlockSpec((egroup, None, cap, d), lambda b, g: (g, b, 0, 0)),
                   pl.BlockSpec((egroup, None, cap, 1), lambda b, g: (g, b, 0, 0))],
        out_shape=[jax.ShapeDtypeStruct((ne, bsz, cap, d), BF16),
                   jax.ShapeDtypeStruct((ne, bsz, cap, 1), F32)],
        compiler_params=_params(("parallel", "parallel"), 48),
        name="moe_gather",
    )(grouped(slot), grouped(aff), h)


def _combine_kernel(slot_t_ref, ye_ref, x_ref, o_ref, *, cap):
    tb, ne = slot_t_ref.shape
    tn = x_ref.shape[1]
    cidx = lax.broadcasted_iota(jnp.int32, (tb, cap), 1)
    pt = jnp.concatenate([jnp.where(cidx == slot_t_ref[:, e:e + 1], 1.0, 0.0).astype(BF16) for e in range(ne)],
                         axis=1)
    ye = ye_ref[...].reshape(ne * cap, tn)
    o_ref[...] = x_ref[...] + jnp.dot(pt, ye, preferred_element_type=F32)


def combine(slot_t, ye, x2d, cap, tn=1024, tb=512):
    bsz, seq, ne = slot_t.shape
    d = x2d.shape[1]
    nt = seq // tb
    return pl.pallas_call(
        functools.partial(_combine_kernel, cap=cap),
        grid=(bsz, d // tn, nt),
        in_specs=[pl.BlockSpec((None, tb, ne), lambda b, n, t: (b, t, 0)),
                  pl.BlockSpec((ne, None, cap, tn), lambda b, n, t: (0, b, 0, n)),
                  pl.BlockSpec((tb, tn), lambda b, n, t: (b * nt + t, n))],
        out_specs=pl.BlockSpec((tb, tn), lambda b, n, t: (b * nt + t, n)),
        out_shape=jax.ShapeDtypeStruct(x2d.shape, F32),
        compiler_params=_params(("parallel", "parallel", "parallel"), 48),
        name="moe_combine",
    )(slot_t, ye, x2d)


def _expert_ffn_kernel(xe_ref, wg_ref, wu_ref, wd_ref, gate_ref, o_ref, acc_ref):
    f = pl.program_id(2)
    last = pl.num_programs(2) - 1
    rows_per = xe_ref.shape[0] // FFN_ROW_CHUNKS

    def step(first, final):
        wg = wg_ref[...].astype(BF16)
        wu = wu_ref[...].astype(BF16)
        wd = wd_ref[...].astype(BF16)
        for c in range(FFN_ROW_CHUNKS):
            rows = slice(c * rows_per, (c + 1) * rows_per)
            xe = xe_ref[rows, :]
            g = jnp.dot(xe, wg, preferred_element_type=F32)
            u = jnp.dot(xe, wu, preferred_element_type=F32)
            hid = (jax.nn.silu(g) * u).astype(BF16)
            part = jnp.dot(hid, wd, preferred_element_type=F32)
            total = part if first else acc_ref[rows, :] + part
            if final:
                o_ref[rows, :] = (total * gate_ref[rows, :]).astype(o_ref.dtype)
            else:
                acc_ref[rows, :] = total

    pl.when(f == 0)(lambda: step(True, False))
    pl.when(jnp.logical_and(f > 0, f < last))(lambda: step(False, False))
    pl.when(f == last)(lambda: step(False, True))


def expert_ffn(xe, w_gate, w_up, w_down, layer, gate, tm=1024, tf=256):
    ne, r, d = xe.shape
    fdim = w_gate.shape[3]
    assert fdim // tf >= 2, "the kernel distinguishes first / middle / last hidden tiles"
    return pl.pallas_call(
        _expert_ffn_kernel,
        grid=(ne, r // tm, fdim // tf),
        in_specs=[pl.BlockSpec((None, tm, d), lambda e, m, f: (e, m, 0)),
                  pl.BlockSpec((None, None, d, tf), lambda e, m, f: (layer, e, 0, f)),
                  pl.BlockSpec((None, None, d, tf), lambda e, m, f: (layer, e, 0, f)),
                  pl.BlockSpec((None, None, tf, d), lambda e, m, f: (layer, e, f, 0)),
                  pl.BlockSpec((None, tm, 1), lambda e, m, f: (e, m, 0))],
        out_specs=pl.BlockSpec((None, tm, d), lambda e, m, f: (e, m, 0)),
        out_shape=jax.ShapeDtypeStruct((ne, r, d), BF16),
        scratch_shapes=[pltpu.VMEM((tm, d), F32)],
        compiler_params=_params(("parallel", "parallel", "arbitrary"), 56),
        name="expert_ffn",
    )(xe, w_gate, w_up, w_down, gate)


def _final_norm_kernel(x_ref, g_ref, o_ref):
    x = x_ref[...]
    ms = jnp.mean(x * x, axis=-1, keepdims=True)
    o_ref[...] = x * lax.rsqrt(ms + RMS_EPS) * g_ref[...]


def final_norm(x2d, g, tm=512):
    t, d = x2d.shape
    return pl.pallas_call(
        _final_norm_kernel,
        grid=(t // tm,),
        in_specs=[pl.BlockSpec((tm, d), lambda i: (i, 0)), pl.BlockSpec((1, d), lambda i: (0, 0))],
        out_specs=pl.BlockSpec((tm, d), lambda i: (i, 0)),
        out_shape=jax.ShapeDtypeStruct((t, d), F32),
        compiler_params=_params(("parallel",), 32),
        name="final_norm",
    )(x2d, g.reshape(1, d))


def expert_choice_moe(x2d, h, logits, bsz, seq, layer, w_gate, w_up, w_down):
    d = x2d.shape[1]
    ne = logits.shape[1]
    cap = EC_CAPACITY_FACTOR * seq // ne
    logits_rows = logits.reshape(bsz, seq, ne).transpose(0, 2, 1).reshape(bsz * ne, seq)
    slot, aff = route(logits_rows, bsz, ne, seq, cap)
    xe, gate = gather_rows(slot, aff, h, cap)
    ye = expert_ffn(xe.reshape(ne, bsz * cap, d), w_gate, w_up, w_down, layer,
                    gate.reshape(ne, bsz * cap, 1))
    return combine(slot.transpose(0, 2, 1), ye.reshape(ne, bsz, cap, d), x2d, cap)


def kernel(x, norm_mix, norm_ffn, norm_final, w_in_ab, a_v_norm, a_spatial_w, a_spatial_b, b_rpb, w_out_ab,
           w_in_cd, c_conv, w_out_cd, router, w_gate, w_up, w_down):
    bsz, seq, d = x.shape
    depth = norm_mix.shape[0]
    half = d // 2
    nh = half // HEAD_DIM
    x2d = x.reshape(bsz * seq, d)
    for layer in range(depth):
        i = layer // 2
        if layer % 2 == 0:
            p = rms_matmul(x2d, norm_mix[layer], w_in_ab[i])
            p3 = p.reshape(bsz, seq, -1)
            ya = sgu(p, a_v_norm[i], a_spatial_w[i], a_spatial_b[i], half)
            yb = neighbourhood_attention(p3, b_rpb[i], 2 * half // HEAD_DIM, nh).reshape(bsz * seq, half)
            mix, w_out = (ya, yb), w_out_ab[i]
        else:
            p = rms_matmul(x2d, norm_mix[layer], w_in_cd[i])
            p3 = p.reshape(bsz, seq, -1)
            yc = gated_short_conv(p3, c_conv[i], half).reshape(bsz * seq, half)
            yd = dilated_attention(p3, 3 * half // HEAD_DIM, nh).reshape(bsz * seq, half)
            mix, w_out = (yc, yd), w_out_cd[i]
        x2d, h, logits = out_proj_norm_router(mix[0], mix[1], w_out.astype(BF16), x2d,
                                              norm_ffn[layer], router[layer])
        x2d = expert_choice_moe(x2d, h, logits, bsz, seq, layer, w_gate, w_up, w_down)
    return final_norm(x2d, norm_final).reshape(bsz, seq, d)
```

```python
import functools

import numpy as np
import jax
import jax.numpy as jnp
from jax import lax
from jax.experimental import pallas as pl
from jax.experimental.pallas import tpu as pltpu

F32 = jnp.float32
BF16 = jnp.bfloat16

HEAD_DIM = 128
CHUNK = 128
GRID_W = 64
NA_ROWS = 8
NA_COLS = 16
NA_QROWS = 4
NA_KROWS = NA_QROWS + NA_ROWS
CONV_W = 3
CONV_HALO = 16
DIL_PATTERNS = ((128, 1), (512, 4), (2048, 16))
DIL_QBLK = 128
IN_PROJ_NORM_CHUNKS = 4
EPILOGUE_ROWS = 128
FFN_ROW_CHUNKS = 2
N_EXPERTS = 16
EC_CAPACITY_FACTOR = 2
RMS_EPS = 1e-6
LN_EPS = 1e-5
NEG_INF = -1e30
LOG2E = 1.4426950408889634
MIB = 1024 * 1024


def _params(semantics, vmem_mib):
    return pltpu.CompilerParams(dimension_semantics=semantics,
                                vmem_limit_bytes=int(vmem_mib * MIB))


def _rms_matmul_kernel(x_ref, g_ref, w_ref, o_ref, h_scr):
    j = pl.program_id(1)

    def first_column_tile():
        w = w_ref[...].astype(BF16)
        rows_per = x_ref.shape[0] // IN_PROJ_NORM_CHUNKS
        for c in range(IN_PROJ_NORM_CHUNKS):
            rows = slice(c * rows_per, (c + 1) * rows_per)
            x = x_ref[rows, :]
            ms = jnp.mean(x * x, axis=-1, keepdims=True)
            h = (x * lax.rsqrt(ms + RMS_EPS) * g_ref[...]).astype(BF16)
            h_scr[rows, :] = h
            o_ref[rows, :] = jnp.dot(h, w, preferred_element_type=F32).astype(o_ref.dtype)

    def later_column_tile():
        o_ref[...] = jnp.dot(h_scr[...], w_ref[...].astype(BF16), preferred_element_type=F32).astype(o_ref.dtype)

    pl.when(j == 0)(first_column_tile)
    pl.when(j > 0)(later_column_tile)


def rms_matmul(x2d, g, w, tm=1024, tn=1024):
    t, d = x2d.shape
    n = w.shape[1]
    return pl.pallas_call(
        _rms_matmul_kernel,
        grid=(t // tm, n // tn),
        in_specs=[pl.BlockSpec((tm, d), lambda i, j: (i, 0)),
                  pl.BlockSpec((1, d), lambda i, j: (0, 0)),
                  pl.BlockSpec((d, tn), lambda i, j: (0, j))],
        out_specs=pl.BlockSpec((tm, tn), lambda i, j: (i, j)),
        out_shape=jax.ShapeDtypeStruct((t, n), BF16),
        scratch_shapes=[pltpu.VMEM((tm, d), BF16)],
        compiler_params=_params(("parallel", "arbitrary"), 56),
        name="rms_in_proj",
    )(x2d, g.reshape(1, d), w)


def _sgu_kernel(u_ref, v_ref, lng_ref, ws_ref, bias_ref, o_ref):
    tm, width = u_ref.shape
    groups = width // HEAD_DIM
    nchunks = tm // CHUNK
    vf = jax.nn.gelu(v_ref[...].astype(F32))
    mu = jnp.mean(vf, axis=-1, keepdims=True)
    dv = vf - mu
    var = jnp.mean(dv * dv, axis=-1, keepdims=True)
    vn = (dv * lax.rsqrt(var + LN_EPS) * lng_ref[...]).astype(BF16)
    for g in range(groups):
        cols = slice(g * HEAD_DIM, (g + 1) * HEAD_DIM)
        vg = jnp.concatenate([vn[n * CHUNK:(n + 1) * CHUNK, cols] for n in range(nchunks)], axis=1)
        mixed = jnp.dot(ws_ref[g], vg, preferred_element_type=F32)
        for n in range(nchunks):
            rows = slice(n * CHUNK, (n + 1) * CHUNK)
            u = jax.nn.gelu(u_ref[rows, cols].astype(F32))
            o_ref[rows, cols] = (u * (mixed[:, n * CHUNK:(n + 1) * CHUNK] + bias_ref[:, cols])).astype(o_ref.dtype)


def sgu(p2d, ln_g, w_s, b_s, width, tm=512):
    t = p2d.shape[0]
    groups = width // HEAD_DIM
    bias_full = jnp.repeat(b_s.T.astype(F32), HEAD_DIM, axis=1)
    return pl.pallas_call(
        _sgu_kernel,
        grid=(t // tm,),
        in_specs=[pl.BlockSpec((tm, width), lambda i: (i, 0)),
                  pl.BlockSpec((tm, width), lambda i: (i, 1)),
                  pl.BlockSpec((1, width), lambda i: (0, 0)),
                  pl.BlockSpec((groups, CHUNK, CHUNK), lambda i: (0, 0, 0)),
                  pl.BlockSpec((CHUNK, width), lambda i: (0, 0))],
        out_specs=pl.BlockSpec((tm, width), lambda i: (i, 0)),
        out_shape=jax.ShapeDtypeStruct((t, width), BF16),
        compiler_params=_params(("parallel",), 40),
        name="sgu",
    )(p2d, p2d, ln_g.reshape(1, width).astype(F32), w_s.astype(BF16), bias_full)


def _na_block_layout(rows):
    nblk = rows // NA_QROWS
    kh = min(NA_ROWS, rows)
    starts, variants, var_key = [], [], {}
    for qb in range(nblk):
        r0 = qb * NA_QROWS
        ws = int(np.clip(r0 - kh // 2, 0, rows - NA_KROWS))
        rs = np.clip(np.arange(r0, r0 + NA_QROWS) - kh // 2, 0, rows - kh)
        key = (tuple(rs - ws), r0 - ws)
        if key not in var_key:
            var_key[key] = len(var_key)
        starts.append(ws)
        variants.append(var_key[key])
    return starts, variants, list(var_key.keys()), kh


def _na_tiles(rpb, rows):
    nh = rpb.shape[0]
    _, _, keys, kh = _na_block_layout(rows)
    qc = np.arange(GRID_W)[:, None]
    kc = np.arange(GRID_W)[None, :]
    col_start = np.clip(qc - NA_COLS // 2, 0, GRID_W - NA_COLS)
    col_valid = (kc >= col_start) & (kc < col_start + NA_COLS)
    dc_idx = np.clip(kc - qc + NA_COLS - 1, 0, 2 * NA_COLS - 2)
    onehot = (dc_idx[None] == np.arange(2 * NA_COLS - 1)[:, None, None]).astype(np.float32)
    expanded = jnp.einsum('hrc,cqk->hrqk', rpb.astype(F32) * LOG2E, jnp.asarray(onehot),
                          precision=lax.Precision.HIGHEST)
    tiles = jnp.where(jnp.asarray(col_valid)[None, None], expanded, NEG_INF)
    tiles = jnp.concatenate([tiles, jnp.full((nh, 1, GRID_W, GRID_W), NEG_INF, F32)], axis=1)
    invalid = 2 * NA_ROWS - 1
    dr = np.full((len(keys), NA_QROWS, NA_KROWS), invalid, np.int32)
    for v, (rs_rel, r0_rel) in enumerate(keys):
        for i in range(NA_QROWS):
            for j in range(NA_KROWS):
                if rs_rel[i] <= j < rs_rel[i] + kh:
                    dr[v, i, j] = j - (r0_rel + i) + NA_ROWS - 1
    return tiles, dr


def _na_kernel(q_ref, k_ref, v_ref, tiles_ref, o_ref, bias_scr, *, starts, variants, tile_idx, scale):
    qblk = NA_QROWS * GRID_W
    kblk = NA_KROWS * GRID_W

    @pl.when(pl.program_id(1) == 0)
    def _():
        for var in range(tile_idx.shape[0]):
            for i in range(NA_QROWS):
                strip = jnp.concatenate([tiles_ref[int(tile_idx[var, i, j])] for j in range(NA_KROWS)], axis=1)
                bias_scr[var, i * GRID_W:(i + 1) * GRID_W, :] = strip

    for qb, (ws, var) in enumerate(zip(starts, variants)):
        q = (q_ref[qb * qblk:(qb + 1) * qblk, :].astype(F32) * scale).astype(BF16)
        k = k_ref[ws * GRID_W:ws * GRID_W + kblk, :]
        v = v_ref[ws * GRID_W:ws * GRID_W + kblk, :]
        s = lax.dot_general(q, k, (((1,), (1,)), ((), ())), preferred_element_type=F32)
        s = s + bias_scr[var]
        m = jnp.max(s, axis=-1, keepdims=True)
        e = jnp.exp2(s - m).astype(BF16)
        ol = jnp.dot(e, jnp.concatenate([v, jnp.ones_like(v)], axis=1), preferred_element_type=F32)
        o_ref[qb * qblk:(qb + 1) * qblk, :] = (ol[:, :HEAD_DIM] / ol[:, HEAD_DIM:]).astype(o_ref.dtype)


def neighbourhood_attention(p3, rpb, col0, nh):
    bsz, seq, _ = p3.shape
    rows = seq // GRID_W
    starts, variants, keys, _ = _na_block_layout(rows)
    tiles, tile_idx = _na_tiles(rpb, rows)
    nvar = len(keys)
    ntiles = tiles.shape[1]
    qblk, kblk = NA_QROWS * GRID_W, NA_KROWS * GRID_W
    kern = functools.partial(_na_kernel, starts=starts, variants=variants, tile_idx=tile_idx,
                             scale=HEAD_DIM ** -0.5 * LOG2E)
    head_spec = lambda off: pl.BlockSpec((None, seq, HEAD_DIM), lambda h, b: (b, 0, col0 + off + h))
    return pl.pallas_call(
        kern,
        grid=(nh, bsz),
        in_specs=[head_spec(0), head_spec(nh), head_spec(2 * nh),
                  pl.BlockSpec((None, ntiles, GRID_W, GRID_W), lambda h, b: (h, 0, 0, 0))],
        out_specs=pl.BlockSpec((None, seq, HEAD_DIM), lambda h, b: (b, 0, h)),
        out_shape=jax.ShapeDtypeStruct((bsz, seq, nh * HEAD_DIM), BF16),
        scratch_shapes=[pltpu.VMEM((nvar, qblk, kblk), F32)],
        compiler_params=_params(("parallel", "arbitrary"), 32),
        name="neighbourhood_attention",
    )(p3, p3, p3, tiles)


def _dil_windows(length):
    nk = min(2 * DIL_QBLK, length)
    out = []
    for i in range(length // DIL_QBLK):
        q0 = i * DIL_QBLK
        k0 = int(np.clip(q0 - DIL_QBLK // 2, 0, length - nk))
        out.append((q0, k0, nk))
    return out


def _dil_tables(seq, nh):
    slopes = np.array([2.0 ** (-8.0 * (h + 1) / nh) for h in range(nh)], dtype=np.float32)
    var_index, absd_tabs, dil_of = {}, [], []
    plan = []
    for window, dil in DIL_PATTERNS:
        radius = window // (2 * dil)
        length = seq // dil
        blocks = []
        for q0, k0, nk in _dil_windows(length):
            key = (dil, radius, k0 - q0, nk)
            if key not in var_index:
                var_index[key] = len(absd_tabs)
                delta = (k0 + np.arange(2 * DIL_QBLK)[None, :]) - (q0 + np.arange(DIL_QBLK)[:, None])
                absd = np.abs(delta).astype(np.float32)
                valid = (np.abs(delta) <= radius) & (np.arange(2 * DIL_QBLK)[None, :] < nk)
                absd_tabs.append((absd * dil, valid))
            blocks.append((q0, k0, nk, var_index[key]))
        plan.append((dil, length, blocks))
    pen = np.stack([a for a, _ in absd_tabs])
    valid = np.stack([v for _, v in absd_tabs])
    bias = np.where(valid[None], -(slopes[:, None, None, None] * pen[None]).astype(np.float32), np.float32(NEG_INF))
    return plan, bias.astype(np.float32)


def _merge_partial(a, b):
    (oa, ma, la), (ob, mb, lb) = a, b
    a_is_max = ma >= mb
    t = jnp.exp2(jnp.minimum(ma, mb) - jnp.maximum(ma, mb))
    ta = jnp.where(a_is_max, 1.0, t)
    tb = jnp.where(a_is_max, t, 1.0)
    return ta * oa + tb * ob, jnp.maximum(ma, mb), ta * la + tb * lb


def _dil_kernel(q_ref, k_ref, v_ref, bias_ref, o_ref, qf, kf, vf, qd, kd, vd, p1, pd, pc, pn, *, plan, scale):
    (_, _, blocks1), (d, len_d, blocks_d), (_, len_c, blocks_c) = plan
    qf[...] = q_ref[...].astype(F32) * scale
    kf[...] = k_ref[...].astype(F32)
    vf[...] = v_ref[...].astype(F32)

    def attend(q, k, v, var, nk):
        s = lax.dot_general(q, k, (((1,), (1,)), ((), ())), preferred_element_type=F32)
        s = s + bias_ref[var][:, :nk]
        m = jnp.max(s, axis=-1, keepdims=True)
        e = jnp.exp2(s - m).astype(BF16)
        ol = jnp.dot(e, jnp.concatenate([v, jnp.ones_like(v)], axis=1), preferred_element_type=F32)
        o = ol[:, :HEAD_DIM]
        return o, jnp.broadcast_to(m, o.shape), ol[:, HEAD_DIM:]

    for q0, k0, nk, var in blocks1:
        part = attend(qf[q0:q0 + DIL_QBLK, :].astype(BF16), kf[k0:k0 + nk, :].astype(BF16),
                      vf[k0:k0 + nk, :].astype(BF16), var, nk)
        for j in range(3):
            p1[j, q0:q0 + DIL_QBLK, :] = part[j]

    for r in range(d):
        qd[r] = qf[pl.ds(r, len_d, stride=d), :]
        kd[r] = kf[pl.ds(r, len_d, stride=d), :]
        vd[r] = vf[pl.ds(r, len_d, stride=d), :]

    for r in range(d):
        qs, ks, vs = qd[r].astype(BF16), kd[r].astype(BF16), vd[r].astype(BF16)
        for q0, k0, nk, var in blocks_d:
            part = attend(qs[q0:q0 + DIL_QBLK], ks[k0:k0 + nk], vs[k0:k0 + nk], var, nk)
            for j in range(3):
                pd[j, r, q0:q0 + DIL_QBLK, :] = part[j]
        for a in range(d):
            qs = qd[r, pl.ds(a, len_c, stride=d), :].astype(BF16)
            ks = kd[r, pl.ds(a, len_c, stride=d), :].astype(BF16)
            vs = vd[r, pl.ds(a, len_c, stride=d), :].astype(BF16)
            for q0, k0, nk, var in blocks_c:
                part = attend(qs[q0:q0 + DIL_QBLK], ks[k0:k0 + nk], vs[k0:k0 + nk], var, nk)
                for j in range(3):
                    pc[j, r, pl.ds(q0 * d + a, DIL_QBLK, stride=d), :] = part[j]

    for r in range(d):
        part = _merge_partial(tuple(pd[j, r] for j in range(3)), tuple(pc[j, r] for j in range(3)))
        for j in range(3):
            pn[j, pl.ds(r, len_d, stride=d), :] = part[j]

    o, _, l = _merge_partial(tuple(p1[j] for j in range(3)), tuple(pn[j] for j in range(3)))
    o_ref[...] = (o / l).astype(o_ref.dtype)


def dilated_attention(p3, col0, nh):
    bsz, seq, _ = p3.shape
    plan, bias_np = _dil_tables(seq, nh)
    dils = [dil for dil, _, _ in plan]
    assert len(dils) == 3 and dils[0] == 1 and dils[2] == dils[1] * dils[1], dils
    d = dils[1]
    nvar = bias_np.shape[1]
    kern = functools.partial(_dil_kernel, plan=plan, scale=HEAD_DIM ** -0.5 * LOG2E)
    head_spec = lambda off: pl.BlockSpec((None, seq, HEAD_DIM), lambda h, b: (b, 0, col0 + off + h))
    natural = pltpu.VMEM((seq, HEAD_DIM), F32)
    by_residue = pltpu.VMEM((d, seq // d, HEAD_DIM), F32)
    partial_natural = pltpu.VMEM((3, seq, HEAD_DIM), F32)
    partial_by_residue = pltpu.VMEM((3, d, seq // d, HEAD_DIM), F32)
    return pl.pallas_call(
        kern,
        grid=(nh, bsz),
        in_specs=[head_spec(0), head_spec(nh), head_spec(2 * nh),
                  pl.BlockSpec((None, nvar, DIL_QBLK, 2 * DIL_QBLK), lambda h, b: (h, 0, 0, 0))],
        out_specs=pl.BlockSpec((None, seq, HEAD_DIM), lambda h, b: (b, 0, h)),
        out_shape=jax.ShapeDtypeStruct((bsz, seq, nh * HEAD_DIM), BF16),
        scratch_shapes=[natural] * 3 + [by_residue] * 3 + [partial_natural, partial_by_residue,
                                                            partial_by_residue, partial_natural],
        compiler_params=_params(("parallel", "parallel"), 40),
        name="dilated_attention",
    )(p3, p3, p3, jnp.asarray(bias_np * np.float32(LOG2E)))


def _residual_norm_router(acc, x_ref, g_ref, r_ref, xo_ref, h_ref, lg_ref):
    ne = lg_ref.shape[1]
    for c in range(acc.shape[0] // EPILOGUE_ROWS):
        rows = slice(c * EPILOGUE_ROWS, (c + 1) * EPILOGUE_ROWS)
        xn = x_ref[rows, :] + acc[rows, :]
        xo_ref[rows, :] = xn
        ms = jnp.mean(xn * xn, axis=-1, keepdims=True)
        h = xn * lax.rsqrt(ms + RMS_EPS) * g_ref[...]
        h_hi = h.astype(BF16)
        h_ref[rows, :] = h_hi
        h_lo = (h - h_hi.astype(F32)).astype(BF16)
        s = (jnp.dot(h_hi, r_ref[...], preferred_element_type=F32)
             + jnp.dot(h_lo, r_ref[...], preferred_element_type=F32))
        lg_ref[rows, :] = s[:, :ne] + s[:, ne:]


def _out_proj_router_kernel(ya_ref, yb_ref, wa_ref, wb_ref, x_ref, g_ref, r_ref, xo_ref, h_ref, lg_ref):
    acc = jnp.dot(ya_ref[...], wa_ref[...], preferred_element_type=F32)
    acc = acc + jnp.dot(yb_ref[...], wb_ref[...], preferred_element_type=F32)
    _residual_norm_router(acc, x_ref, g_ref, r_ref, xo_ref, h_ref, lg_ref)


def _conv_out_proj_router_kernel(b_ref, c_ref, xin_ref, cp_ref, xp_ref, cn_ref, xn_ref, taps_ref, yb_ref,
                                 wa_ref, wb_ref, x_ref, g_ref, r_ref, xo_ref, h_ref, lg_ref, *, tiles_per_seq):
    tm = c_ref.shape[0]
    pos = lax.rem(pl.program_id(0), tiles_per_seq)
    acc = jnp.dot(yb_ref[...], wb_ref[...], preferred_element_type=F32)
    z = c_ref[...].astype(F32) * xin_ref[...].astype(F32)
    z_before = (cp_ref[...].astype(F32) * xp_ref[...].astype(F32))[CONV_HALO - 1:CONV_HALO, :]
    z_after = (cn_ref[...].astype(F32) * xn_ref[...].astype(F32))[0:1, :]
    z_before = jnp.where(pos == 0, 0.0, z_before)
    z_after = jnp.where(pos == tiles_per_seq - 1, 0.0, z_after)
    row = lax.broadcasted_iota(jnp.int32, z.shape, 0)
    z_prev = jnp.where(row == 0, z_before, pltpu.roll(z, 1, axis=0))
    z_next = jnp.where(row == tm - 1, z_after, pltpu.roll(z, tm - 1, axis=0))
    y = taps_ref[0:1, :] * z_prev + taps_ref[1:2, :] * z + taps_ref[2:3, :] * z_next
    ya = (b_ref[...].astype(F32) * y).astype(BF16)
    acc = acc + jnp.dot(ya, wa_ref[...], preferred_element_type=F32)
    _residual_norm_router(acc, x_ref, g_ref, r_ref, xo_ref, h_ref, lg_ref)


def _router_split(router):
    r_hi = router.astype(BF16)
    r_lo = (router - r_hi.astype(F32)).astype(BF16)
    return jnp.concatenate([r_hi, r_lo], axis=1)


def conv_out_proj_norm_router(p2d, taps, yb, w_bf16, x2d, g, router, seq, tm=512):
    t, d = x2d.shape
    width = yb.shape[1]
    ne = router.shape[1]
    assert seq % tm == 0 and tm % CONV_HALO == 0
    halo_per_tile = tm // CONV_HALO
    last_halo = t // CONV_HALO - 1
    tile = lambda col: pl.BlockSpec((tm, width), lambda i: (i, col))
    before = lambda col: pl.BlockSpec((CONV_HALO, width), lambda i: (jnp.maximum(i * halo_per_tile - 1, 0), col))
    after = lambda col: pl.BlockSpec((CONV_HALO, width),
                                     lambda i: (jnp.minimum((i + 1) * halo_per_tile, last_halo), col))
    return pl.pallas_call(
        functools.partial(_conv_out_proj_router_kernel, tiles_per_seq=seq // tm),
        grid=(t // tm,),
        in_specs=[tile(0), tile(1), tile(2), before(1), before(2), after(1), after(2),
                  pl.BlockSpec((CONV_W, width), lambda i: (0, 0)),
                  pl.BlockSpec((tm, width), lambda i: (i, 0)),
                  pl.BlockSpec((width, d), lambda i: (0, 0)),
                  pl.BlockSpec((width, d), lambda i: (1, 0)),
                  pl.BlockSpec((tm, d), lambda i: (i, 0)),
                  pl.BlockSpec((1, d), lambda i: (0, 0)),
                  pl.BlockSpec((d, 2 * ne), lambda i: (0, 0))],
        out_specs=[pl.BlockSpec((tm, d), lambda i: (i, 0)),
                   pl.BlockSpec((tm, d), lambda i: (i, 0)),
                   pl.BlockSpec((tm, ne), lambda i: (i, 0))],
        out_shape=[jax.ShapeDtypeStruct((t, d), F32),
                   jax.ShapeDtypeStruct((t, d), BF16),
                   jax.ShapeDtypeStruct((t, ne), F32)],
        compiler_params=_params(("parallel",), 56),
        name="conv_out_proj_norm_router",
    )(p2d, p2d, p2d, p2d, p2d, p2d, p2d, taps.astype(F32), yb, w_bf16, w_bf16, x2d, g.reshape(1, d),
      _router_split(router))


def out_proj_norm_router(ya, yb, w_bf16, x2d, g, router, tm=512):
    t, d = x2d.shape
    ka = ya.shape[1]
    kb = yb.shape[1]
    assert ka == kb
    ne = router.shape[1]
    r_split = _router_split(router)
    return pl.pallas_call(
        _out_proj_router_kernel,
        grid=(t // tm,),
        in_specs=[pl.BlockSpec((tm, ka), lambda i: (i, 0)),
                  pl.BlockSpec((tm, kb), lambda i: (i, 0)),
                  pl.BlockSpec((ka, d), lambda i: (0, 0)),
                  pl.BlockSpec((kb, d), lambda i: (1, 0)),
                  pl.BlockSpec((tm, d), lambda i: (i, 0)),
                  pl.BlockSpec((1, d), lambda i: (0, 0)),
                  pl.BlockSpec((d, 2 * ne), lambda i: (0, 0))],
        out_specs=[pl.BlockSpec((tm, d), lambda i: (i, 0)),
                   pl.BlockSpec((tm, d), lambda i: (i, 0)),
                   pl.BlockSpec((tm, ne), lambda i: (i, 0))],
        out_shape=[jax.ShapeDtypeStruct((t, d), F32),
                   jax.ShapeDtypeStruct((t, d), BF16),
                   jax.ShapeDtypeStruct((t, ne), F32)],
        compiler_params=_params(("parallel",), 56),
        name="out_proj_norm_router",
    )(ya, yb, w_bf16, w_bf16, x2d, g.reshape(1, d), r_split)


def _route_kernel(lg_ref, slot_ref, aff_ref, *, cap):
    nb, ne, seq = aff_ref.shape
    logits = lg_ref[...].reshape(nb, ne, seq)
    mx = jnp.max(logits, axis=1, keepdims=True)
    ex = jnp.exp(logits - mx)
    a3 = ex / jnp.sum(ex, axis=1, keepdims=True)
    aff_ref[...] = a3
    a = a3.reshape(nb * ne, seq)
    nrows = nb * ne
    lanes = 128
    r = lax.broadcasted_iota(jnp.int32, (lanes, lanes), 0)
    c = lax.broadcasted_iota(jnp.int32, (lanes, lanes), 1)
    tri = jnp.where(r <= c, 1.0, 0.0).astype(BF16)

    def count(mask_f):
        return jnp.sum(mask_f, axis=1, keepdims=True)

    def prefix(x):
        parts = []
        run = jnp.zeros((nrows, 1), F32)
        for j in range(seq // lanes):
            blk = x[:, j * lanes:(j + 1) * lanes]
            inc = jnp.dot(blk.astype(BF16), tri, preferred_element_type=F32)
            parts.append(inc - blk + run)
            run = run + count(blk)
        return jnp.concatenate(parts, axis=1)

    def body(i, ans):
        cand = ans | jnp.left_shift(jnp.int32(1), 30 - i)
        cnt = count(jnp.where(a >= lax.bitcast_convert_type(cand, F32), 1.0, 0.0))
        return jnp.where(cnt >= cap, cand, ans)

    ans = lax.fori_loop(0, 31, body, jnp.zeros((nrows, 1), jnp.int32))
    thr = lax.bitcast_convert_type(ans, F32)
    gt = jnp.where(a > thr, 1.0, 0.0)
    eq = jnp.where(a >= thr, 1.0, 0.0) - gt
    need = cap - count(gt)
    sel = gt + eq * jnp.where(prefix(eq) < need, 1.0, 0.0)
    slot_ref[...] = jnp.where(sel > 0.5, prefix(sel), -1.0).astype(jnp.int32).reshape(nb, ne, seq)


def route(logits_rows, bsz, ne, seq, cap):
    return pl.pallas_call(
        functools.partial(_route_kernel, cap=cap),
        grid=(1,),
        in_specs=[pl.BlockSpec((bsz * ne, seq), lambda i: (0, 0))],
        out_specs=[pl.BlockSpec((bsz, ne, seq), lambda i: (0, 0, 0)),
                   pl.BlockSpec((bsz, ne, seq), lambda i: (0, 0, 0))],
        out_shape=[jax.ShapeDtypeStruct((bsz, ne, seq), jnp.int32),
                   jax.ShapeDtypeStruct((bsz, ne, seq), F32)],
        compiler_params=_params(("arbitrary",), 40),
        name="route",
    )(logits_rows)


def _gather_kernel(slot_ref, aff_ref, h_ref, xe_ref, gate_ref, *, cap, ncol):
    eg, seq = slot_ref.shape
    d = h_ref.shape[1]
    cidx = lax.broadcasted_iota(jnp.int32, (cap, seq), 0)
    rows = []
    for j in range(eg):
        hit = cidx == slot_ref[j:j + 1, :]
        rows.append(jnp.where(hit, 1.0, 0.0).astype(BF16))
        gate_ref[j] = jnp.sum(jnp.where(hit, aff_ref[j:j + 1, :], 0.0), axis=1, keepdims=True)
    p = jnp.concatenate(rows, axis=0)
    dc = d // ncol
    for c in range(ncol):
        xe = jnp.dot(p, h_ref[:, c * dc:(c + 1) * dc], preferred_element_type=F32)
        xe_ref[:, :, c * dc:(c + 1) * dc] = xe.reshape(eg, cap, dc).astype(xe_ref.dtype)


def gather_rows(slot, aff, h, cap, egroup=4, ncol=2):
    bsz, ne, seq = slot.shape
    d = h.shape[1]
    ngroups = ne // egroup
    grouped = lambda a: a.reshape(bsz, ngroups, egroup, seq)
    return pl.pallas_call(
        functools.partial(_gather_kernel, cap=cap, ncol=ncol),
        grid=(bsz, ngroups),
        in_specs=[pl.BlockSpec((None, None, egroup, seq), lambda b, g: (b, g, 0, 0)),
                  pl.BlockSpec((None, None, egroup, seq), lambda b, g: (b, g, 0, 0)),
                  pl.BlockSpec((seq, d), lambda b, g: (b, 0))],
        out_specs=[pl.BlockSpec((egroup, None, cap, d), lambda b, g: (g, b, 0, 0)),
                   pl.BlockSpec((egroup, None, cap, 1), lambda b, g: (g, b, 0, 0))],
        out_shape=[jax.ShapeDtypeStruct((ne, bsz, cap, d), BF16),
                   jax.ShapeDtypeStruct((ne, bsz, cap, 1), F32)],
        compiler_params=_params(("parallel", "parallel"), 48),
        name="moe_gather",
    )(grouped(slot), grouped(aff), h)


def _combine_kernel(slot_t_ref, ye_ref, x_ref, o_ref, *, cap):
    tb, ne = slot_t_ref.shape
    tn = x_ref.shape[1]
    cidx = lax.broadcasted_iota(jnp.int32, (tb, cap), 1)
    pt = jnp.concatenate([jnp.where(cidx == slot_t_ref[:, e:e + 1], 1.0, 0.0).astype(BF16) for e in range(ne)],
                         axis=1)
    ye = ye_ref[...].reshape(ne * cap, tn)
    o_ref[...] = x_ref[...] + jnp.dot(pt, ye, preferred_element_type=F32)


def combine(slot_t, ye, x2d, cap, tn=1024, tb=512):
    bsz, seq, ne = slot_t.shape
    d = x2d.shape[1]
    nt = seq // tb
    return pl.pallas_call(
        functools.partial(_combine_kernel, cap=cap),
        grid=(bsz, d // tn, nt),
        in_specs=[pl.BlockSpec((None, tb, ne), lambda b, n, t: (b, t, 0)),
                  pl.BlockSpec((ne, None, cap, tn), lambda b, n, t: (0, b, 0, n)),
                  pl.BlockSpec((tb, tn), lambda b, n, t: (b * nt + t, n))],
        out_specs=pl.BlockSpec((tb, tn), lambda b, n, t: (b * nt + t, n)),
        out_shape=jax.ShapeDtypeStruct(x2d.shape, F32),
        compiler_params=_params(("parallel", "parallel", "parallel"), 48),
        name="moe_combine",
    )(slot_t, ye, x2d)


def _expert_ffn_kernel(xe_ref, wg_ref, wu_ref, wd_ref, gate_ref, o_ref, acc_ref):
    f = pl.program_id(2)
    last = pl.num_programs(2) - 1
    rows_per = xe_ref.shape[0] // FFN_ROW_CHUNKS

    def step(first, final):
        wg = wg_ref[...].astype(BF16)
        wu = wu_ref[...].astype(BF16)
        wd = wd_ref[...].astype(BF16)
        for c in range(FFN_ROW_CHUNKS):
            rows = slice(c * rows_per, (c + 1) * rows_per)
            xe = xe_ref[rows, :]
            g = jnp.dot(xe, wg, preferred_element_type=F32)
            u = jnp.dot(xe, wu, preferred_element_type=F32)
            hid = (jax.nn.silu(g) * u).astype(BF16)
            part = jnp.dot(hid, wd, preferred_element_type=F32)
            total = part if first else acc_ref[rows, :] + part
            if final:
                o_ref[rows, :] = (total * gate_ref[rows, :]).astype(o_ref.dtype)
            else:
                acc_ref[rows, :] = total

    pl.when(f == 0)(lambda: step(True, False))
    pl.when(jnp.logical_and(f > 0, f < last))(lambda: step(False, False))
    pl.when(f == last)(lambda: step(False, True))


def expert_ffn(xe, w_gate, w_up, w_down, layer, gate, tm=1024, tf=256):
    ne, r, d = xe.shape
    fdim = w_gate.shape[3]
    assert fdim // tf >= 2, "the kernel distinguishes first / middle / last hidden tiles"
    return pl.pallas_call(
        _expert_ffn_kernel,
        grid=(ne, r // tm, fdim // tf),
        in_specs=[pl.BlockSpec((None, tm, d), lambda e, m, f: (e, m, 0)),
                  pl.BlockSpec((None, None, d, tf), lambda e, m, f: (layer, e, 0, f)),
                  pl.BlockSpec((None, None, d, tf), lambda e, m, f: (layer, e, 0, f)),
                  pl.BlockSpec((None, None, tf, d), lambda e, m, f: (layer, e, f, 0)),
                  pl.BlockSpec((None, tm, 1), lambda e, m, f: (e, m, 0))],
        out_specs=pl.BlockSpec((None, tm, d), lambda e, m, f: (e, m, 0)),
        out_shape=jax.ShapeDtypeStruct((ne, r, d), BF16),
        scratch_shapes=[pltpu.VMEM((tm, d), F32)],
        compiler_params=_params(("parallel", "parallel", "arbitrary"), 56),
        name="expert_ffn",
    )(xe, w_gate, w_up, w_down, gate)


def _final_norm_kernel(x_ref, g_ref, o_ref):
    x = x_ref[...]
    ms = jnp.mean(x * x, axis=-1, keepdims=True)
    o_ref[...] = x * lax.rsqrt(ms + RMS_EPS) * g_ref[...]


def final_norm(x2d, g, tm=512):
    t, d = x2d.shape
    return pl.pallas_call(
        _final_norm_kernel,
        grid=(t // tm,),
        in_specs=[pl.BlockSpec((tm, d), lambda i: (i, 0)), pl.BlockSpec((1, d), lambda i: (0, 0))],
        out_specs=pl.BlockSpec((tm, d), lambda i: (i, 0)),
        out_shape=jax.ShapeDtypeStruct((t, d), F32),
        compiler_params=_params(("parallel",), 32),
        name="final_norm",
    )(x2d, g.reshape(1, d))


def expert_choice_moe(x2d, h, logits, bsz, seq, layer, w_gate, w_up, w_down):
    d = x2d.shape[1]
    ne = logits.shape[1]
    cap = EC_CAPACITY_FACTOR * seq // ne
    logits_rows = logits.reshape(bsz, seq, ne).transpose(0, 2, 1).reshape(bsz * ne, seq)
    slot, aff = route(logits_rows, bsz, ne, seq, cap)
    xe, gate = gather_rows(slot, aff, h, cap)
    ye = expert_ffn(xe.reshape(ne, bsz * cap, d), w_gate, w_up, w_down, layer,
                    gate.reshape(ne, bsz * cap, 1))
    return combine(slot.transpose(0, 2, 1), ye.reshape(ne, bsz, cap, d), x2d, cap)


def kernel(x, norm_mix, norm_ffn, norm_final, w_in_ab, a_v_norm, a_spatial_w, a_spatial_b, b_rpb, w_out_ab,
           w_in_cd, c_conv, w_out_cd, router, w_gate, w_up, w_down):
    bsz, seq, d = x.shape
    depth = norm_mix.shape[0]
    half = d // 2
    nh = half // HEAD_DIM
    x2d = x.reshape(bsz * seq, d)
    for layer in range(depth):
        i = layer // 2
        if layer % 2 == 0:
            p = rms_matmul(x2d, norm_mix[layer], w_in_ab[i])
            p3 = p.reshape(bsz, seq, -1)
            ya = sgu(p, a_v_norm[i], a_spatial_w[i], a_spatial_b[i], half)
            yb = neighbourhood_attention(p3, b_rpb[i], 2 * half // HEAD_DIM, nh).reshape(bsz * seq, half)
            x2d, h, logits = out_proj_norm_router(ya, yb, w_out_ab[i].astype(BF16), x2d,
                                                  norm_ffn[layer], router[layer])
        else:
            p = rms_matmul(x2d, norm_mix[layer], w_in_cd[i])
            p3 = p.reshape(bsz, seq, -1)
            yd = dilated_attention(p3, 3 * half // HEAD_DIM, nh).reshape(bsz * seq, half)
            x2d, h, logits = conv_out_proj_norm_router(p, c_conv[i], yd, w_out_cd[i].astype(BF16), x2d,
                                                       norm_ffn[layer], router[layer], seq)
        x2d = expert_choice_moe(x2d, h, logits, bsz, seq, layer, w_gate, w_up, w_down)
    return final_norm(x2d, norm_final).reshape(bsz, seq, d)
```

```python
import functools

import numpy as np
import jax
import jax.numpy as jnp
from jax import lax
from jax.experimental import pallas as pl
from jax.experimental.pallas import tpu as pltpu

F32 = jnp.float32
BF16 = jnp.bfloat16

HEAD_DIM = 128
CHUNK = 128
GRID_W = 64
NA_ROWS = 8
NA_COLS = 16
NA_QROWS = 4
NA_KROWS = NA_QROWS + NA_ROWS
CONV_W = 3
CONV_HALO = 16
DIL_PATTERNS = ((128, 1), (512, 4), (2048, 16))
DIL_QBLK = 128
IN_PROJ_NORM_CHUNKS = 4
EPILOGUE_ROWS = 128
FFN_ROW_CHUNKS = 2
EC_CAPACITY_FACTOR = 2
RMS_EPS = 1e-6
LN_EPS = 1e-5
NEG_INF = -1e30
LOG2E = 1.4426950408889634
MIB = 1024 * 1024


def _params(semantics, vmem_mib):
    return pltpu.CompilerParams(dimension_semantics=semantics,
                                vmem_limit_bytes=int(vmem_mib * MIB))


def _rms_matmul_kernel(x_ref, g_ref, w_ref, o_ref, h_scr):
    j = pl.program_id(1)

    def first_column_tile():
        w = w_ref[...].astype(BF16)
        rows_per = x_ref.shape[0] // IN_PROJ_NORM_CHUNKS
        for c in range(IN_PROJ_NORM_CHUNKS):
            rows = slice(c * rows_per, (c + 1) * rows_per)
            x = x_ref[rows, :]
            ms = jnp.mean(x * x, axis=-1, keepdims=True)
            h = (x * lax.rsqrt(ms + RMS_EPS) * g_ref[...]).astype(BF16)
            h_scr[rows, :] = h
            o_ref[rows, :] = jnp.dot(h, w, preferred_element_type=F32).astype(o_ref.dtype)

    def later_column_tile():
        o_ref[...] = jnp.dot(h_scr[...], w_ref[...].astype(BF16), preferred_element_type=F32).astype(o_ref.dtype)

    pl.when(j == 0)(first_column_tile)
    pl.when(j > 0)(later_column_tile)


def rms_matmul(x2d, g, w, tm=1024, tn=1024):
    t, d = x2d.shape
    n = w.shape[1]
    return pl.pallas_call(
        _rms_matmul_kernel,
        grid=(t // tm, n // tn),
        in_specs=[pl.BlockSpec((tm, d), lambda i, j: (i, 0)),
                  pl.BlockSpec((1, d), lambda i, j: (0, 0)),
                  pl.BlockSpec((d, tn), lambda i, j: (0, j))],
        out_specs=pl.BlockSpec((tm, tn), lambda i, j: (i, j)),
        out_shape=jax.ShapeDtypeStruct((t, n), BF16),
        scratch_shapes=[pltpu.VMEM((tm, d), BF16)],
        compiler_params=_params(("parallel", "arbitrary"), 56),
        name="rms_in_proj",
    )(x2d, g.reshape(1, d), w)


def _sgu_kernel(u_ref, v_ref, lng_ref, ws_ref, bias_ref, o_ref):
    tm, width = u_ref.shape
    groups = width // HEAD_DIM
    nchunks = tm // CHUNK
    vf = jax.nn.gelu(v_ref[...].astype(F32))
    mu = jnp.mean(vf, axis=-1, keepdims=True)
    dv = vf - mu
    var = jnp.mean(dv * dv, axis=-1, keepdims=True)
    vn = (dv * lax.rsqrt(var + LN_EPS) * lng_ref[...]).astype(BF16)
    for g in range(groups):
        cols = slice(g * HEAD_DIM, (g + 1) * HEAD_DIM)
        vg = jnp.concatenate([vn[n * CHUNK:(n + 1) * CHUNK, cols] for n in range(nchunks)], axis=1)
        mixed = jnp.dot(ws_ref[g], vg, preferred_element_type=F32)
        for n in range(nchunks):
            rows = slice(n * CHUNK, (n + 1) * CHUNK)
            u = jax.nn.gelu(u_ref[rows, cols].astype(F32))
            o_ref[rows, cols] = (u * (mixed[:, n * CHUNK:(n + 1) * CHUNK] + bias_ref[:, cols])).astype(o_ref.dtype)


def sgu(p2d, ln_g, w_s, b_s, width, tm=512):
    t = p2d.shape[0]
    groups = width // HEAD_DIM
    bias_full = jnp.repeat(b_s.T.astype(F32), HEAD_DIM, axis=1)
    return pl.pallas_call(
        _sgu_kernel,
        grid=(t // tm,),
        in_specs=[pl.BlockSpec((tm, width), lambda i: (i, 0)),
                  pl.BlockSpec((tm, width), lambda i: (i, 1)),
                  pl.BlockSpec((1, width), lambda i: (0, 0)),
                  pl.BlockSpec((groups, CHUNK, CHUNK), lambda i: (0, 0, 0)),
                  pl.BlockSpec((CHUNK, width), lambda i: (0, 0))],
        out_specs=pl.BlockSpec((tm, width), lambda i: (i, 0)),
        out_shape=jax.ShapeDtypeStruct((t, width), BF16),
        compiler_params=_params(("parallel",), 40),
        name="sgu",
    )(p2d, p2d, ln_g.reshape(1, width).astype(F32), w_s.astype(BF16), bias_full)


def _na_block_layout(rows):
    nblk = rows // NA_QROWS
    kh = min(NA_ROWS, rows)
    starts, variants, var_key = [], [], {}
    for qb in range(nblk):
        r0 = qb * NA_QROWS
        ws = int(np.clip(r0 - kh // 2, 0, rows - NA_KROWS))
        rs = np.clip(np.arange(r0, r0 + NA_QROWS) - kh // 2, 0, rows - kh)
        key = (tuple(rs - ws), r0 - ws)
        if key not in var_key:
            var_key[key] = len(var_key)
        starts.append(ws)
        variants.append(var_key[key])
    return starts, variants, list(var_key.keys()), kh


def _na_tiles(rpb, rows):
    nh = rpb.shape[0]
    _, _, keys, kh = _na_block_layout(rows)
    qc = np.arange(GRID_W)[:, None]
    kc = np.arange(GRID_W)[None, :]
    col_start = np.clip(qc - NA_COLS // 2, 0, GRID_W - NA_COLS)
    col_valid = (kc >= col_start) & (kc < col_start + NA_COLS)
    dc_idx = np.clip(kc - qc + NA_COLS - 1, 0, 2 * NA_COLS - 2)
    onehot = (dc_idx[None] == np.arange(2 * NA_COLS - 1)[:, None, None]).astype(np.float32)
    expanded = jnp.einsum('hrc,cqk->hrqk', rpb.astype(F32) * LOG2E, jnp.asarray(onehot),
                          precision=lax.Precision.HIGHEST)
    tiles = jnp.where(jnp.asarray(col_valid)[None, None], expanded, NEG_INF)
    tiles = jnp.concatenate([tiles, jnp.full((nh, 1, GRID_W, GRID_W), NEG_INF, F32)], axis=1)
    invalid = 2 * NA_ROWS - 1
    dr = np.full((len(keys), NA_QROWS, NA_KROWS), invalid, np.int32)
    for v, (rs_rel, r0_rel) in enumerate(keys):
        for i in range(NA_QROWS):
            for j in range(NA_KROWS):
                if rs_rel[i] <= j < rs_rel[i] + kh:
                    dr[v, i, j] = j - (r0_rel + i) + NA_ROWS - 1
    return tiles, dr


def _na_kernel(q_ref, k_ref, v_ref, tiles_ref, o_ref, bias_scr, *, starts, variants, tile_idx, scale):
    qblk = NA_QROWS * GRID_W
    kblk = NA_KROWS * GRID_W

    @pl.when(pl.program_id(1) == 0)
    def _():
        for var in range(tile_idx.shape[0]):
            for i in range(NA_QROWS):
                strip = jnp.concatenate([tiles_ref[int(tile_idx[var, i, j])] for j in range(NA_KROWS)], axis=1)
                bias_scr[var, i * GRID_W:(i + 1) * GRID_W, :] = strip

    for qb, (ws, var) in enumerate(zip(starts, variants)):
        q = (q_ref[qb * qblk:(qb + 1) * qblk, :].astype(F32) * scale).astype(BF16)
        k = k_ref[ws * GRID_W:ws * GRID_W + kblk, :]
        v = v_ref[ws * GRID_W:ws * GRID_W + kblk, :]
        s = lax.dot_general(q, k, (((1,), (1,)), ((), ())), preferred_element_type=F32)
        s = s + bias_scr[var]
        m = jnp.max(s, axis=-1, keepdims=True)
        e = jnp.exp2(s - m).astype(BF16)
        ol = jnp.dot(e, jnp.concatenate([v, jnp.ones_like(v)], axis=1), preferred_element_type=F32)
        o_ref[qb * qblk:(qb + 1) * qblk, :] = (ol[:, :HEAD_DIM] / ol[:, HEAD_DIM:]).astype(o_ref.dtype)


def neighbourhood_attention(p3, rpb, col0, nh):
    bsz, seq, _ = p3.shape
    rows = seq // GRID_W
    starts, variants, keys, _ = _na_block_layout(rows)
    tiles, tile_idx = _na_tiles(rpb, rows)
    nvar = len(keys)
    ntiles = tiles.shape[1]
    qblk, kblk = NA_QROWS * GRID_W, NA_KROWS * GRID_W
    kern = functools.partial(_na_kernel, starts=starts, variants=variants, tile_idx=tile_idx,
                             scale=HEAD_DIM ** -0.5 * LOG2E)
    head_spec = lambda off: pl.BlockSpec((None, seq, HEAD_DIM), lambda h, b: (b, 0, col0 + off + h))
    return pl.pallas_call(
        kern,
        grid=(nh, bsz),
        in_specs=[head_spec(0), head_spec(nh), head_spec(2 * nh),
                  pl.BlockSpec((None, ntiles, GRID_W, GRID_W), lambda h, b: (h, 0, 0, 0))],
        out_specs=pl.BlockSpec((None, seq, HEAD_DIM), lambda h, b: (b, 0, h)),
        out_shape=jax.ShapeDtypeStruct((bsz, seq, nh * HEAD_DIM), BF16),
        scratch_shapes=[pltpu.VMEM((nvar, qblk, kblk), F32)],
        compiler_params=_params(("parallel", "arbitrary"), 32),
        name="neighbourhood_attention",
    )(p3, p3, p3, tiles)


def _dil_windows(length):
    nk = min(2 * DIL_QBLK, length)
    out = []
    for i in range(length // DIL_QBLK):
        q0 = i * DIL_QBLK
        k0 = int(np.clip(q0 - DIL_QBLK // 2, 0, length - nk))
        out.append((q0, k0, nk))
    return out


def _dil_tables(seq, nh):
    slopes = np.array([2.0 ** (-8.0 * (h + 1) / nh) for h in range(nh)], dtype=np.float32)
    var_index, absd_tabs, dil_of = {}, [], []
    plan = []
    for window, dil in DIL_PATTERNS:
        radius = window // (2 * dil)
        length = seq // dil
        blocks = []
        for q0, k0, nk in _dil_windows(length):
            key = (dil, radius, k0 - q0, nk)
            if key not in var_index:
                var_index[key] = len(absd_tabs)
                delta = (k0 + np.arange(2 * DIL_QBLK)[None, :]) - (q0 + np.arange(DIL_QBLK)[:, None])
                absd = np.abs(delta).astype(np.float32)
                valid = (np.abs(delta) <= radius) & (np.arange(2 * DIL_QBLK)[None, :] < nk)
                absd_tabs.append((absd * dil, valid))
            blocks.append((q0, k0, nk, var_index[key]))
        plan.append((dil, length, blocks))
    pen = np.stack([a for a, _ in absd_tabs])
    valid = np.stack([v for _, v in absd_tabs])
    bias = np.where(valid[None], -(slopes[:, None, None, None] * pen[None]).astype(np.float32), np.float32(NEG_INF))
    return plan, bias.astype(np.float32)


def _merge_partial(a, b):
    (oa, ma, la), (ob, mb, lb) = a, b
    a_is_max = ma >= mb
    t = jnp.exp2(jnp.minimum(ma, mb) - jnp.maximum(ma, mb))
    ta = jnp.where(a_is_max, 1.0, t)
    tb = jnp.where(a_is_max, t, 1.0)
    return ta * oa + tb * ob, jnp.maximum(ma, mb), ta * la + tb * lb


def _dil_kernel(q_ref, k_ref, v_ref, bias_ref, o_ref, qf, kf, vf, qd, kd, vd, p1, pd, pc, pn, *, plan, scale):
    (_, _, blocks1), (d, len_d, blocks_d), (_, len_c, blocks_c) = plan
    qf[...] = q_ref[...].astype(F32) * scale
    kf[...] = k_ref[...].astype(F32)
    vf[...] = v_ref[...].astype(F32)

    def attend(q, k, v, var, nk):
        s = lax.dot_general(q, k, (((1,), (1,)), ((), ())), preferred_element_type=F32)
        s = s + bias_ref[var][:, :nk]
        m = jnp.max(s, axis=-1, keepdims=True)
        e = jnp.exp2(s - m).astype(BF16)
        ol = jnp.dot(e, jnp.concatenate([v, jnp.ones_like(v)], axis=1), preferred_element_type=F32)
        o = ol[:, :HEAD_DIM]
        return o, jnp.broadcast_to(m, o.shape), ol[:, HEAD_DIM:]

    for q0, k0, nk, var in blocks1:
        part = attend(qf[q0:q0 + DIL_QBLK, :].astype(BF16), kf[k0:k0 + nk, :].astype(BF16),
                      vf[k0:k0 + nk, :].astype(BF16), var, nk)
        for j in range(3):
            p1[j, q0:q0 + DIL_QBLK, :] = part[j]

    for r in range(d):
        qd[r] = qf[pl.ds(r, len_d, stride=d), :]
        kd[r] = kf[pl.ds(r, len_d, stride=d), :]
        vd[r] = vf[pl.ds(r, len_d, stride=d), :]

    for r in range(d):
        qs, ks, vs = qd[r].astype(BF16), kd[r].astype(BF16), vd[r].astype(BF16)
        for q0, k0, nk, var in blocks_d:
            part = attend(qs[q0:q0 + DIL_QBLK], ks[k0:k0 + nk], vs[k0:k0 + nk], var, nk)
            for j in range(3):
                pd[j, r, q0:q0 + DIL_QBLK, :] = part[j]
        for a in range(d):
            qs = qd[r, pl.ds(a, len_c, stride=d), :].astype(BF16)
            ks = kd[r, pl.ds(a, len_c, stride=d), :].astype(BF16)
            vs = vd[r, pl.ds(a, len_c, stride=d), :].astype(BF16)
            for q0, k0, nk, var in blocks_c:
                part = attend(qs[q0:q0 + DIL_QBLK], ks[k0:k0 + nk], vs[k0:k0 + nk], var, nk)
                for j in range(3):
                    pc[j, r, pl.ds(q0 * d + a, DIL_QBLK, stride=d), :] = part[j]

    for r in range(d):
        part = _merge_partial(tuple(pd[j, r] for j in range(3)), tuple(pc[j, r] for j in range(3)))
        for j in range(3):
            pn[j, pl.ds(r, len_d, stride=d), :] = part[j]

    o, _, l = _merge_partial(tuple(p1[j] for j in range(3)), tuple(pn[j] for j in range(3)))
    o_ref[...] = (o / l).astype(o_ref.dtype)


def dilated_attention(p3, col0, nh):
    bsz, seq, _ = p3.shape
    plan, bias_np = _dil_tables(seq, nh)
    dils = [dil for dil, _, _ in plan]
    assert len(dils) == 3 and dils[0] == 1 and dils[2] == dils[1] * dils[1], dils
    d = dils[1]
    nvar = bias_np.shape[1]
    kern = functools.partial(_dil_kernel, plan=plan, scale=HEAD_DIM ** -0.5 * LOG2E)
    head_spec = lambda off: pl.BlockSpec((None, seq, HEAD_DIM), lambda h, b: (b, 0, col0 + off + h))
    natural = pltpu.VMEM((seq, HEAD_DIM), F32)
    by_residue = pltpu.VMEM((d, seq // d, HEAD_DIM), F32)
    partial_natural = pltpu.VMEM((3, seq, HEAD_DIM), F32)
    partial_by_residue = pltpu.VMEM((3, d, seq // d, HEAD_DIM), F32)
    return pl.pallas_call(
        kern,
        grid=(nh, bsz),
        in_specs=[head_spec(0), head_spec(nh), head_spec(2 * nh),
                  pl.BlockSpec((None, nvar, DIL_QBLK, 2 * DIL_QBLK), lambda h, b: (h, 0, 0, 0))],
        out_specs=pl.BlockSpec((None, seq, HEAD_DIM), lambda h, b: (b, 0, h)),
        out_shape=jax.ShapeDtypeStruct((bsz, seq, nh * HEAD_DIM), BF16),
        scratch_shapes=[natural] * 3 + [by_residue] * 3 + [partial_natural, partial_by_residue,
                                                            partial_by_residue, partial_natural],
        compiler_params=_params(("parallel", "parallel"), 40),
        name="dilated_attention",
    )(p3, p3, p3, jnp.asarray(bias_np * np.float32(LOG2E)))


def _residual_norm_router(acc, x_ref, g_ref, r_ref, xo_ref, h_ref, lg_ref):
    ne = lg_ref.shape[1]
    for c in range(acc.shape[0] // EPILOGUE_ROWS):
        rows = slice(c * EPILOGUE_ROWS, (c + 1) * EPILOGUE_ROWS)
        xn = x_ref[rows, :] + acc[rows, :]
        xo_ref[rows, :] = xn
        ms = jnp.mean(xn * xn, axis=-1, keepdims=True)
        h = xn * lax.rsqrt(ms + RMS_EPS) * g_ref[...]
        h_hi = h.astype(BF16)
        h_ref[rows, :] = h_hi
        s = jnp.dot(h_hi, r_ref[...], preferred_element_type=F32)
        lg_ref[rows, :] = s[:, :ne] + s[:, ne:]


def _out_proj_router_kernel(ya_ref, yb_ref, wa_ref, wb_ref, x_ref, g_ref, r_ref, xo_ref, h_ref, lg_ref):
    acc = jnp.dot(ya_ref[...], wa_ref[...], preferred_element_type=F32)
    acc = acc + jnp.dot(yb_ref[...], wb_ref[...], preferred_element_type=F32)
    _residual_norm_router(acc, x_ref, g_ref, r_ref, xo_ref, h_ref, lg_ref)


def _conv_out_proj_router_kernel(b_ref, c_ref, xin_ref, cp_ref, xp_ref, cn_ref, xn_ref, taps_ref, yb_ref,
                                 wa_ref, wb_ref, x_ref, g_ref, r_ref, xo_ref, h_ref, lg_ref, *, tiles_per_seq):
    tm = c_ref.shape[0]
    pos = lax.rem(pl.program_id(0), tiles_per_seq)
    acc = jnp.dot(yb_ref[...], wb_ref[...], preferred_element_type=F32)
    z = c_ref[...].astype(F32) * xin_ref[...].astype(F32)
    z_before = (cp_ref[...].astype(F32) * xp_ref[...].astype(F32))[CONV_HALO - 1:CONV_HALO, :]
    z_after = (cn_ref[...].astype(F32) * xn_ref[...].astype(F32))[0:1, :]
    z_before = jnp.where(pos == 0, 0.0, z_before)
    z_after = jnp.where(pos == tiles_per_seq - 1, 0.0, z_after)
    row = lax.broadcasted_iota(jnp.int32, z.shape, 0)
    z_prev = jnp.where(row == 0, z_before, pltpu.roll(z, 1, axis=0))
    z_next = jnp.where(row == tm - 1, z_after, pltpu.roll(z, tm - 1, axis=0))
    y = taps_ref[0:1, :] * z_prev + taps_ref[1:2, :] * z + taps_ref[2:3, :] * z_next
    ya = (b_ref[...].astype(F32) * y).astype(BF16)
    acc = acc + jnp.dot(ya, wa_ref[...], preferred_element_type=F32)
    _residual_norm_router(acc, x_ref, g_ref, r_ref, xo_ref, h_ref, lg_ref)


def _router_split(router):
    r_hi = router.astype(BF16)
    r_lo = (router - r_hi.astype(F32)).astype(BF16)
    return jnp.concatenate([r_hi, r_lo], axis=1)


def conv_out_proj_norm_router(p2d, taps, yb, w_bf16, x2d, g, router, seq, tm=512):
    t, d = x2d.shape
    width = yb.shape[1]
    ne = router.shape[1]
    assert seq % tm == 0 and tm % CONV_HALO == 0
    halo_per_tile = tm // CONV_HALO
    last_halo = t // CONV_HALO - 1
    tile = lambda col: pl.BlockSpec((tm, width), lambda i: (i, col))
    before = lambda col: pl.BlockSpec((CONV_HALO, width), lambda i: (jnp.maximum(i * halo_per_tile - 1, 0), col))
    after = lambda col: pl.BlockSpec((CONV_HALO, width),
                                     lambda i: (jnp.minimum((i + 1) * halo_per_tile, last_halo), col))
    return pl.pallas_call(
        functools.partial(_conv_out_proj_router_kernel, tiles_per_seq=seq // tm),
        grid=(t // tm,),
        in_specs=[tile(0), tile(1), tile(2), before(1), before(2), after(1), after(2),
                  pl.BlockSpec((CONV_W, width), lambda i: (0, 0)),
                  pl.BlockSpec((tm, width), lambda i: (i, 0)),
                  pl.BlockSpec((width, d), lambda i: (0, 0)),
                  pl.BlockSpec((width, d), lambda i: (1, 0)),
                  pl.BlockSpec((tm, d), lambda i: (i, 0)),
                  pl.BlockSpec((1, d), lambda i: (0, 0)),
                  pl.BlockSpec((d, 2 * ne), lambda i: (0, 0))],
        out_specs=[pl.BlockSpec((tm, d), lambda i: (i, 0)),
                   pl.BlockSpec((tm, d), lambda i: (i, 0)),
                   pl.BlockSpec((tm, ne), lambda i: (i, 0))],
        out_shape=[jax.ShapeDtypeStruct((t, d), F32),
                   jax.ShapeDtypeStruct((t, d), BF16),
                   jax.ShapeDtypeStruct((t, ne), F32)],
        compiler_params=_params(("parallel",), 56),
        name="conv_out_proj_norm_router",
    )(p2d, p2d, p2d, p2d, p2d, p2d, p2d, taps.astype(F32), yb, w_bf16, w_bf16, x2d, g.reshape(1, d),
      _router_split(router))


def out_proj_norm_router(ya, yb, w_bf16, x2d, g, router, tm=512):
    t, d = x2d.shape
    ka = ya.shape[1]
    kb = yb.shape[1]
    assert ka == kb
    ne = router.shape[1]
    r_split = _router_split(router)
    return pl.pallas_call(
        _out_proj_router_kernel,
        grid=(t // tm,),
        in_specs=[pl.BlockSpec((tm, ka), lambda i: (i, 0)),
                  pl.BlockSpec((tm, kb), lambda i: (i, 0)),
                  pl.BlockSpec((ka, d), lambda i: (0, 0)),
                  pl.BlockSpec((kb, d), lambda i: (1, 0)),
                  pl.BlockSpec((tm, d), lambda i: (i, 0)),
                  pl.BlockSpec((1, d), lambda i: (0, 0)),
                  pl.BlockSpec((d, 2 * ne), lambda i: (0, 0))],
        out_specs=[pl.BlockSpec((tm, d), lambda i: (i, 0)),
                   pl.BlockSpec((tm, d), lambda i: (i, 0)),
                   pl.BlockSpec((tm, ne), lambda i: (i, 0))],
        out_shape=[jax.ShapeDtypeStruct((t, d), F32),
                   jax.ShapeDtypeStruct((t, d), BF16),
                   jax.ShapeDtypeStruct((t, ne), F32)],
        compiler_params=_params(("parallel",), 56),
        name="out_proj_norm_router",
    )(ya, yb, w_bf16, w_bf16, x2d, g.reshape(1, d), r_split)


def _route_kernel(lg_ref, slot_ref, aff_ref, *, cap):
    nb, ne, seq = aff_ref.shape
    logits = lg_ref[...].reshape(nb, ne, seq)
    mx = jnp.max(logits, axis=1, keepdims=True)
    ex = jnp.exp(logits - mx)
    a3 = ex / jnp.sum(ex, axis=1, keepdims=True)
    aff_ref[...] = a3
    a = a3.reshape(nb * ne, seq)
    nrows = nb * ne
    lanes = 128
    r = lax.broadcasted_iota(jnp.int32, (lanes, lanes), 0)
    c = lax.broadcasted_iota(jnp.int32, (lanes, lanes), 1)
    tri = jnp.where(r <= c, 1.0, 0.0).astype(BF16)

    def count(mask_f):
        return jnp.sum(mask_f, axis=1, keepdims=True)

    def prefix(x):
        parts = []
        run = jnp.zeros((nrows, 1), F32)
        for j in range(seq // lanes):
            blk = x[:, j * lanes:(j + 1) * lanes]
            inc = jnp.dot(blk.astype(BF16), tri, preferred_element_type=F32)
            parts.append(inc - blk + run)
            run = run + count(blk)
        return jnp.concatenate(parts, axis=1)

    def body(i, ans):
        cand = ans | jnp.left_shift(jnp.int32(1), 30 - i)
        cnt = count(jnp.where(a >= lax.bitcast_convert_type(cand, F32), 1.0, 0.0))
        return jnp.where(cnt >= cap, cand, ans)

    ans = lax.fori_loop(0, 31, body, jnp.zeros((nrows, 1), jnp.int32))
    thr = lax.bitcast_convert_type(ans, F32)
    gt = jnp.where(a > thr, 1.0, 0.0)
    eq = jnp.where(a >= thr, 1.0, 0.0) - gt
    need = cap - count(gt)
    sel = gt + eq * jnp.where(prefix(eq) < need, 1.0, 0.0)
    slot_ref[...] = jnp.where(sel > 0.5, prefix(sel), -1.0).astype(jnp.int32).reshape(nb, ne, seq)


def route(logits_rows, bsz, ne, seq, cap):
    return pl.pallas_call(
        functools.partial(_route_kernel, cap=cap),
        grid=(1,),
        in_specs=[pl.BlockSpec((bsz * ne, seq), lambda i: (0, 0))],
        out_specs=[pl.BlockSpec((bsz, ne, seq), lambda i: (0, 0, 0)),
                   pl.BlockSpec((bsz, ne, seq), lambda i: (0, 0, 0))],
        out_shape=[jax.ShapeDtypeStruct((bsz, ne, seq), jnp.int32),
                   jax.ShapeDtypeStruct((bsz, ne, seq), F32)],
        compiler_params=_params(("arbitrary",), 40),
        name="route",
    )(logits_rows)


def _gather_kernel(slot_ref, aff_ref, h_ref, xe_ref, gate_ref, *, cap, ncol):
    eg, seq = slot_ref.shape
    d = h_ref.shape[1]
    cidx = lax.broadcasted_iota(jnp.int32, (cap, seq), 0)
    rows = []
    for j in range(eg):
        hit = cidx == slot_ref[j:j + 1, :]
        rows.append(jnp.where(hit, 1.0, 0.0).astype(BF16))
        gate_ref[j] = jnp.sum(jnp.where(hit, aff_ref[j:j + 1, :], 0.0), axis=1, keepdims=True)
    p = jnp.concatenate(rows, axis=0)
    dc = d // ncol
    for c in range(ncol):
        xe = jnp.dot(p, h_ref[:, c * dc:(c + 1) * dc], preferred_element_type=F32)
        xe_ref[:, :, c * dc:(c + 1) * dc] = xe.reshape(eg, cap, dc).astype(xe_ref.dtype)


def gather_rows(slot, aff, h, cap, egroup=4, ncol=2):
    bsz, ne, seq = slot.shape
    d = h.shape[1]
    ngroups = ne // egroup
    grouped = lambda a: a.reshape(bsz, ngroups, egroup, seq)
    return pl.pallas_call(
        functools.partial(_gather_kernel, cap=cap, ncol=ncol),
        grid=(bsz, ngroups),
        in_specs=[pl.BlockSpec((None, None, egroup, seq), lambda b, g: (b, g, 0, 0)),
                  pl.BlockSpec((None, None, egroup, seq), lambda b, g: (b, g, 0, 0)),
                  pl.BlockSpec((seq, d), lambda b, g: (b, 0))],
        out_specs=[pl.BlockSpec((egroup, None, cap, d), lambda b, g: (g, b, 0, 0)),
                   pl.BlockSpec((egroup, None, cap, 1), lambda b, g: (g, b, 0, 0))],
        out_shape=[jax.ShapeDtypeStruct((ne, bsz, cap, d), BF16),
                   jax.ShapeDtypeStruct((ne, bsz, cap, 1), F32)],
        compiler_params=_params(("parallel", "parallel"), 48),
        name="moe_gather",
    )(grouped(slot), grouped(aff), h)


def _combine_kernel(slot_t_ref, ye_ref, x_ref, o_ref, *, cap):
    tb, ne = slot_t_ref.shape
    tn = x_ref.shape[1]
    cidx = lax.broadcasted_iota(jnp.int32, (tb, cap), 1)
    pt = jnp.concatenate([jnp.where(cidx == slot_t_ref[:, e:e + 1], 1.0, 0.0).astype(BF16) for e in range(ne)],
                         axis=1)
    ye = ye_ref[...].reshape(ne * cap, tn)
    o_ref[...] = x_ref[...] + jnp.dot(pt, ye, preferred_element_type=F32)


def combine(slot_t, ye, x2d, cap, tn=1024, tb=512):
    bsz, seq, ne = slot_t.shape
    d = x2d.shape[1]
    nt = seq // tb
    return pl.pallas_call(
        functools.partial(_combine_kernel, cap=cap),
        grid=(bsz, d // tn, nt),
        in_specs=[pl.BlockSpec((None, tb, ne), lambda b, n, t: (b, t, 0)),
                  pl.BlockSpec((ne, None, cap, tn), lambda b, n, t: (0, b, 0, n)),
                  pl.BlockSpec((tb, tn), lambda b, n, t: (b * nt + t, n))],
        out_specs=pl.BlockSpec((tb, tn), lambda b, n, t: (b * nt + t, n)),
        out_shape=jax.ShapeDtypeStruct(x2d.shape, F32),
        compiler_params=_params(("parallel", "parallel", "parallel"), 48),
        name="moe_combine",
    )(slot_t, ye, x2d)


def _expert_ffn_kernel(xe_ref, wg_ref, wu_ref, wd_ref, gate_ref, o_ref, acc_ref):
    f = pl.program_id(2)
    last = pl.num_programs(2) - 1
    rows_per = xe_ref.shape[0] // FFN_ROW_CHUNKS

    def step(first, final):
        wg = wg_ref[...].astype(BF16)
        wu = wu_ref[...].astype(BF16)
        wd = wd_ref[...].astype(BF16)
        for c in range(FFN_ROW_CHUNKS):
            rows = slice(c * rows_per, (c + 1) * rows_per)
            xe = xe_ref[rows, :]
            g = jnp.dot(xe, wg, preferred_element_type=F32)
            u = jnp.dot(xe, wu, preferred_element_type=F32)
            hid = (jax.nn.silu(g) * u).astype(BF16)
            part = jnp.dot(hid, wd, preferred_element_type=F32)
            total = part if first else acc_ref[rows, :] + part
            if final:
                o_ref[rows, :] = (total * gate_ref[rows, :]).astype(o_ref.dtype)
            else:
                acc_ref[rows, :] = total

    pl.when(f == 0)(lambda: step(True, False))
    pl.when(jnp.logical_and(f > 0, f < last))(lambda: step(False, False))
    pl.when(f == last)(lambda: step(False, True))


def expert_ffn(xe, w_gate, w_up, w_down, layer, gate, tm=1024, tf=256):
    ne, r, d = xe.shape
    fdim = w_gate.shape[3]
    assert fdim // tf >= 2, "the kernel distinguishes first / middle / last hidden tiles"
    return pl.pallas_call(
        _expert_ffn_kernel,
        grid=(ne, r // tm, fdim // tf),
        in_specs=[pl.BlockSpec((None, tm, d), lambda e, m, f: (e, m, 0)),
                  pl.BlockSpec((None, None, d, tf), lambda e, m, f: (layer, e, 0, f)),
                  pl.BlockSpec((None, None, d, tf), lambda e, m, f: (layer, e, 0, f)),
                  pl.BlockSpec((None, None, tf, d), lambda e, m, f: (layer, e, f, 0)),
                  pl.BlockSpec((None, tm, 1), lambda e, m, f: (e, m, 0))],
        out_specs=pl.BlockSpec((None, tm, d), lambda e, m, f: (e, m, 0)),
        out_shape=jax.ShapeDtypeStruct((ne, r, d), BF16),
        scratch_shapes=[pltpu.VMEM((tm, d), F32)],
        compiler_params=_params(("parallel", "parallel", "arbitrary"), 56),
        name="expert_ffn",
    )(xe, w_gate, w_up, w_down, gate)


def _final_norm_kernel(x_ref, g_ref, o_ref):
    x = x_ref[...]
    ms = jnp.mean(x * x, axis=-1, keepdims=True)
    o_ref[...] = x * lax.rsqrt(ms + RMS_EPS) * g_ref[...]


def final_norm(x2d, g, tm=512):
    t, d = x2d.shape
    return pl.pallas_call(
        _final_norm_kernel,
        grid=(t // tm,),
        in_specs=[pl.BlockSpec((tm, d), lambda i: (i, 0)), pl.BlockSpec((1, d), lambda i: (0, 0))],
        out_specs=pl.BlockSpec((tm, d), lambda i: (i, 0)),
        out_shape=jax.ShapeDtypeStruct((t, d), F32),
        compiler_params=_params(("parallel",), 32),
        name="final_norm",
    )(x2d, g.reshape(1, d))


def expert_choice_moe(x2d, h, logits, bsz, seq, layer, w_gate, w_up, w_down):
    d = x2d.shape[1]
    ne = logits.shape[1]
    cap = EC_CAPACITY_FACTOR * seq // ne
    logits_rows = logits.reshape(bsz, seq, ne).transpose(0, 2, 1).reshape(bsz * ne, seq)
    slot, aff = route(logits_rows, bsz, ne, seq, cap)
    xe, gate = gather_rows(slot, aff, h, cap)
    ye = expert_ffn(xe.reshape(ne, bsz * cap, d), w_gate, w_up, w_down, layer,
                    gate.reshape(ne, bsz * cap, 1))
    return combine(slot.transpose(0, 2, 1), ye.reshape(ne, bsz, cap, d), x2d, cap)


def kernel(x, norm_mix, norm_ffn, norm_final, w_in_ab, a_v_norm, a_spatial_w, a_spatial_b, b_rpb, w_out_ab,
           w_in_cd, c_conv, w_out_cd, router, w_gate, w_up, w_down):
    bsz, seq, d = x.shape
    depth = norm_mix.shape[0]
    half = d // 2
    nh = half // HEAD_DIM
    x2d = x.reshape(bsz * seq, d)
    for layer in range(depth):
        i = layer // 2
        if layer % 2 == 0:
            p = rms_matmul(x2d, norm_mix[layer], w_in_ab[i])
            p3 = p.reshape(bsz, seq, -1)
            ya = sgu(p, a_v_norm[i], a_spatial_w[i], a_spatial_b[i], half)
            yb = neighbourhood_attention(p3, b_rpb[i], 2 * half // HEAD_DIM, nh).reshape(bsz * seq, half)
            x2d, h, logits = out_proj_norm_router(ya, yb, w_out_ab[i].astype(BF16), x2d,
                                                  norm_ffn[layer], router[layer])
        else:
            p = rms_matmul(x2d, norm_mix[layer], w_in_cd[i])
            p3 = p.reshape(bsz, seq, -1)
            yd = dilated_attention(p3, 3 * half // HEAD_DIM, nh).reshape(bsz * seq, half)
            x2d, h, logits = conv_out_proj_norm_router(p, c_conv[i], yd, w_out_cd[i].astype(BF16), x2d,
                                                       norm_ffn[layer], router[layer], seq)
        x2d = expert_choice_moe(x2d, h, logits, bsz, seq, layer, w_gate, w_up, w_down)
    return final_norm(x2d, norm_final).reshape(bsz, seq, d)
```

```python
import functools

import numpy as np
import jax
import jax.numpy as jnp
from jax import lax
from jax.experimental import pallas as pl
from jax.experimental.pallas import tpu as pltpu

F32 = jnp.float32
BF16 = jnp.bfloat16

HEAD_DIM = 128
CHUNK = 128
GRID_W = 64
NA_ROWS = 8
NA_COLS = 16
NA_QROWS = 4
NA_KROWS = NA_QROWS + NA_ROWS
CONV_W = 3
CONV_HALO = 16
DIL_PATTERNS = ((128, 1), (512, 4), (2048, 16))
DIL_QBLK = 128
IN_PROJ_NORM_CHUNKS = 4
EPILOGUE_ROWS = 128
FFN_ROW_CHUNKS = 2
EC_CAPACITY_FACTOR = 2
RMS_EPS = 1e-6
LN_EPS = 1e-5
NEG_INF = -1e30
LOG2E = 1.4426950408889634
MIB = 1024 * 1024


def _params(semantics, vmem_mib):
    return pltpu.CompilerParams(dimension_semantics=semantics,
                                vmem_limit_bytes=int(vmem_mib * MIB))


def _rms_matmul_kernel(x_ref, g_ref, w_ref, o_ref, h_scr):
    j = pl.program_id(1)

    def first_column_tile():
        w = w_ref[...].astype(BF16)
        rows_per = x_ref.shape[0] // IN_PROJ_NORM_CHUNKS
        for c in range(IN_PROJ_NORM_CHUNKS):
            rows = slice(c * rows_per, (c + 1) * rows_per)
            x = x_ref[rows, :]
            ms = jnp.mean(x * x, axis=-1, keepdims=True)
            h = (x * lax.rsqrt(ms + RMS_EPS) * g_ref[...]).astype(BF16)
            h_scr[rows, :] = h
            o_ref[rows, :] = jnp.dot(h, w, preferred_element_type=F32).astype(o_ref.dtype)

    def later_column_tile():
        o_ref[...] = jnp.dot(h_scr[...], w_ref[...].astype(BF16), preferred_element_type=F32).astype(o_ref.dtype)

    pl.when(j == 0)(first_column_tile)
    pl.when(j > 0)(later_column_tile)


def rms_matmul(x2d, g, w, tm=1024, tn=1024):
    t, d = x2d.shape
    n = w.shape[1]
    return pl.pallas_call(
        _rms_matmul_kernel,
        grid=(t // tm, n // tn),
        in_specs=[pl.BlockSpec((tm, d), lambda i, j: (i, 0)),
                  pl.BlockSpec((1, d), lambda i, j: (0, 0)),
                  pl.BlockSpec((d, tn), lambda i, j: (0, j))],
        out_specs=pl.BlockSpec((tm, tn), lambda i, j: (i, j)),
        out_shape=jax.ShapeDtypeStruct((t, n), BF16),
        scratch_shapes=[pltpu.VMEM((tm, d), BF16)],
        compiler_params=_params(("parallel", "arbitrary"), 56),
        name="rms_in_proj",
    )(x2d, g.reshape(1, d), w)


def _sgu_kernel(u_ref, v_ref, lng_ref, ws_ref, bias_ref, o_ref):
    tm, width = u_ref.shape
    groups = width // HEAD_DIM
    nchunks = tm // CHUNK
    vf = jax.nn.gelu(v_ref[...].astype(F32))
    mu = jnp.mean(vf, axis=-1, keepdims=True)
    dv = vf - mu
    var = jnp.mean(dv * dv, axis=-1, keepdims=True)
    vn = (dv * lax.rsqrt(var + LN_EPS) * lng_ref[...]).astype(BF16)
    for g in range(groups):
        cols = slice(g * HEAD_DIM, (g + 1) * HEAD_DIM)
        vg = jnp.concatenate([vn[n * CHUNK:(n + 1) * CHUNK, cols] for n in range(nchunks)], axis=1)
        mixed = jnp.dot(ws_ref[g], vg, preferred_element_type=F32)
        for n in range(nchunks):
            rows = slice(n * CHUNK, (n + 1) * CHUNK)
            u = jax.nn.gelu(u_ref[rows, cols].astype(F32))
            o_ref[rows, cols] = (u * (mixed[:, n * CHUNK:(n + 1) * CHUNK] + bias_ref[:, cols])).astype(o_ref.dtype)


def sgu(p2d, ln_g, w_s, b_s, width, tm=512):
    t = p2d.shape[0]
    groups = width // HEAD_DIM
    bias_full = jnp.repeat(b_s.T.astype(F32), HEAD_DIM, axis=1)
    return pl.pallas_call(
        _sgu_kernel,
        grid=(t // tm,),
        in_specs=[pl.BlockSpec((tm, width), lambda i: (i, 0)),
                  pl.BlockSpec((tm, width), lambda i: (i, 1)),
                  pl.BlockSpec((1, width), lambda i: (0, 0)),
                  pl.BlockSpec((groups, CHUNK, CHUNK), lambda i: (0, 0, 0)),
                  pl.BlockSpec((CHUNK, width), lambda i: (0, 0))],
        out_specs=pl.BlockSpec((tm, width), lambda i: (i, 0)),
        out_shape=jax.ShapeDtypeStruct((t, width), BF16),
        compiler_params=_params(("parallel",), 40),
        name="sgu",
    )(p2d, p2d, ln_g.reshape(1, width).astype(F32), w_s.astype(BF16), bias_full)


def _na_block_layout(rows):
    nblk = rows // NA_QROWS
    kh = min(NA_ROWS, rows)
    starts, variants, var_key = [], [], {}
    for qb in range(nblk):
        r0 = qb * NA_QROWS
        ws = int(np.clip(r0 - kh // 2, 0, rows - NA_KROWS))
        rs = np.clip(np.arange(r0, r0 + NA_QROWS) - kh // 2, 0, rows - kh)
        key = (tuple(rs - ws), r0 - ws)
        if key not in var_key:
            var_key[key] = len(var_key)
        starts.append(ws)
        variants.append(var_key[key])
    return starts, variants, list(var_key.keys()), kh


def _na_tiles(rpb, rows):
    nh = rpb.shape[0]
    _, _, keys, kh = _na_block_layout(rows)
    qc = np.arange(GRID_W)[:, None]
    kc = np.arange(GRID_W)[None, :]
    col_start = np.clip(qc - NA_COLS // 2, 0, GRID_W - NA_COLS)
    col_valid = (kc >= col_start) & (kc < col_start + NA_COLS)
    dc_idx = np.clip(kc - qc + NA_COLS - 1, 0, 2 * NA_COLS - 2)
    onehot = (dc_idx[None] == np.arange(2 * NA_COLS - 1)[:, None, None]).astype(np.float32)
    expanded = jnp.einsum('hrc,cqk->hrqk', rpb.astype(F32) * LOG2E, jnp.asarray(onehot),
                          precision=lax.Precision.HIGHEST)
    tiles = jnp.where(jnp.asarray(col_valid)[None, None], expanded, NEG_INF)
    tiles = jnp.concatenate([tiles, jnp.full((nh, 1, GRID_W, GRID_W), NEG_INF, F32)], axis=1)
    invalid = 2 * NA_ROWS - 1
    dr = np.full((len(keys), NA_QROWS, NA_KROWS), invalid, np.int32)
    for v, (rs_rel, r0_rel) in enumerate(keys):
        for i in range(NA_QROWS):
            for j in range(NA_KROWS):
                if rs_rel[i] <= j < rs_rel[i] + kh:
                    dr[v, i, j] = j - (r0_rel + i) + NA_ROWS - 1
    return tiles, dr


def _na_kernel(q_ref, k_ref, v_ref, tiles_ref, o_ref, bias_scr, *, starts, variants, tile_idx, scale):
    qblk = NA_QROWS * GRID_W
    kblk = NA_KROWS * GRID_W

    @pl.when(pl.program_id(1) == 0)
    def _():
        for var in range(tile_idx.shape[0]):
            for i in range(NA_QROWS):
                strip = jnp.concatenate([tiles_ref[int(tile_idx[var, i, j])] for j in range(NA_KROWS)], axis=1)
                bias_scr[var, i * GRID_W:(i + 1) * GRID_W, :] = strip

    for qb, (ws, var) in enumerate(zip(starts, variants)):
        q = (q_ref[qb * qblk:(qb + 1) * qblk, :].astype(F32) * scale).astype(BF16)
        k = k_ref[ws * GRID_W:ws * GRID_W + kblk, :]
        v = v_ref[ws * GRID_W:ws * GRID_W + kblk, :]
        s = lax.dot_general(q, k, (((1,), (1,)), ((), ())), preferred_element_type=F32)
        s = s + bias_scr[var]
        m = jnp.max(s, axis=-1, keepdims=True)
        e = jnp.exp2(s - m).astype(BF16)
        ol = jnp.dot(e, jnp.concatenate([v, jnp.ones_like(v)], axis=1), preferred_element_type=F32)
        o_ref[qb * qblk:(qb + 1) * qblk, :] = (ol[:, :HEAD_DIM] / ol[:, HEAD_DIM:]).astype(o_ref.dtype)


def neighbourhood_attention(p3, rpb, col0, nh):
    bsz, seq, _ = p3.shape
    rows = seq // GRID_W
    starts, variants, keys, _ = _na_block_layout(rows)
    tiles, tile_idx = _na_tiles(rpb, rows)
    nvar = len(keys)
    ntiles = tiles.shape[1]
    qblk, kblk = NA_QROWS * GRID_W, NA_KROWS * GRID_W
    kern = functools.partial(_na_kernel, starts=starts, variants=variants, tile_idx=tile_idx,
                             scale=HEAD_DIM ** -0.5 * LOG2E)
    head_spec = lambda off: pl.BlockSpec((None, seq, HEAD_DIM), lambda h, b: (b, 0, col0 + off + h))
    return pl.pallas_call(
        kern,
        grid=(nh, bsz),
        in_specs=[head_spec(0), head_spec(nh), head_spec(2 * nh),
                  pl.BlockSpec((None, ntiles, GRID_W, GRID_W), lambda h, b: (h, 0, 0, 0))],
        out_specs=pl.BlockSpec((None, seq, HEAD_DIM), lambda h, b: (b, 0, h)),
        out_shape=jax.ShapeDtypeStruct((bsz, seq, nh * HEAD_DIM), BF16),
        scratch_shapes=[pltpu.VMEM((nvar, qblk, kblk), F32)],
        compiler_params=_params(("parallel", "arbitrary"), 32),
        name="neighbourhood_attention",
    )(p3, p3, p3, tiles)


def _dil_windows(length):
    nk = min(2 * DIL_QBLK, length)
    out = []
    for i in range(length // DIL_QBLK):
        q0 = i * DIL_QBLK
        k0 = int(np.clip(q0 - DIL_QBLK // 2, 0, length - nk))
        out.append((q0, k0, nk))
    return out


def _dil_tables(seq, nh):
    slopes = np.array([2.0 ** (-8.0 * (h + 1) / nh) for h in range(nh)], dtype=np.float32)
    var_index, absd_tabs, dil_of = {}, [], []
    plan = []
    for window, dil in DIL_PATTERNS:
        radius = window // (2 * dil)
        length = seq // dil
        blocks = []
        for q0, k0, nk in _dil_windows(length):
            key = (dil, radius, k0 - q0, nk)
            if key not in var_index:
                var_index[key] = len(absd_tabs)
                delta = (k0 + np.arange(2 * DIL_QBLK)[None, :]) - (q0 + np.arange(DIL_QBLK)[:, None])
                absd = np.abs(delta).astype(np.float32)
                valid = (np.abs(delta) <= radius) & (np.arange(2 * DIL_QBLK)[None, :] < nk)
                absd_tabs.append((absd * dil, valid))
            blocks.append((q0, k0, nk, var_index[key]))
        plan.append((dil, length, blocks))
    pen = np.stack([a for a, _ in absd_tabs])
    valid = np.stack([v for _, v in absd_tabs])
    bias = np.where(valid[None], -(slopes[:, None, None, None] * pen[None]).astype(np.float32), np.float32(NEG_INF))
    return plan, bias.astype(np.float32)


def _merge_partial(a, b):
    (oa, ma, la), (ob, mb, lb) = a, b
    a_is_max = ma >= mb
    t = jnp.exp2(jnp.minimum(ma, mb) - jnp.maximum(ma, mb))
    ta = jnp.where(a_is_max, 1.0, t)
    tb = jnp.where(a_is_max, t, 1.0)
    return ta * oa + tb * ob, jnp.maximum(ma, mb), ta * la + tb * lb


def _dil_kernel(q_ref, k_ref, v_ref, bias_ref, o_ref, qf, kf, vf, qd, kd, vd, p1, pd, pc, pn, *, plan, scale):
    (_, _, blocks1), (d, len_d, blocks_d), (_, len_c, blocks_c) = plan
    qf[...] = q_ref[...].astype(F32) * scale
    kf[...] = k_ref[...].astype(F32)
    vf[...] = v_ref[...].astype(F32)

    def attend(q, k, v, var, nk):
        s = lax.dot_general(q, k, (((1,), (1,)), ((), ())), preferred_element_type=F32)
        s = s + bias_ref[var][:, :nk]
        m = jnp.max(s, axis=-1, keepdims=True)
        e = jnp.exp2(s - m).astype(BF16)
        ol = jnp.dot(e, jnp.concatenate([v, jnp.ones_like(v)], axis=1), preferred_element_type=F32)
        o = ol[:, :HEAD_DIM]
        return o, jnp.broadcast_to(m, o.shape), ol[:, HEAD_DIM:]

    for q0, k0, nk, var in blocks1:
        part = attend(qf[q0:q0 + DIL_QBLK, :].astype(BF16), kf[k0:k0 + nk, :].astype(BF16),
                      vf[k0:k0 + nk, :].astype(BF16), var, nk)
        for j in range(3):
            p1[j, q0:q0 + DIL_QBLK, :] = part[j]

    for r in range(d):
        qd[r] = qf[pl.ds(r, len_d, stride=d), :]
        kd[r] = kf[pl.ds(r, len_d, stride=d), :]
        vd[r] = vf[pl.ds(r, len_d, stride=d), :]

    for r in range(d):
        qs, ks, vs = qd[r].astype(BF16), kd[r].astype(BF16), vd[r].astype(BF16)
        for q0, k0, nk, var in blocks_d:
            part = attend(qs[q0:q0 + DIL_QBLK], ks[k0:k0 + nk], vs[k0:k0 + nk], var, nk)
            for j in range(3):
                pd[j, r, q0:q0 + DIL_QBLK, :] = part[j]
        for a in range(d):
            qs = qd[r, pl.ds(a, len_c, stride=d), :].astype(BF16)
            ks = kd[r, pl.ds(a, len_c, stride=d), :].astype(BF16)
            vs = vd[r, pl.ds(a, len_c, stride=d), :].astype(BF16)
            for q0, k0, nk, var in blocks_c:
                part = attend(qs[q0:q0 + DIL_QBLK], ks[k0:k0 + nk], vs[k0:k0 + nk], var, nk)
                for j in range(3):
                    pc[j, r, pl.ds(q0 * d + a, DIL_QBLK, stride=d), :] = part[j]

    for r in range(d):
        part = _merge_partial(tuple(pd[j, r] for j in range(3)), tuple(pc[j, r] for j in range(3)))
        for j in range(3):
            pn[j, pl.ds(r, len_d, stride=d), :] = part[j]

    o, _, l = _merge_partial(tuple(p1[j] for j in range(3)), tuple(pn[j] for j in range(3)))
    o_ref[...] = (o / l).astype(o_ref.dtype)


def dilated_attention(p3, col0, nh):
    bsz, seq, _ = p3.shape
    plan, bias_np = _dil_tables(seq, nh)
    dils = [dil for dil, _, _ in plan]
    assert len(dils) == 3 and dils[0] == 1 and dils[2] == dils[1] * dils[1], dils
    d = dils[1]
    nvar = bias_np.shape[1]
    kern = functools.partial(_dil_kernel, plan=plan, scale=HEAD_DIM ** -0.5 * LOG2E)
    head_spec = lambda off: pl.BlockSpec((None, seq, HEAD_DIM), lambda h, b: (b, 0, col0 + off + h))
    natural = pltpu.VMEM((seq, HEAD_DIM), F32)
    by_residue = pltpu.VMEM((d, seq // d, HEAD_DIM), F32)
    partial_natural = pltpu.VMEM((3, seq, HEAD_DIM), F32)
    partial_by_residue = pltpu.VMEM((3, d, seq // d, HEAD_DIM), F32)
    return pl.pallas_call(
        kern,
        grid=(nh, bsz),
        in_specs=[head_spec(0), head_spec(nh), head_spec(2 * nh),
                  pl.BlockSpec((None, nvar, DIL_QBLK, 2 * DIL_QBLK), lambda h, b: (h, 0, 0, 0))],
        out_specs=pl.BlockSpec((None, seq, HEAD_DIM), lambda h, b: (b, 0, h)),
        out_shape=jax.ShapeDtypeStruct((bsz, seq, nh * HEAD_DIM), BF16),
        scratch_shapes=[natural] * 3 + [by_residue] * 3 + [partial_natural, partial_by_residue,
                                                            partial_by_residue, partial_natural],
        compiler_params=_params(("parallel", "parallel"), 40),
        name="dilated_attention",
    )(p3, p3, p3, jnp.asarray(bias_np * np.float32(LOG2E)))


def _residual_norm_router(acc, x_ref, g_ref, r_ref, xo_ref, h_ref, lg_ref):
    ne = lg_ref.shape[1]
    for c in range(acc.shape[0] // EPILOGUE_ROWS):
        rows = slice(c * EPILOGUE_ROWS, (c + 1) * EPILOGUE_ROWS)
        xn = x_ref[rows, :] + acc[rows, :]
        xo_ref[rows, :] = xn
        ms = jnp.mean(xn * xn, axis=-1, keepdims=True)
        h = xn * lax.rsqrt(ms + RMS_EPS) * g_ref[...]
        h_hi = h.astype(BF16)
        h_ref[rows, :] = h_hi
        s = jnp.dot(h_hi, r_ref[...], preferred_element_type=F32)
        lg_ref[rows, :] = s[:, :ne] + s[:, ne:]


def _out_proj_router_kernel(ya_ref, yb_ref, wa_ref, wb_ref, x_ref, g_ref, r_ref, xo_ref, h_ref, lg_ref):
    acc = jnp.dot(ya_ref[...], wa_ref[...], preferred_element_type=F32)
    acc = acc + jnp.dot(yb_ref[...], wb_ref[...], preferred_element_type=F32)
    _residual_norm_router(acc, x_ref, g_ref, r_ref, xo_ref, h_ref, lg_ref)


def _conv_out_proj_router_kernel(b_ref, c_ref, xin_ref, cp_ref, xp_ref, cn_ref, xn_ref, taps_ref, yb_ref,
                                 wa_ref, wb_ref, x_ref, g_ref, r_ref, xo_ref, h_ref, lg_ref, *, tiles_per_seq):
    tm = c_ref.shape[0]
    pos = lax.rem(pl.program_id(0), tiles_per_seq)
    acc = jnp.dot(yb_ref[...], wb_ref[...], preferred_element_type=F32)
    z = c_ref[...].astype(F32) * xin_ref[...].astype(F32)
    z_before = (cp_ref[...].astype(F32) * xp_ref[...].astype(F32))[CONV_HALO - 1:CONV_HALO, :]
    z_after = (cn_ref[...].astype(F32) * xn_ref[...].astype(F32))[0:1, :]
    z_before = jnp.where(pos == 0, 0.0, z_before)
    z_after = jnp.where(pos == tiles_per_seq - 1, 0.0, z_after)
    row = lax.broadcasted_iota(jnp.int32, z.shape, 0)
    z_prev = jnp.where(row == 0, z_before, pltpu.roll(z, 1, axis=0))
    z_next = jnp.where(row == tm - 1, z_after, pltpu.roll(z, tm - 1, axis=0))
    y = taps_ref[0:1, :] * z_prev + taps_ref[1:2, :] * z + taps_ref[2:3, :] * z_next
    ya = (b_ref[...].astype(F32) * y).astype(BF16)
    acc = acc + jnp.dot(ya, wa_ref[...], preferred_element_type=F32)
    _residual_norm_router(acc, x_ref, g_ref, r_ref, xo_ref, h_ref, lg_ref)


def _router_split(router):
    r_hi = router.astype(BF16)
    r_lo = (router - r_hi.astype(F32)).astype(BF16)
    return jnp.concatenate([r_hi, r_lo], axis=1)


def conv_out_proj_norm_router(p2d, taps, yb, w_bf16, x2d, g, router, seq, tm=512):
    t, d = x2d.shape
    width = yb.shape[1]
    ne = router.shape[1]
    assert seq % tm == 0 and tm % CONV_HALO == 0
    halo_per_tile = tm // CONV_HALO
    last_halo = t // CONV_HALO - 1
    tile = lambda col: pl.BlockSpec((tm, width), lambda i: (i, col))
    before = lambda col: pl.BlockSpec((CONV_HALO, width), lambda i: (jnp.maximum(i * halo_per_tile - 1, 0), col))
    after = lambda col: pl.BlockSpec((CONV_HALO, width),
                                     lambda i: (jnp.minimum((i + 1) * halo_per_tile, last_halo), col))
    return pl.pallas_call(
        functools.partial(_conv_out_proj_router_kernel, tiles_per_seq=seq // tm),
        grid=(t // tm,),
        in_specs=[tile(0), tile(1), tile(2), before(1), before(2), after(1), after(2),
                  pl.BlockSpec((CONV_W, width), lambda i: (0, 0)),
                  pl.BlockSpec((tm, width), lambda i: (i, 0)),
                  pl.BlockSpec((width, d), lambda i: (0, 0)),
                  pl.BlockSpec((width, d), lambda i: (1, 0)),
                  pl.BlockSpec((tm, d), lambda i: (i, 0)),
                  pl.BlockSpec((1, d), lambda i: (0, 0)),
                  pl.BlockSpec((d, 2 * ne), lambda i: (0, 0))],
        out_specs=[pl.BlockSpec((tm, d), lambda i: (i, 0)),
                   pl.BlockSpec((tm, d), lambda i: (i, 0)),
                   pl.BlockSpec((tm, ne), lambda i: (i, 0))],
        out_shape=[jax.ShapeDtypeStruct((t, d), F32),
                   jax.ShapeDtypeStruct((t, d), BF16),
                   jax.ShapeDtypeStruct((t, ne), F32)],
        compiler_params=_params(("parallel",), 56),
        name="conv_out_proj_norm_router",
    )(p2d, p2d, p2d, p2d, p2d, p2d, p2d, taps.astype(F32), yb, w_bf16, w_bf16, x2d, g.reshape(1, d),
      _router_split(router))


def out_proj_norm_router(ya, yb, w_bf16, x2d, g, router, tm=512):
    t, d = x2d.shape
    ka = ya.shape[1]
    kb = yb.shape[1]
    assert ka == kb
    ne = router.shape[1]
    r_split = _router_split(router)
    return pl.pallas_call(
        _out_proj_router_kernel,
        grid=(t // tm,),
        in_specs=[pl.BlockSpec((tm, ka), lambda i: (i, 0)),
                  pl.BlockSpec((tm, kb), lambda i: (i, 0)),
                  pl.BlockSpec((ka, d), lambda i: (0, 0)),
                  pl.BlockSpec((kb, d), lambda i: (1, 0)),
                  pl.BlockSpec((tm, d), lambda i: (i, 0)),
                  pl.BlockSpec((1, d), lambda i: (0, 0)),
                  pl.BlockSpec((d, 2 * ne), lambda i: (0, 0))],
        out_specs=[pl.BlockSpec((tm, d), lambda i: (i, 0)),
                   pl.BlockSpec((tm, d), lambda i: (i, 0)),
                   pl.BlockSpec((tm, ne), lambda i: (i, 0))],
        out_shape=[jax.ShapeDtypeStruct((t, d), F32),
                   jax.ShapeDtypeStruct((t, d), BF16),
                   jax.ShapeDtypeStruct((t, ne), F32)],
        compiler_params=_params(("parallel",), 56),
        name="out_proj_norm_router",
    )(ya, yb, w_bf16, w_bf16, x2d, g.reshape(1, d), r_split)


def _route_kernel(lg_ref, slot_ref, aff_ref, *, cap):
    nb, ne, seq = aff_ref.shape
    logits = lg_ref[...].reshape(nb, ne, seq)
    mx = jnp.max(logits, axis=1, keepdims=True)
    ex = jnp.exp(logits - mx)
    a3 = ex / jnp.sum(ex, axis=1, keepdims=True)
    aff_ref[...] = a3
    a = a3.reshape(nb * ne, seq)
    nrows = nb * ne
    lanes = 128
    r = lax.broadcasted_iota(jnp.int32, (lanes, lanes), 0)
    c = lax.broadcasted_iota(jnp.int32, (lanes, lanes), 1)
    tri = jnp.where(r <= c, 1.0, 0.0).astype(BF16)

    def count(mask_f):
        return jnp.sum(mask_f, axis=1, keepdims=True)

    def prefix(x):
        parts = []
        run = jnp.zeros((nrows, 1), F32)
        for j in range(seq // lanes):
            blk = x[:, j * lanes:(j + 1) * lanes]
            inc = jnp.dot(blk.astype(BF16), tri, preferred_element_type=F32)
            parts.append(inc - blk + run)
            run = run + count(blk)
        return jnp.concatenate(parts, axis=1)

    def body(i, ans):
        cand = ans | jnp.left_shift(jnp.int32(1), 30 - i)
        cnt = count(jnp.where(a >= lax.bitcast_convert_type(cand, F32), 1.0, 0.0))
        return jnp.where(cnt >= cap, cand, ans)

    ans = lax.fori_loop(0, 31, body, jnp.zeros((nrows, 1), jnp.int32))
    thr = lax.bitcast_convert_type(ans, F32)
    gt = jnp.where(a > thr, 1.0, 0.0)
    eq = jnp.where(a >= thr, 1.0, 0.0) - gt
    need = cap - count(gt)
    sel = gt + eq * jnp.where(prefix(eq) < need, 1.0, 0.0)
    slot_ref[...] = jnp.where(sel > 0.5, prefix(sel), -1.0).astype(jnp.int32).reshape(nb, ne, seq)


def route(logits_rows, bsz, ne, seq, cap):
    return pl.pallas_call(
        functools.partial(_route_kernel, cap=cap),
        grid=(1,),
        in_specs=[pl.BlockSpec((bsz * ne, seq), lambda i: (0, 0))],
        out_specs=[pl.BlockSpec((bsz, ne, seq), lambda i: (0, 0, 0)),
                   pl.BlockSpec((bsz, ne, seq), lambda i: (0, 0, 0))],
        out_shape=[jax.ShapeDtypeStruct((bsz, ne, seq), jnp.int32),
                   jax.ShapeDtypeStruct((bsz, ne, seq), F32)],
        compiler_params=_params(("arbitrary",), 40),
        name="route",
    )(logits_rows)


def _gather_kernel(slot_ref, aff_ref, h_ref, xe_ref, gate_ref, *, cap, ncol):
    eg, seq = slot_ref.shape
    d = h_ref.shape[1]
    cidx = lax.broadcasted_iota(jnp.int32, (cap, seq), 0)
    rows = []
    for j in range(eg):
        hit = cidx == slot_ref[j:j + 1, :]
        rows.append(jnp.where(hit, 1.0, 0.0).astype(BF16))
        gate_ref[j] = jnp.sum(jnp.where(hit, aff_ref[j:j + 1, :], 0.0), axis=1, keepdims=True)
    p = jnp.concatenate(rows, axis=0)
    dc = d // ncol
    for c in range(ncol):
        xe = jnp.dot(p, h_ref[:, c * dc:(c + 1) * dc], preferred_element_type=F32)
        xe_ref[:, :, c * dc:(c + 1) * dc] = xe.reshape(eg, cap, dc).astype(xe_ref.dtype)


def gather_rows(slot, aff, h, cap, egroup=4, ncol=2):
    bsz, ne, seq = slot.shape
    d = h.shape[1]
    ngroups = ne // egroup
    grouped = lambda a: a.reshape(bsz, ngroups, egroup, seq)
    return pl.pallas_call(
        functools.partial(_gather_kernel, cap=cap, ncol=ncol),
        grid=(bsz, ngroups),
        in_specs=[pl.BlockSpec((None, None, egroup, seq), lambda b, g: (b, g, 0, 0)),
                  pl.BlockSpec((None, None, egroup, seq), lambda b, g: (b, g, 0, 0)),
                  pl.BlockSpec((seq, d), lambda b, g: (b, 0))],
        out_specs=[pl.BlockSpec((egroup, None, cap, d), lambda b, g: (g, b, 0, 0)),
                   pl.BlockSpec((egroup, None, cap, 1), lambda b, g: (g, b, 0, 0))],
        out_shape=[jax.ShapeDtypeStruct((ne, bsz, cap, d), BF16),
                   jax.ShapeDtypeStruct((ne, bsz, cap, 1), F32)],
        compiler_params=_params(("parallel", "parallel"), 48),
        name="moe_gather",
    )(grouped(slot), grouped(aff), h)


def _combine_kernel(slot_t_ref, ye_ref, x_ref, o_ref, *, cap):
    tb, ne = slot_t_ref.shape
    tn = x_ref.shape[1]
    cidx = lax.broadcasted_iota(jnp.int32, (tb, cap), 1)
    pt = jnp.concatenate([jnp.where(cidx == slot_t_ref[:, e:e + 1], 1.0, 0.0).astype(BF16) for e in range(ne)],
                         axis=1)
    ye = ye_ref[...].reshape(ne * cap, tn)
    o_ref[...] = x_ref[...] + jnp.dot(pt, ye, preferred_element_type=F32)


def combine(slot_t, ye, x2d, cap, tn=1024, tb=1024):
    bsz, seq, ne = slot_t.shape
    d = x2d.shape[1]
    nt = seq // tb
    return pl.pallas_call(
        functools.partial(_combine_kernel, cap=cap),
        grid=(bsz, d // tn, nt),
        in_specs=[pl.BlockSpec((None, tb, ne), lambda b, n, t: (b, t, 0)),
                  pl.BlockSpec((ne, None, cap, tn), lambda b, n, t: (0, b, 0, n)),
                  pl.BlockSpec((tb, tn), lambda b, n, t: (b * nt + t, n))],
        out_specs=pl.BlockSpec((tb, tn), lambda b, n, t: (b * nt + t, n)),
        out_shape=jax.ShapeDtypeStruct(x2d.shape, F32),
        compiler_params=_params(("parallel", "parallel", "parallel"), 48),
        name="moe_combine",
    )(slot_t, ye, x2d)


def _expert_ffn_kernel(xe_ref, wg_ref, wu_ref, wd_ref, gate_ref, o_ref, acc_ref):
    f = pl.program_id(2)
    last = pl.num_programs(2) - 1
    rows_per = xe_ref.shape[0] // FFN_ROW_CHUNKS

    def step(first, final):
        wg = wg_ref[...].astype(BF16)
        wu = wu_ref[...].astype(BF16)
        wd = wd_ref[...].astype(BF16)
        for c in range(FFN_ROW_CHUNKS):
            rows = slice(c * rows_per, (c + 1) * rows_per)
            xe = xe_ref[rows, :]
            g = jnp.dot(xe, wg, preferred_element_type=F32)
            u = jnp.dot(xe, wu, preferred_element_type=F32)
            hid = (jax.nn.silu(g) * u).astype(BF16)
            part = jnp.dot(hid, wd, preferred_element_type=F32)
            total = part if first else acc_ref[rows, :] + part
            if final:
                o_ref[rows, :] = (total * gate_ref[rows, :]).astype(o_ref.dtype)
            else:
                acc_ref[rows, :] = total

    pl.when(f == 0)(lambda: step(True, False))
    pl.when(jnp.logical_and(f > 0, f < last))(lambda: step(False, False))
    pl.when(f == last)(lambda: step(False, True))


def expert_ffn(xe, w_gate, w_up, w_down, layer, gate, tm=1024, tf=256):
    ne, r, d = xe.shape
    fdim = w_gate.shape[3]
    assert fdim // tf >= 2, "the kernel distinguishes first / middle / last hidden tiles"
    return pl.pallas_call(
        _expert_ffn_kernel,
        grid=(ne, r // tm, fdim // tf),
        in_specs=[pl.BlockSpec((None, tm, d), lambda e, m, f: (e, m, 0)),
                  pl.BlockSpec((None, None, d, tf), lambda e, m, f: (layer, e, 0, f)),
                  pl.BlockSpec((None, None, d, tf), lambda e, m, f: (layer, e, 0, f)),
                  pl.BlockSpec((None, None, tf, d), lambda e, m, f: (layer, e, f, 0)),
                  pl.BlockSpec((None, tm, 1), lambda e, m, f: (e, m, 0))],
        out_specs=pl.BlockSpec((None, tm, d), lambda e, m, f: (e, m, 0)),
        out_shape=jax.ShapeDtypeStruct((ne, r, d), BF16),
        scratch_shapes=[pltpu.VMEM((tm, d), F32)],
        compiler_params=_params(("parallel", "parallel", "arbitrary"), 56),
        name="expert_ffn",
    )(xe, w_gate, w_up, w_down, gate)


def _final_norm_kernel(x_ref, g_ref, o_ref):
    x = x_ref[...]
    ms = jnp.mean(x * x, axis=-1, keepdims=True)
    o_ref[...] = x * lax.rsqrt(ms + RMS_EPS) * g_ref[...]


def final_norm(x2d, g, tm=512):
    t, d = x2d.shape
    return pl.pallas_call(
        _final_norm_kernel,
        grid=(t // tm,),
        in_specs=[pl.BlockSpec((tm, d), lambda i: (i, 0)), pl.BlockSpec((1, d), lambda i: (0, 0))],
        out_specs=pl.BlockSpec((tm, d), lambda i: (i, 0)),
        out_shape=jax.ShapeDtypeStruct((t, d), F32),
        compiler_params=_params(("parallel",), 32),
        name="final_norm",
    )(x2d, g.reshape(1, d))


def expert_choice_moe(x2d, h, logits, bsz, seq, layer, w_gate, w_up, w_down):
    d = x2d.shape[1]
    ne = logits.shape[1]
    cap = EC_CAPACITY_FACTOR * seq // ne
    logits_rows = logits.reshape(bsz, seq, ne).transpose(0, 2, 1).reshape(bsz * ne, seq)
    slot, aff = route(logits_rows, bsz, ne, seq, cap)
    xe, gate = gather_rows(slot, aff, h, cap)
    ye = expert_ffn(xe.reshape(ne, bsz * cap, d), w_gate, w_up, w_down, layer,
                    gate.reshape(ne, bsz * cap, 1))
    return combine(slot.transpose(0, 2, 1), ye.reshape(ne, bsz, cap, d), x2d, cap)


def kernel(x, norm_mix, norm_ffn, norm_final, w_in_ab, a_v_norm, a_spatial_w, a_spatial_b, b_rpb, w_out_ab,
           w_in_cd, c_conv, w_out_cd, router, w_gate, w_up, w_down):
    bsz, seq, d = x.shape
    depth = norm_mix.shape[0]
    half = d // 2
    nh = half // HEAD_DIM
    x2d = x.reshape(bsz * seq, d)
    for layer in range(depth):
        i = layer // 2
        if layer % 2 == 0:
            p = rms_matmul(x2d, norm_mix[layer], w_in_ab[i])
            p3 = p.reshape(bsz, seq, -1)
            ya = sgu(p, a_v_norm[i], a_spatial_w[i], a_spatial_b[i], half)
            yb = neighbourhood_attention(p3, b_rpb[i], 2 * half // HEAD_DIM, nh).reshape(bsz * seq, half)
            x2d, h, logits = out_proj_norm_router(ya, yb, w_out_ab[i].astype(BF16), x2d,
                                                  norm_ffn[layer], router[layer])
        else:
            p = rms_matmul(x2d, norm_mix[layer], w_in_cd[i])
            p3 = p.reshape(bsz, seq, -1)
            yd = dilated_attention(p3, 3 * half // HEAD_DIM, nh).reshape(bsz * seq, half)
            x2d, h, logits = conv_out_proj_norm_router(p, c_conv[i], yd, w_out_cd[i].astype(BF16), x2d,
                                                       norm_ffn[layer], router[layer], seq)
        x2d = expert_choice_moe(x2d, h, logits, bsz, seq, layer, w_gate, w_up, w_down)
    return final_norm(x2d, norm_final).reshape(bsz, seq, d)
```

```python
import functools

import numpy as np
import jax
import jax.numpy as jnp
from jax import lax
from jax.experimental import pallas as pl
from jax.experimental.pallas import tpu as pltpu

F32 = jnp.float32
BF16 = jnp.bfloat16

HEAD_DIM = 128
CHUNK = 128
GRID_W = 64
NA_ROWS = 8
NA_COLS = 16
NA_QROWS = 4
NA_KROWS = NA_QROWS + NA_ROWS
CONV_W = 3
CONV_HALO = 16
DIL_PATTERNS = ((128, 1), (512, 4), (2048, 16))
DIL_QBLK = 128
IN_PROJ_NORM_CHUNKS = 4
EPILOGUE_ROWS = 128
FFN_ROW_CHUNKS = 2
EC_CAPACITY_FACTOR = 2
RMS_EPS = 1e-6
LN_EPS = 1e-5
NEG_INF = -1e30
LOG2E = 1.4426950408889634
MIB = 1024 * 1024


def _params(semantics, vmem_mib):
    return pltpu.CompilerParams(dimension_semantics=semantics,
                                vmem_limit_bytes=int(vmem_mib * MIB))


def _rms_matmul_kernel(x_ref, g_ref, w_ref, o_ref, h_scr):
    j = pl.program_id(1)

    def first_column_tile():
        w = w_ref[...].astype(BF16)
        rows_per = x_ref.shape[0] // IN_PROJ_NORM_CHUNKS
        for c in range(IN_PROJ_NORM_CHUNKS):
            rows = slice(c * rows_per, (c + 1) * rows_per)
            x = x_ref[rows, :]
            ms = jnp.mean(x * x, axis=-1, keepdims=True)
            h = (x * lax.rsqrt(ms + RMS_EPS) * g_ref[...]).astype(BF16)
            h_scr[rows, :] = h
            o_ref[rows, :] = jnp.dot(h, w, preferred_element_type=F32).astype(o_ref.dtype)

    def later_column_tile():
        o_ref[...] = jnp.dot(h_scr[...], w_ref[...].astype(BF16), preferred_element_type=F32).astype(o_ref.dtype)

    pl.when(j == 0)(first_column_tile)
    pl.when(j > 0)(later_column_tile)


def rms_matmul(x2d, g, w, tm=1024, tn=1024):
    t, d = x2d.shape
    n = w.shape[1]
    return pl.pallas_call(
        _rms_matmul_kernel,
        grid=(t // tm, n // tn),
        in_specs=[pl.BlockSpec((tm, d), lambda i, j: (i, 0)),
                  pl.BlockSpec((1, d), lambda i, j: (0, 0)),
                  pl.BlockSpec((d, tn), lambda i, j: (0, j))],
        out_specs=pl.BlockSpec((tm, tn), lambda i, j: (i, j)),
        out_shape=jax.ShapeDtypeStruct((t, n), BF16),
        scratch_shapes=[pltpu.VMEM((tm, d), BF16)],
        compiler_params=_params(("parallel", "arbitrary"), 56),
        name="rms_in_proj",
    )(x2d, g.reshape(1, d), w)


def _sgu_kernel(u_ref, v_ref, lng_ref, ws_ref, bias_ref, o_ref):
    tm, width = u_ref.shape
    groups = width // HEAD_DIM
    nchunks = tm // CHUNK
    vf = jax.nn.gelu(v_ref[...].astype(F32))
    mu = jnp.mean(vf, axis=-1, keepdims=True)
    dv = vf - mu
    var = jnp.mean(dv * dv, axis=-1, keepdims=True)
    vn = (dv * lax.rsqrt(var + LN_EPS) * lng_ref[...]).astype(BF16)
    for g in range(groups):
        cols = slice(g * HEAD_DIM, (g + 1) * HEAD_DIM)
        vg = jnp.concatenate([vn[n * CHUNK:(n + 1) * CHUNK, cols] for n in range(nchunks)], axis=1)
        mixed = jnp.dot(ws_ref[g], vg, preferred_element_type=F32)
        for n in range(nchunks):
            rows = slice(n * CHUNK, (n + 1) * CHUNK)
            u = jax.nn.gelu(u_ref[rows, cols].astype(F32))
            o_ref[rows, cols] = (u * (mixed[:, n * CHUNK:(n + 1) * CHUNK] + bias_ref[:, cols])).astype(o_ref.dtype)


def sgu(p2d, ln_g, w_s, b_s, width, tm=512):
    t = p2d.shape[0]
    groups = width // HEAD_DIM
    bias_full = jnp.repeat(b_s.T.astype(F32), HEAD_DIM, axis=1)
    return pl.pallas_call(
        _sgu_kernel,
        grid=(t // tm,),
        in_specs=[pl.BlockSpec((tm, width), lambda i: (i, 0)),
                  pl.BlockSpec((tm, width), lambda i: (i, 1)),
                  pl.BlockSpec((1, width), lambda i: (0, 0)),
                  pl.BlockSpec((groups, CHUNK, CHUNK), lambda i: (0, 0, 0)),
                  pl.BlockSpec((CHUNK, width), lambda i: (0, 0))],
        out_specs=pl.BlockSpec((tm, width), lambda i: (i, 0)),
        out_shape=jax.ShapeDtypeStruct((t, width), BF16),
        compiler_params=_params(("parallel",), 40),
        name="sgu",
    )(p2d, p2d, ln_g.reshape(1, width).astype(F32), w_s.astype(BF16), bias_full)


def _na_block_layout(rows):
    nblk = rows // NA_QROWS
    kh = min(NA_ROWS, rows)
    starts, variants, var_key = [], [], {}
    for qb in range(nblk):
        r0 = qb * NA_QROWS
        ws = int(np.clip(r0 - kh // 2, 0, rows - NA_KROWS))
        rs = np.clip(np.arange(r0, r0 + NA_QROWS) - kh // 2, 0, rows - kh)
        key = (tuple(rs - ws), r0 - ws)
        if key not in var_key:
            var_key[key] = len(var_key)
        starts.append(ws)
        variants.append(var_key[key])
    return starts, variants, list(var_key.keys()), kh


def _na_tiles(rpb, rows):
    nh = rpb.shape[0]
    _, _, keys, kh = _na_block_layout(rows)
    qc = np.arange(GRID_W)[:, None]
    kc = np.arange(GRID_W)[None, :]
    col_start = np.clip(qc - NA_COLS // 2, 0, GRID_W - NA_COLS)
    col_valid = (kc >= col_start) & (kc < col_start + NA_COLS)
    dc_idx = np.clip(kc - qc + NA_COLS - 1, 0, 2 * NA_COLS - 2)
    onehot = (dc_idx[None] == np.arange(2 * NA_COLS - 1)[:, None, None]).astype(np.float32)
    expanded = jnp.einsum('hrc,cqk->hrqk', rpb.astype(F32) * LOG2E, jnp.asarray(onehot),
                          precision=lax.Precision.HIGHEST)
    tiles = jnp.where(jnp.asarray(col_valid)[None, None], expanded, NEG_INF)
    tiles = jnp.concatenate([tiles, jnp.full((nh, 1, GRID_W, GRID_W), NEG_INF, F32)], axis=1)
    invalid = 2 * NA_ROWS - 1
    dr = np.full((len(keys), NA_QROWS, NA_KROWS), invalid, np.int32)
    for v, (rs_rel, r0_rel) in enumerate(keys):
        for i in range(NA_QROWS):
            for j in range(NA_KROWS):
                if rs_rel[i] <= j < rs_rel[i] + kh:
                    dr[v, i, j] = j - (r0_rel + i) + NA_ROWS - 1
    return tiles, dr


def _na_kernel(q_ref, k_ref, v_ref, tiles_ref, o_ref, bias_scr, *, starts, variants, tile_idx, scale):
    qblk = NA_QROWS * GRID_W
    kblk = NA_KROWS * GRID_W

    @pl.when(pl.program_id(1) == 0)
    def _():
        for var in range(tile_idx.shape[0]):
            for i in range(NA_QROWS):
                strip = jnp.concatenate([tiles_ref[int(tile_idx[var, i, j])] for j in range(NA_KROWS)], axis=1)
                bias_scr[var, i * GRID_W:(i + 1) * GRID_W, :] = strip

    for qb, (ws, var) in enumerate(zip(starts, variants)):
        q = (q_ref[qb * qblk:(qb + 1) * qblk, :].astype(F32) * scale).astype(BF16)
        k = k_ref[ws * GRID_W:ws * GRID_W + kblk, :]
        v = v_ref[ws * GRID_W:ws * GRID_W + kblk, :]
        s = lax.dot_general(q, k, (((1,), (1,)), ((), ())), preferred_element_type=F32)
        s = s + bias_scr[var]
        m = jnp.max(s, axis=-1, keepdims=True)
        e = jnp.exp2(s - m).astype(BF16)
        ol = jnp.dot(e, jnp.concatenate([v, jnp.ones_like(v)], axis=1), preferred_element_type=F32)
        o_ref[qb * qblk:(qb + 1) * qblk, :] = (ol[:, :HEAD_DIM] / ol[:, HEAD_DIM:]).astype(o_ref.dtype)


def neighbourhood_attention(p3, rpb, col0, nh):
    bsz, seq, _ = p3.shape
    rows = seq // GRID_W
    starts, variants, keys, _ = _na_block_layout(rows)
    tiles, tile_idx = _na_tiles(rpb, rows)
    nvar = len(keys)
    ntiles = tiles.shape[1]
    qblk, kblk = NA_QROWS * GRID_W, NA_KROWS * GRID_W
    kern = functools.partial(_na_kernel, starts=starts, variants=variants, tile_idx=tile_idx,
                             scale=HEAD_DIM ** -0.5 * LOG2E)
    head_spec = lambda off: pl.BlockSpec((None, seq, HEAD_DIM), lambda h, b: (b, 0, col0 + off + h))
    return pl.pallas_call(
        kern,
        grid=(nh, bsz),
        in_specs=[head_spec(0), head_spec(nh), head_spec(2 * nh),
                  pl.BlockSpec((None, ntiles, GRID_W, GRID_W), lambda h, b: (h, 0, 0, 0))],
        out_specs=pl.BlockSpec((None, seq, HEAD_DIM), lambda h, b: (b, 0, h)),
        out_shape=jax.ShapeDtypeStruct((bsz, seq, nh * HEAD_DIM), BF16),
        scratch_shapes=[pltpu.VMEM((nvar, qblk, kblk), F32)],
        compiler_params=_params(("parallel", "arbitrary"), 32),
        name="neighbourhood_attention",
    )(p3, p3, p3, tiles)


def _dil_windows(length):
    nk = min(2 * DIL_QBLK, length)
    out = []
    for i in range(length // DIL_QBLK):
        q0 = i * DIL_QBLK
        k0 = int(np.clip(q0 - DIL_QBLK // 2, 0, length - nk))
        out.append((q0, k0, nk))
    return out


def _dil_tables(seq, nh):
    slopes = np.array([2.0 ** (-8.0 * (h + 1) / nh) for h in range(nh)], dtype=np.float32)
    var_index, absd_tabs, dil_of = {}, [], []
    plan = []
    for window, dil in DIL_PATTERNS:
        radius = window // (2 * dil)
        length = seq // dil
        blocks = []
        for q0, k0, nk in _dil_windows(length):
            key = (dil, radius, k0 - q0, nk)
            if key not in var_index:
                var_index[key] = len(absd_tabs)
                delta = (k0 + np.arange(2 * DIL_QBLK)[None, :]) - (q0 + np.arange(DIL_QBLK)[:, None])
                absd = np.abs(delta).astype(np.float32)
                valid = (np.abs(delta) <= radius) & (np.arange(2 * DIL_QBLK)[None, :] < nk)
                absd_tabs.append((absd * dil, valid))
            blocks.append((q0, k0, nk, var_index[key]))
        plan.append((dil, length, blocks))
    pen = np.stack([a for a, _ in absd_tabs])
    valid = np.stack([v for _, v in absd_tabs])
    bias = np.where(valid[None], -(slopes[:, None, None, None] * pen[None]).astype(np.float32), np.float32(NEG_INF))
    return plan, bias.astype(np.float32)


def _merge_partial(a, b):
    (oa, ma, la), (ob, mb, lb) = a, b
    a_is_max = ma >= mb
    t = jnp.exp2(jnp.minimum(ma, mb) - jnp.maximum(ma, mb))
    ta = jnp.where(a_is_max, 1.0, t)
    tb = jnp.where(a_is_max, t, 1.0)
    return ta * oa + tb * ob, jnp.maximum(ma, mb), ta * la + tb * lb


def _dil_kernel(q_ref, k_ref, v_ref, bias_ref, o_ref, qf, kf, vf, qd, kd, vd, p1, pd, pc, pn, *, plan, scale):
    (_, _, blocks1), (d, len_d, blocks_d), (_, len_c, blocks_c) = plan
    qf[...] = q_ref[...].astype(F32) * scale
    kf[...] = k_ref[...].astype(F32)
    vf[...] = v_ref[...].astype(F32)

    def attend(q, k, v, var, nk):
        s = lax.dot_general(q, k, (((1,), (1,)), ((), ())), preferred_element_type=F32)
        s = s + bias_ref[var][:, :nk]
        m = jnp.max(s, axis=-1, keepdims=True)
        e = jnp.exp2(s - m).astype(BF16)
        ol = jnp.dot(e, jnp.concatenate([v, jnp.ones_like(v)], axis=1), preferred_element_type=F32)
        o = ol[:, :HEAD_DIM]
        return o, jnp.broadcast_to(m, o.shape), ol[:, HEAD_DIM:]

    for q0, k0, nk, var in blocks1:
        part = attend(qf[q0:q0 + DIL_QBLK, :].astype(BF16), kf[k0:k0 + nk, :].astype(BF16),
                      vf[k0:k0 + nk, :].astype(BF16), var, nk)
        for j in range(3):
            p1[j, q0:q0 + DIL_QBLK, :] = part[j]

    for r in range(d):
        qd[r] = qf[pl.ds(r, len_d, stride=d), :]
        kd[r] = kf[pl.ds(r, len_d, stride=d), :]
        vd[r] = vf[pl.ds(r, len_d, stride=d), :]

    for r in range(d):
        qs, ks, vs = qd[r].astype(BF16), kd[r].astype(BF16), vd[r].astype(BF16)
        for q0, k0, nk, var in blocks_d:
            part = attend(qs[q0:q0 + DIL_QBLK], ks[k0:k0 + nk], vs[k0:k0 + nk], var, nk)
            for j in range(3):
                pd[j, r, q0:q0 + DIL_QBLK, :] = part[j]
        for a in range(d):
            qs = qd[r, pl.ds(a, len_c, stride=d), :].astype(BF16)
            ks = kd[r, pl.ds(a, len_c, stride=d), :].astype(BF16)
            vs = vd[r, pl.ds(a, len_c, stride=d), :].astype(BF16)
            for q0, k0, nk, var in blocks_c:
                part = attend(qs[q0:q0 + DIL_QBLK], ks[k0:k0 + nk], vs[k0:k0 + nk], var, nk)
                for j in range(3):
                    pc[j, r, pl.ds(q0 * d + a, DIL_QBLK, stride=d), :] = part[j]

    for r in range(d):
        part = _merge_partial(tuple(pd[j, r] for j in range(3)), tuple(pc[j, r] for j in range(3)))
        for j in range(3):
            pn[j, pl.ds(r, len_d, stride=d), :] = part[j]

    o, _, l = _merge_partial(tuple(p1[j] for j in range(3)), tuple(pn[j] for j in range(3)))
    o_ref[...] = (o / l).astype(o_ref.dtype)


def dilated_attention(p3, col0, nh):
    bsz, seq, _ = p3.shape
    plan, bias_np = _dil_tables(seq, nh)
    dils = [dil for dil, _, _ in plan]
    assert len(dils) == 3 and dils[0] == 1 and dils[2] == dils[1] * dils[1], dils
    d = dils[1]
    nvar = bias_np.shape[1]
    kern = functools.partial(_dil_kernel, plan=plan, scale=HEAD_DIM ** -0.5 * LOG2E)
    head_spec = lambda off: pl.BlockSpec((None, seq, HEAD_DIM), lambda h, b: (b, 0, col0 + off + h))
    natural = pltpu.VMEM((seq, HEAD_DIM), F32)
    by_residue = pltpu.VMEM((d, seq // d, HEAD_DIM), F32)
    partial_natural = pltpu.VMEM((3, seq, HEAD_DIM), F32)
    partial_by_residue = pltpu.VMEM((3, d, seq // d, HEAD_DIM), F32)
    return pl.pallas_call(
        kern,
        grid=(nh, bsz),
        in_specs=[head_spec(0), head_spec(nh), head_spec(2 * nh),
                  pl.BlockSpec((None, nvar, DIL_QBLK, 2 * DIL_QBLK), lambda h, b: (h, 0, 0, 0))],
        out_specs=pl.BlockSpec((None, seq, HEAD_DIM), lambda h, b: (b, 0, h)),
        out_shape=jax.ShapeDtypeStruct((bsz, seq, nh * HEAD_DIM), BF16),
        scratch_shapes=[natural] * 3 + [by_residue] * 3 + [partial_natural, partial_by_residue,
                                                            partial_by_residue, partial_natural],
        compiler_params=_params(("parallel", "parallel"), 40),
        name="dilated_attention",
    )(p3, p3, p3, jnp.asarray(bias_np * np.float32(LOG2E)))


def _residual_norm_router(acc, x_ref, g_ref, r_ref, xo_ref, h_ref, lg_ref):
    ne = lg_ref.shape[1]
    for c in range(acc.shape[0] // EPILOGUE_ROWS):
        rows = slice(c * EPILOGUE_ROWS, (c + 1) * EPILOGUE_ROWS)
        xn = x_ref[rows, :] + acc[rows, :]
        xo_ref[rows, :] = xn
        ms = jnp.mean(xn * xn, axis=-1, keepdims=True)
        h = xn * lax.rsqrt(ms + RMS_EPS) * g_ref[...]
        h_hi = h.astype(BF16)
        h_ref[rows, :] = h_hi
        s = jnp.dot(h_hi, r_ref[...], preferred_element_type=F32)
        lg_ref[rows, :] = s[:, :ne] + s[:, ne:]


def _out_proj_router_kernel(ya_ref, yb_ref, wa_ref, wb_ref, x_ref, g_ref, r_ref, xo_ref, h_ref, lg_ref):
    acc = jnp.dot(ya_ref[...], wa_ref[...], preferred_element_type=F32)
    acc = acc + jnp.dot(yb_ref[...], wb_ref[...], preferred_element_type=F32)
    _residual_norm_router(acc, x_ref, g_ref, r_ref, xo_ref, h_ref, lg_ref)


def _conv_out_proj_router_kernel(b_ref, c_ref, xin_ref, cp_ref, xp_ref, cn_ref, xn_ref, taps_ref, yb_ref,
                                 wa_ref, wb_ref, x_ref, g_ref, r_ref, xo_ref, h_ref, lg_ref, *, tiles_per_seq):
    tm = c_ref.shape[0]
    pos = lax.rem(pl.program_id(0), tiles_per_seq)
    acc = jnp.dot(yb_ref[...], wb_ref[...], preferred_element_type=F32)
    z = c_ref[...].astype(F32) * xin_ref[...].astype(F32)
    z_before = (cp_ref[...].astype(F32) * xp_ref[...].astype(F32))[CONV_HALO - 1:CONV_HALO, :]
    z_after = (cn_ref[...].astype(F32) * xn_ref[...].astype(F32))[0:1, :]
    z_before = jnp.where(pos == 0, 0.0, z_before)
    z_after = jnp.where(pos == tiles_per_seq - 1, 0.0, z_after)
    row = lax.broadcasted_iota(jnp.int32, z.shape, 0)
    z_prev = jnp.where(row == 0, z_before, pltpu.roll(z, 1, axis=0))
    z_next = jnp.where(row == tm - 1, z_after, pltpu.roll(z, tm - 1, axis=0))
    y = taps_ref[0:1, :] * z_prev + taps_ref[1:2, :] * z + taps_ref[2:3, :] * z_next
    ya = (b_ref[...].astype(F32) * y).astype(BF16)
    acc = acc + jnp.dot(ya, wa_ref[...], preferred_element_type=F32)
    _residual_norm_router(acc, x_ref, g_ref, r_ref, xo_ref, h_ref, lg_ref)


def _router_split(router):
    r_hi = router.astype(BF16)
    r_lo = (router - r_hi.astype(F32)).astype(BF16)
    return jnp.concatenate([r_hi, r_lo], axis=1)


def conv_out_proj_norm_router(p2d, taps, yb, w_bf16, x2d, g, router, seq, tm=512):
    t, d = x2d.shape
    width = yb.shape[1]
    ne = router.shape[1]
    assert seq % tm == 0 and tm % CONV_HALO == 0
    halo_per_tile = tm // CONV_HALO
    last_halo = t // CONV_HALO - 1
    tile = lambda col: pl.BlockSpec((tm, width), lambda i: (i, col))
    before = lambda col: pl.BlockSpec((CONV_HALO, width), lambda i: (jnp.maximum(i * halo_per_tile - 1, 0), col))
    after = lambda col: pl.BlockSpec((CONV_HALO, width),
                                     lambda i: (jnp.minimum((i + 1) * halo_per_tile, last_halo), col))
    return pl.pallas_call(
        functools.partial(_conv_out_proj_router_kernel, tiles_per_seq=seq // tm),
        grid=(t // tm,),
        in_specs=[tile(0), tile(1), tile(2), before(1), before(2), after(1), after(2),
                  pl.BlockSpec((CONV_W, width), lambda i: (0, 0)),
                  pl.BlockSpec((tm, width), lambda i: (i, 0)),
                  pl.BlockSpec((width, d), lambda i: (0, 0)),
                  pl.BlockSpec((width, d), lambda i: (1, 0)),
                  pl.BlockSpec((tm, d), lambda i: (i, 0)),
                  pl.BlockSpec((1, d), lambda i: (0, 0)),
                  pl.BlockSpec((d, 2 * ne), lambda i: (0, 0))],
        out_specs=[pl.BlockSpec((tm, d), lambda i: (i, 0)),
                   pl.BlockSpec((tm, d), lambda i: (i, 0)),
                   pl.BlockSpec((tm, ne), lambda i: (i, 0))],
        out_shape=[jax.ShapeDtypeStruct((t, d), F32),
                   jax.ShapeDtypeStruct((t, d), BF16),
                   jax.ShapeDtypeStruct((t, ne), F32)],
        compiler_params=_params(("parallel",), 56),
        name="conv_out_proj_norm_router",
    )(p2d, p2d, p2d, p2d, p2d, p2d, p2d, taps.astype(F32), yb, w_bf16, w_bf16, x2d, g.reshape(1, d),
      _router_split(router))


def out_proj_norm_router(ya, yb, w_bf16, x2d, g, router, tm=512):
    t, d = x2d.shape
    ka = ya.shape[1]
    kb = yb.shape[1]
    assert ka == kb
    ne = router.shape[1]
    r_split = _router_split(router)
    return pl.pallas_call(
        _out_proj_router_kernel,
        grid=(t // tm,),
        in_specs=[pl.BlockSpec((tm, ka), lambda i: (i, 0)),
                  pl.BlockSpec((tm, kb), lambda i: (i, 0)),
                  pl.BlockSpec((ka, d), lambda i: (0, 0)),
                  pl.BlockSpec((kb, d), lambda i: (1, 0)),
                  pl.BlockSpec((tm, d), lambda i: (i, 0)),
                  pl.BlockSpec((1, d), lambda i: (0, 0)),
                  pl.BlockSpec((d, 2 * ne), lambda i: (0, 0))],
        out_specs=[pl.BlockSpec((tm, d), lambda i: (i, 0)),
                   pl.BlockSpec((tm, d), lambda i: (i, 0)),
                   pl.BlockSpec((tm, ne), lambda i: (i, 0))],
        out_shape=[jax.ShapeDtypeStruct((t, d), F32),
                   jax.ShapeDtypeStruct((t, d), BF16),
                   jax.ShapeDtypeStruct((t, ne), F32)],
        compiler_params=_params(("parallel",), 56),
        name="out_proj_norm_router",
    )(ya, yb, w_bf16, w_bf16, x2d, g.reshape(1, d), r_split)


def _route_kernel(lg_ref, slot_ref, aff_ref, *, cap):
    nb, ne, seq = aff_ref.shape
    logits = lg_ref[...].reshape(nb, ne, seq)
    mx = jnp.max(logits, axis=1, keepdims=True)
    ex = jnp.exp(logits - mx)
    a3 = ex / jnp.sum(ex, axis=1, keepdims=True)
    aff_ref[...] = a3
    a = a3.reshape(nb * ne, seq)
    nrows = nb * ne
    lanes = 128
    r = lax.broadcasted_iota(jnp.int32, (lanes, lanes), 0)
    c = lax.broadcasted_iota(jnp.int32, (lanes, lanes), 1)
    tri = jnp.where(r <= c, 1.0, 0.0).astype(BF16)

    def count(mask_f):
        return jnp.sum(mask_f, axis=1, keepdims=True)

    def prefix(x):
        parts = []
        run = jnp.zeros((nrows, 1), F32)
        for j in range(seq // lanes):
            blk = x[:, j * lanes:(j + 1) * lanes]
            inc = jnp.dot(blk.astype(BF16), tri, preferred_element_type=F32)
            parts.append(inc - blk + run)
            run = run + count(blk)
        return jnp.concatenate(parts, axis=1)

    def body(i, ans):
        cand = ans | jnp.left_shift(jnp.int32(1), 30 - i)
        cnt = count(jnp.where(a >= lax.bitcast_convert_type(cand, F32), 1.0, 0.0))
        return jnp.where(cnt >= cap, cand, ans)

    ans = lax.fori_loop(0, 31, body, jnp.zeros((nrows, 1), jnp.int32))
    thr = lax.bitcast_convert_type(ans, F32)
    gt = jnp.where(a > thr, 1.0, 0.0)
    eq = jnp.where(a >= thr, 1.0, 0.0) - gt
    need = cap - count(gt)
    sel = gt + eq * jnp.where(prefix(eq) < need, 1.0, 0.0)
    slot_ref[...] = jnp.where(sel > 0.5, prefix(sel), -1.0).astype(jnp.int32).reshape(nb, ne, seq)


def route(logits_rows, bsz, ne, seq, cap):
    return pl.pallas_call(
        functools.partial(_route_kernel, cap=cap),
        grid=(1,),
        in_specs=[pl.BlockSpec((bsz * ne, seq), lambda i: (0, 0))],
        out_specs=[pl.BlockSpec((bsz, ne, seq), lambda i: (0, 0, 0)),
                   pl.BlockSpec((bsz, ne, seq), lambda i: (0, 0, 0))],
        out_shape=[jax.ShapeDtypeStruct((bsz, ne, seq), jnp.int32),
                   jax.ShapeDtypeStruct((bsz, ne, seq), F32)],
        compiler_params=_params(("arbitrary",), 40),
        name="route",
    )(logits_rows)


def _gather_kernel(slot_ref, aff_ref, h_ref, xe_ref, gate_ref, *, cap, ncol):
    eg, seq = slot_ref.shape
    d = h_ref.shape[1]
    cidx = lax.broadcasted_iota(jnp.int32, (cap, seq), 0)
    rows = []
    for j in range(eg):
        hit = cidx == slot_ref[j:j + 1, :]
        rows.append(jnp.where(hit, 1.0, 0.0).astype(BF16))
        gate_ref[j] = jnp.sum(jnp.where(hit, aff_ref[j:j + 1, :], 0.0), axis=1, keepdims=True)
    p = jnp.concatenate(rows, axis=0)
    dc = d // ncol
    for c in range(ncol):
        xe = jnp.dot(p, h_ref[:, c * dc:(c + 1) * dc], preferred_element_type=F32)
        xe_ref[:, :, c * dc:(c + 1) * dc] = xe.reshape(eg, cap, dc).astype(xe_ref.dtype)


def gather_rows(slot, aff, h, cap, egroup=4, ncol=2):
    bsz, ne, seq = slot.shape
    d = h.shape[1]
    ngroups = ne // egroup
    grouped = lambda a: a.reshape(bsz, ngroups, egroup, seq)
    return pl.pallas_call(
        functools.partial(_gather_kernel, cap=cap, ncol=ncol),
        grid=(bsz, ngroups),
        in_specs=[pl.BlockSpec((None, None, egroup, seq), lambda b, g: (b, g, 0, 0)),
                  pl.BlockSpec((None, None, egroup, seq), lambda b, g: (b, g, 0, 0)),
                  pl.BlockSpec((seq, d), lambda b, g: (b, 0))],
        out_specs=[pl.BlockSpec((egroup, None, cap, d), lambda b, g: (g, b, 0, 0)),
                   pl.BlockSpec((egroup, None, cap, 1), lambda b, g: (g, b, 0, 0))],
        out_shape=[jax.ShapeDtypeStruct((ne, bsz, cap, d), BF16),
                   jax.ShapeDtypeStruct((ne, bsz, cap, 1), F32)],
        compiler_params=_params(("parallel", "parallel"), 48),
        name="moe_gather",
    )(grouped(slot), grouped(aff), h)


def _combine_kernel(slot_t_ref, ye_ref, x_ref, o_ref, *, cap):
    tb, ne = slot_t_ref.shape
    tn = x_ref.shape[1]
    cidx = lax.broadcasted_iota(jnp.int32, (tb, cap), 1)
    pt = jnp.concatenate([jnp.where(cidx == slot_t_ref[:, e:e + 1], 1.0, 0.0).astype(BF16) for e in range(ne)],
                         axis=1)
    ye = ye_ref[...].reshape(ne * cap, tn)
    o_ref[...] = x_ref[...] + jnp.dot(pt, ye, preferred_element_type=F32)


def combine(slot_t, ye, x2d, cap, tn=1024, tb=1024):
    bsz, seq, ne = slot_t.shape
    d = x2d.shape[1]
    nt = seq // tb
    return pl.pallas_call(
        functools.partial(_combine_kernel, cap=cap),
        grid=(bsz, d // tn, nt),
        in_specs=[pl.BlockSpec((None, tb, ne), lambda b, n, t: (b, t, 0)),
                  pl.BlockSpec((ne, None, cap, tn), lambda b, n, t: (0, b, 0, n)),
                  pl.BlockSpec((tb, tn), lambda b, n, t: (b * nt + t, n))],
        out_specs=pl.BlockSpec((tb, tn), lambda b, n, t: (b * nt + t, n)),
        out_shape=jax.ShapeDtypeStruct(x2d.shape, F32),
        compiler_params=_params(("parallel", "parallel", "parallel"), 48),
        name="moe_combine",
    )(slot_t, ye, x2d)


def _combine_norm_kernel(slot_t_ref, ye_ref, x_ref, g_ref, o_ref, *, cap):
    tb, ne = slot_t_ref.shape
    d = x_ref.shape[1]
    cidx = lax.broadcasted_iota(jnp.int32, (tb, cap), 1)
    pt = jnp.concatenate([jnp.where(cidx == slot_t_ref[:, e:e + 1], 1.0, 0.0).astype(BF16) for e in range(ne)],
                         axis=1)
    acc = jnp.dot(pt, ye_ref[...].reshape(ne * cap, d), preferred_element_type=F32)
    for c in range(tb // EPILOGUE_ROWS):
        rows = slice(c * EPILOGUE_ROWS, (c + 1) * EPILOGUE_ROWS)
        xn = x_ref[rows, :] + acc[rows, :]
        ms = jnp.mean(xn * xn, axis=-1, keepdims=True)
        o_ref[rows, :] = xn * lax.rsqrt(ms + RMS_EPS) * g_ref[...]


def combine_final_norm(slot_t, ye, x2d, g, cap, tb=512):
    bsz, seq, ne = slot_t.shape
    d = x2d.shape[1]
    nt = seq // tb
    return pl.pallas_call(
        functools.partial(_combine_norm_kernel, cap=cap),
        grid=(bsz, nt),
        in_specs=[pl.BlockSpec((None, tb, ne), lambda b, t: (b, t, 0)),
                  pl.BlockSpec((ne, None, cap, d), lambda b, t: (0, b, 0, 0), pipeline_mode=pl.Buffered(1)),
                  pl.BlockSpec((tb, d), lambda b, t: (b * nt + t, 0)),
                  pl.BlockSpec((1, d), lambda b, t: (0, 0))],
        out_specs=pl.BlockSpec((tb, d), lambda b, t: (b * nt + t, 0)),
        out_shape=jax.ShapeDtypeStruct(x2d.shape, F32),
        compiler_params=_params(("parallel", "arbitrary"), 56),
        name="moe_combine_final_norm",
    )(slot_t, ye, x2d, g.reshape(1, d))


def _expert_ffn_kernel(xe_ref, wg_ref, wu_ref, wd_ref, gate_ref, o_ref, acc_ref):
    f = pl.program_id(2)
    last = pl.num_programs(2) - 1
    rows_per = xe_ref.shape[0] // FFN_ROW_CHUNKS

    def step(first, final):
        wg = wg_ref[...].astype(BF16)
        wu = wu_ref[...].astype(BF16)
        wd = wd_ref[...].astype(BF16)
        for c in range(FFN_ROW_CHUNKS):
            rows = slice(c * rows_per, (c + 1) * rows_per)
            xe = xe_ref[rows, :]
            g = jnp.dot(xe, wg, preferred_element_type=F32)
            u = jnp.dot(xe, wu, preferred_element_type=F32)
            hid = (jax.nn.silu(g) * u).astype(BF16)
            part = jnp.dot(hid, wd, preferred_element_type=F32)
            total = part if first else acc_ref[rows, :] + part
            if final:
                o_ref[rows, :] = (total * gate_ref[rows, :]).astype(o_ref.dtype)
            else:
                acc_ref[rows, :] = total

    pl.when(f == 0)(lambda: step(True, False))
    pl.when(jnp.logical_and(f > 0, f < last))(lambda: step(False, False))
    pl.when(f == last)(lambda: step(False, True))


def expert_ffn(xe, w_gate, w_up, w_down, layer, gate, tm=1024, tf=256):
    ne, r, d = xe.shape
    fdim = w_gate.shape[3]
    assert fdim // tf >= 2, "the kernel distinguishes first / middle / last hidden tiles"
    return pl.pallas_call(
        _expert_ffn_kernel,
        grid=(ne, r // tm, fdim // tf),
        in_specs=[pl.BlockSpec((None, tm, d), lambda e, m, f: (e, m, 0)),
                  pl.BlockSpec((None, None, d, tf), lambda e, m, f: (layer, e, 0, f)),
                  pl.BlockSpec((None, None, d, tf), lambda e, m, f: (layer, e, 0, f)),
                  pl.BlockSpec((None, None, tf, d), lambda e, m, f: (layer, e, f, 0)),
                  pl.BlockSpec((None, tm, 1), lambda e, m, f: (e, m, 0))],
        out_specs=pl.BlockSpec((None, tm, d), lambda e, m, f: (e, m, 0)),
        out_shape=jax.ShapeDtypeStruct((ne, r, d), BF16),
        scratch_shapes=[pltpu.VMEM((tm, d), F32)],
        compiler_params=_params(("parallel", "parallel", "arbitrary"), 56),
        name="expert_ffn",
    )(xe, w_gate, w_up, w_down, gate)


def expert_choice_moe(x2d, h, logits, bsz, seq, layer, w_gate, w_up, w_down, final_gain=None):
    d = x2d.shape[1]
    ne = logits.shape[1]
    cap = EC_CAPACITY_FACTOR * seq // ne
    logits_rows = logits.reshape(bsz, seq, ne).transpose(0, 2, 1).reshape(bsz * ne, seq)
    slot, aff = route(logits_rows, bsz, ne, seq, cap)
    xe, gate = gather_rows(slot, aff, h, cap)
    ye = expert_ffn(xe.reshape(ne, bsz * cap, d), w_gate, w_up, w_down, layer,
                    gate.reshape(ne, bsz * cap, 1))
    slot_t, ye = slot.transpose(0, 2, 1), ye.reshape(ne, bsz, cap, d)
    if final_gain is None:
        return combine(slot_t, ye, x2d, cap)
    return combine_final_norm(slot_t, ye, x2d, final_gain, cap)


def kernel(x, norm_mix, norm_ffn, norm_final, w_in_ab, a_v_norm, a_spatial_w, a_spatial_b, b_rpb, w_out_ab,
           w_in_cd, c_conv, w_out_cd, router, w_gate, w_up, w_down):
    bsz, seq, d = x.shape
    depth = norm_mix.shape[0]
    half = d // 2
    nh = half // HEAD_DIM
    x2d = x.reshape(bsz * seq, d)
    for layer in range(depth):
        i = layer // 2
        if layer % 2 == 0:
            p = rms_matmul(x2d, norm_mix[layer], w_in_ab[i])
            p3 = p.reshape(bsz, seq, -1)
            ya = sgu(p, a_v_norm[i], a_spatial_w[i], a_spatial_b[i], half)
            yb = neighbourhood_attention(p3, b_rpb[i], 2 * half // HEAD_DIM, nh).reshape(bsz * seq, half)
            x2d, h, logits = out_proj_norm_router(ya, yb, w_out_ab[i].astype(BF16), x2d,
                                                  norm_ffn[layer], router[layer])
        else:
            p = rms_matmul(x2d, norm_mix[layer], w_in_cd[i])
            p3 = p.reshape(bsz, seq, -1)
            yd = dilated_attention(p3, 3 * half // HEAD_DIM, nh).reshape(bsz * seq, half)
            x2d, h, logits = conv_out_proj_norm_router(p, c_conv[i], yd, w_out_cd[i].astype(BF16), x2d,
                                                       norm_ffn[layer], router[layer], seq)
        x2d = expert_choice_moe(x2d, h, logits, bsz, seq, layer, w_gate, w_up, w_down,
                                final_gain=norm_final if layer == depth - 1 else None)
    return x2d.reshape(bsz, seq, d)
```

```python
import functools

import numpy as np
import jax
import jax.numpy as jnp
from jax import lax
from jax.experimental import pallas as pl
from jax.experimental.pallas import tpu as pltpu

F32 = jnp.float32
BF16 = jnp.bfloat16

HEAD_DIM = 128
CHUNK = 128
GRID_W = 64
NA_ROWS = 8
NA_COLS = 16
NA_QROWS = 4
NA_KROWS = NA_QROWS + NA_ROWS
CONV_W = 3
CONV_HALO = 16
DIL_PATTERNS = ((128, 1), (512, 4), (2048, 16))
DIL_QBLK = 128
IN_PROJ_NORM_CHUNKS = 4
EPILOGUE_ROWS = 128
FFN_ROW_CHUNKS = 2
EC_CAPACITY_FACTOR = 2
RMS_EPS = 1e-6
LN_EPS = 1e-5
NEG_INF = -1e30
LOG2E = 1.4426950408889634
MIB = 1024 * 1024


def _params(semantics, vmem_mib):
    return pltpu.CompilerParams(dimension_semantics=semantics,
                                vmem_limit_bytes=int(vmem_mib * MIB))


def _rms_matmul_kernel(x_ref, g_ref, w_ref, o_ref, h_scr):
    j = pl.program_id(1)

    def first_column_tile():
        w = w_ref[...].astype(BF16)
        rows_per = x_ref.shape[0] // IN_PROJ_NORM_CHUNKS
        for c in range(IN_PROJ_NORM_CHUNKS):
            rows = slice(c * rows_per, (c + 1) * rows_per)
            x = x_ref[rows, :]
            ms = jnp.mean(x * x, axis=-1, keepdims=True)
            h = (x * lax.rsqrt(ms + RMS_EPS) * g_ref[...]).astype(BF16)
            h_scr[rows, :] = h
            o_ref[rows, :] = jnp.dot(h, w, preferred_element_type=F32).astype(o_ref.dtype)

    def later_column_tile():
        o_ref[...] = jnp.dot(h_scr[...], w_ref[...].astype(BF16), preferred_element_type=F32).astype(o_ref.dtype)

    pl.when(j == 0)(first_column_tile)
    pl.when(j > 0)(later_column_tile)


def rms_matmul(x2d, g, w, tm=1024, tn=1024):
    t, d = x2d.shape
    n = w.shape[1]
    return pl.pallas_call(
        _rms_matmul_kernel,
        grid=(t // tm, n // tn),
        in_specs=[pl.BlockSpec((tm, d), lambda i, j: (i, 0)),
                  pl.BlockSpec((1, d), lambda i, j: (0, 0)),
                  pl.BlockSpec((d, tn), lambda i, j: (0, j))],
        out_specs=pl.BlockSpec((tm, tn), lambda i, j: (i, j)),
        out_shape=jax.ShapeDtypeStruct((t, n), BF16),
        scratch_shapes=[pltpu.VMEM((tm, d), BF16)],
        compiler_params=_params(("parallel", "arbitrary"), 56),
        name="rms_in_proj",
    )(x2d, g.reshape(1, d), w)


def _sgu_kernel(u_ref, v_ref, lng_ref, ws_ref, bias_ref, o_ref):
    tm, width = u_ref.shape
    groups = width // HEAD_DIM
    nchunks = tm // CHUNK
    vf = jax.nn.gelu(v_ref[...].astype(F32))
    mu = jnp.mean(vf, axis=-1, keepdims=True)
    dv = vf - mu
    var = jnp.mean(dv * dv, axis=-1, keepdims=True)
    vn = (dv * lax.rsqrt(var + LN_EPS) * lng_ref[...]).astype(BF16)
    for g in range(groups):
        cols = slice(g * HEAD_DIM, (g + 1) * HEAD_DIM)
        vg = jnp.concatenate([vn[n * CHUNK:(n + 1) * CHUNK, cols] for n in range(nchunks)], axis=1)
        mixed = jnp.dot(ws_ref[g], vg, preferred_element_type=F32)
        for n in range(nchunks):
            rows = slice(n * CHUNK, (n + 1) * CHUNK)
            u = jax.nn.gelu(u_ref[rows, cols].astype(F32))
            o_ref[rows, cols] = (u * (mixed[:, n * CHUNK:(n + 1) * CHUNK] + bias_ref[:, cols])).astype(o_ref.dtype)


def sgu(p2d, ln_g, w_s, b_s, width, tm=512):
    t = p2d.shape[0]
    groups = width // HEAD_DIM
    bias_full = jnp.repeat(b_s.T.astype(F32), HEAD_DIM, axis=1)
    return pl.pallas_call(
        _sgu_kernel,
        grid=(t // tm,),
        in_specs=[pl.BlockSpec((tm, width), lambda i: (i, 0)),
                  pl.BlockSpec((tm, width), lambda i: (i, 1)),
                  pl.BlockSpec((1, width), lambda i: (0, 0)),
                  pl.BlockSpec((groups, CHUNK, CHUNK), lambda i: (0, 0, 0)),
                  pl.BlockSpec((CHUNK, width), lambda i: (0, 0))],
        out_specs=pl.BlockSpec((tm, width), lambda i: (i, 0)),
        out_shape=jax.ShapeDtypeStruct((t, width), BF16),
        compiler_params=_params(("parallel",), 40),
        name="sgu",
    )(p2d, p2d, ln_g.reshape(1, width).astype(F32), w_s.astype(BF16), bias_full)


def _na_block_layout(rows):
    nblk = rows // NA_QROWS
    kh = min(NA_ROWS, rows)
    starts, variants, var_key = [], [], {}
    for qb in range(nblk):
        r0 = qb * NA_QROWS
        ws = int(np.clip(r0 - kh // 2, 0, rows - NA_KROWS))
        rs = np.clip(np.arange(r0, r0 + NA_QROWS) - kh // 2, 0, rows - kh)
        key = (tuple(rs - ws), r0 - ws)
        if key not in var_key:
            var_key[key] = len(var_key)
        starts.append(ws)
        variants.append(var_key[key])
    return starts, variants, list(var_key.keys()), kh


def _na_tiles(rpb, rows):
    nh = rpb.shape[0]
    _, _, keys, kh = _na_block_layout(rows)
    qc = np.arange(GRID_W)[:, None]
    kc = np.arange(GRID_W)[None, :]
    col_start = np.clip(qc - NA_COLS // 2, 0, GRID_W - NA_COLS)
    col_valid = (kc >= col_start) & (kc < col_start + NA_COLS)
    dc_idx = np.clip(kc - qc + NA_COLS - 1, 0, 2 * NA_COLS - 2)
    onehot = (dc_idx[None] == np.arange(2 * NA_COLS - 1)[:, None, None]).astype(np.float32)
    expanded = jnp.einsum('hrc,cqk->hrqk', rpb.astype(F32) * LOG2E, jnp.asarray(onehot),
                          precision=lax.Precision.HIGHEST)
    tiles = jnp.where(jnp.asarray(col_valid)[None, None], expanded, NEG_INF)
    tiles = jnp.concatenate([tiles, jnp.full((nh, 1, GRID_W, GRID_W), NEG_INF, F32)], axis=1)
    invalid = 2 * NA_ROWS - 1
    dr = np.full((len(keys), NA_QROWS, NA_KROWS), invalid, np.int32)
    for v, (rs_rel, r0_rel) in enumerate(keys):
        for i in range(NA_QROWS):
            for j in range(NA_KROWS):
                if rs_rel[i] <= j < rs_rel[i] + kh:
                    dr[v, i, j] = j - (r0_rel + i) + NA_ROWS - 1
    return tiles, dr


def _na_kernel(q_ref, k_ref, v_ref, tiles_ref, o_ref, bias_scr, *, starts, variants, tile_idx, scale):
    qblk = NA_QROWS * GRID_W
    kblk = NA_KROWS * GRID_W

    @pl.when(pl.program_id(1) == 0)
    def _():
        for var in range(tile_idx.shape[0]):
            for i in range(NA_QROWS):
                strip = jnp.concatenate([tiles_ref[int(tile_idx[var, i, j])] for j in range(NA_KROWS)], axis=1)
                bias_scr[var, i * GRID_W:(i + 1) * GRID_W, :] = strip

    for qb, (ws, var) in enumerate(zip(starts, variants)):
        q = (q_ref[qb * qblk:(qb + 1) * qblk, :].astype(F32) * scale).astype(BF16)
        k = k_ref[ws * GRID_W:ws * GRID_W + kblk, :]
        v = v_ref[ws * GRID_W:ws * GRID_W + kblk, :]
        s = lax.dot_general(q, k, (((1,), (1,)), ((), ())), preferred_element_type=F32)
        s = s + bias_scr[var]
        m = jnp.max(s, axis=-1, keepdims=True)
        e = jnp.exp2(s - m).astype(BF16)
        ol = jnp.dot(e, jnp.concatenate([v, jnp.ones_like(v)], axis=1), preferred_element_type=F32)
        o_ref[qb * qblk:(qb + 1) * qblk, :] = (ol[:, :HEAD_DIM] / ol[:, HEAD_DIM:]).astype(o_ref.dtype)


def neighbourhood_attention(p3, rpb, col0, nh):
    bsz, seq, _ = p3.shape
    rows = seq // GRID_W
    starts, variants, keys, _ = _na_block_layout(rows)
    tiles, tile_idx = _na_tiles(rpb, rows)
    nvar = len(keys)
    ntiles = tiles.shape[1]
    qblk, kblk = NA_QROWS * GRID_W, NA_KROWS * GRID_W
    kern = functools.partial(_na_kernel, starts=starts, variants=variants, tile_idx=tile_idx,
                             scale=HEAD_DIM ** -0.5 * LOG2E)
    head_spec = lambda off: pl.BlockSpec((None, seq, HEAD_DIM), lambda h, b: (b, 0, col0 + off + h))
    return pl.pallas_call(
        kern,
        grid=(nh, bsz),
        in_specs=[head_spec(0), head_spec(nh), head_spec(2 * nh),
                  pl.BlockSpec((None, ntiles, GRID_W, GRID_W), lambda h, b: (h, 0, 0, 0))],
        out_specs=pl.BlockSpec((None, seq, HEAD_DIM), lambda h, b: (b, 0, h)),
        out_shape=jax.ShapeDtypeStruct((bsz, seq, nh * HEAD_DIM), BF16),
        scratch_shapes=[pltpu.VMEM((nvar, qblk, kblk), F32)],
        compiler_params=_params(("parallel", "arbitrary"), 32),
        name="neighbourhood_attention",
    )(p3, p3, p3, tiles)


def _dil_windows(length):
    nk = min(2 * DIL_QBLK, length)
    out = []
    for i in range(length // DIL_QBLK):
        q0 = i * DIL_QBLK
        k0 = int(np.clip(q0 - DIL_QBLK // 2, 0, length - nk))
        out.append((q0, k0, nk))
    return out


def _dil_tables(seq, nh):
    slopes = np.array([2.0 ** (-8.0 * (h + 1) / nh) for h in range(nh)], dtype=np.float32)
    var_index, absd_tabs, dil_of = {}, [], []
    plan = []
    for window, dil in DIL_PATTERNS:
        radius = window // (2 * dil)
        length = seq // dil
        blocks = []
        for q0, k0, nk in _dil_windows(length):
            key = (dil, radius, k0 - q0, nk)
            if key not in var_index:
                var_index[key] = len(absd_tabs)
                delta = (k0 + np.arange(2 * DIL_QBLK)[None, :]) - (q0 + np.arange(DIL_QBLK)[:, None])
                absd = np.abs(delta).astype(np.float32)
                valid = (np.abs(delta) <= radius) & (np.arange(2 * DIL_QBLK)[None, :] < nk)
                absd_tabs.append((absd * dil, valid))
            blocks.append((q0, k0, nk, var_index[key]))
        plan.append((dil, length, blocks))
    pen = np.stack([a for a, _ in absd_tabs])
    valid = np.stack([v for _, v in absd_tabs])
    bias = np.where(valid[None], -(slopes[:, None, None, None] * pen[None]).astype(np.float32), np.float32(NEG_INF))
    return plan, bias.astype(np.float32)


def _merge_partial(a, b):
    (oa, ma, la), (ob, mb, lb) = a, b
    a_is_max = ma >= mb
    t = jnp.exp2(jnp.minimum(ma, mb) - jnp.maximum(ma, mb))
    ta = jnp.where(a_is_max, 1.0, t)
    tb = jnp.where(a_is_max, t, 1.0)
    return ta * oa + tb * ob, jnp.maximum(ma, mb), ta * la + tb * lb


def _dil_kernel(q_ref, k_ref, v_ref, bias_ref, o_ref, qf, kf, vf, qd, kd, vd, p1, pd, pc, pn, *, plan, scale):
    (_, _, blocks1), (d, len_d, blocks_d), (_, len_c, blocks_c) = plan
    qf[...] = q_ref[...].astype(F32) * scale
    kf[...] = k_ref[...].astype(F32)
    vf[...] = v_ref[...].astype(F32)

    def attend(q, k, v, var, nk):
        s = lax.dot_general(q, k, (((1,), (1,)), ((), ())), preferred_element_type=F32)
        s = s + bias_ref[var][:, :nk]
        m = jnp.max(s, axis=-1, keepdims=True)
        e = jnp.exp2(s - m).astype(BF16)
        ol = jnp.dot(e, jnp.concatenate([v, jnp.ones_like(v)], axis=1), preferred_element_type=F32)
        o = ol[:, :HEAD_DIM]
        return o, jnp.broadcast_to(m, o.shape), ol[:, HEAD_DIM:]

    for q0, k0, nk, var in blocks1:
        part = attend(qf[q0:q0 + DIL_QBLK, :].astype(BF16), kf[k0:k0 + nk, :].astype(BF16),
                      vf[k0:k0 + nk, :].astype(BF16), var, nk)
        for j in range(3):
            p1[j, q0:q0 + DIL_QBLK, :] = part[j]

    for r in range(d):
        qd[r] = qf[pl.ds(r, len_d, stride=d), :]
        kd[r] = kf[pl.ds(r, len_d, stride=d), :]
        vd[r] = vf[pl.ds(r, len_d, stride=d), :]

    for r in range(d):
        qs, ks, vs = qd[r].astype(BF16), kd[r].astype(BF16), vd[r].astype(BF16)
        for q0, k0, nk, var in blocks_d:
            part = attend(qs[q0:q0 + DIL_QBLK], ks[k0:k0 + nk], vs[k0:k0 + nk], var, nk)
            for j in range(3):
                pd[j, r, q0:q0 + DIL_QBLK, :] = part[j]
        for a in range(d):
            qs = qd[r, pl.ds(a, len_c, stride=d), :].astype(BF16)
            ks = kd[r, pl.ds(a, len_c, stride=d), :].astype(BF16)
            vs = vd[r, pl.ds(a, len_c, stride=d), :].astype(BF16)
            for q0, k0, nk, var in blocks_c:
                part = attend(qs[q0:q0 + DIL_QBLK], ks[k0:k0 + nk], vs[k0:k0 + nk], var, nk)
                for j in range(3):
                    pc[j, r, pl.ds(q0 * d + a, DIL_QBLK, stride=d), :] = part[j]

    for r in range(d):
        part = _merge_partial(tuple(pd[j, r] for j in range(3)), tuple(pc[j, r] for j in range(3)))
        for j in range(3):
            pn[j, pl.ds(r, len_d, stride=d), :] = part[j]

    o, _, l = _merge_partial(tuple(p1[j] for j in range(3)), tuple(pn[j] for j in range(3)))
    o_ref[...] = (o / l).astype(o_ref.dtype)


def dilated_attention(p3, col0, nh):
    bsz, seq, _ = p3.shape
    plan, bias_np = _dil_tables(seq, nh)
    dils = [dil for dil, _, _ in plan]
    assert len(dils) == 3 and dils[0] == 1 and dils[2] == dils[1] * dils[1], dils
    d = dils[1]
    nvar = bias_np.shape[1]
    kern = functools.partial(_dil_kernel, plan=plan, scale=HEAD_DIM ** -0.5 * LOG2E)
    head_spec = lambda off: pl.BlockSpec((None, seq, HEAD_DIM), lambda h, b: (b, 0, col0 + off + h))
    natural = pltpu.VMEM((seq, HEAD_DIM), F32)
    by_residue = pltpu.VMEM((d, seq // d, HEAD_DIM), F32)
    partial_natural = pltpu.VMEM((3, seq, HEAD_DIM), F32)
    partial_by_residue = pltpu.VMEM((3, d, seq // d, HEAD_DIM), F32)
    return pl.pallas_call(
        kern,
        grid=(nh, bsz),
        in_specs=[head_spec(0), head_spec(nh), head_spec(2 * nh),
                  pl.BlockSpec((None, nvar, DIL_QBLK, 2 * DIL_QBLK), lambda h, b: (h, 0, 0, 0))],
        out_specs=pl.BlockSpec((None, seq, HEAD_DIM), lambda h, b: (b, 0, h)),
        out_shape=jax.ShapeDtypeStruct((bsz, seq, nh * HEAD_DIM), BF16),
        scratch_shapes=[natural] * 3 + [by_residue] * 3 + [partial_natural, partial_by_residue,
                                                            partial_by_residue, partial_natural],
        compiler_params=_params(("parallel", "parallel"), 40),
        name="dilated_attention",
    )(p3, p3, p3, jnp.asarray(bias_np * np.float32(LOG2E)))


def _residual_norm_router(acc, x_ref, g_ref, r_ref, xo_ref, h_ref, lg_ref):
    ne = lg_ref.shape[1]
    for c in range(acc.shape[0] // EPILOGUE_ROWS):
        rows = slice(c * EPILOGUE_ROWS, (c + 1) * EPILOGUE_ROWS)
        xn = x_ref[rows, :] + acc[rows, :]
        xo_ref[rows, :] = xn
        ms = jnp.mean(xn * xn, axis=-1, keepdims=True)
        h = xn * lax.rsqrt(ms + RMS_EPS) * g_ref[...]
        h_hi = h.astype(BF16)
        h_ref[rows, :] = h_hi
        s = jnp.dot(h_hi, r_ref[...], preferred_element_type=F32)
        lg_ref[rows, :] = s[:, :ne] + s[:, ne:]


def _out_proj_router_kernel(ya_ref, yb_ref, wa_ref, wb_ref, x_ref, g_ref, r_ref, xo_ref, h_ref, lg_ref):
    acc = jnp.dot(ya_ref[...], wa_ref[...], preferred_element_type=F32)
    acc = acc + jnp.dot(yb_ref[...], wb_ref[...], preferred_element_type=F32)
    _residual_norm_router(acc, x_ref, g_ref, r_ref, xo_ref, h_ref, lg_ref)


def _conv_out_proj_router_kernel(b_ref, c_ref, xin_ref, cp_ref, xp_ref, cn_ref, xn_ref, taps_ref, yb_ref,
                                 wa_ref, wb_ref, x_ref, g_ref, r_ref, xo_ref, h_ref, lg_ref, *, tiles_per_seq):
    tm = c_ref.shape[0]
    pos = lax.rem(pl.program_id(0), tiles_per_seq)
    acc = jnp.dot(yb_ref[...], wb_ref[...], preferred_element_type=F32)
    z = c_ref[...].astype(F32) * xin_ref[...].astype(F32)
    z_before = (cp_ref[...].astype(F32) * xp_ref[...].astype(F32))[CONV_HALO - 1:CONV_HALO, :]
    z_after = (cn_ref[...].astype(F32) * xn_ref[...].astype(F32))[0:1, :]
    z_before = jnp.where(pos == 0, 0.0, z_before)
    z_after = jnp.where(pos == tiles_per_seq - 1, 0.0, z_after)
    row = lax.broadcasted_iota(jnp.int32, z.shape, 0)
    z_prev = jnp.where(row == 0, z_before, pltpu.roll(z, 1, axis=0))
    z_next = jnp.where(row == tm - 1, z_after, pltpu.roll(z, tm - 1, axis=0))
    y = taps_ref[0:1, :] * z_prev + taps_ref[1:2, :] * z + taps_ref[2:3, :] * z_next
    ya = (b_ref[...].astype(F32) * y).astype(BF16)
    acc = acc + jnp.dot(ya, wa_ref[...], preferred_element_type=F32)
    _residual_norm_router(acc, x_ref, g_ref, r_ref, xo_ref, h_ref, lg_ref)


def _router_split(router):
    r_hi = router.astype(BF16)
    r_lo = (router - r_hi.astype(F32)).astype(BF16)
    return jnp.concatenate([r_hi, r_lo], axis=1)


def conv_out_proj_norm_router(p2d, taps, yb, w_bf16, x2d, g, router, seq, tm=512):
    t, d = x2d.shape
    width = yb.shape[1]
    ne = router.shape[1]
    assert seq % tm == 0 and tm % CONV_HALO == 0
    halo_per_tile = tm // CONV_HALO
    last_halo = t // CONV_HALO - 1
    tile = lambda col: pl.BlockSpec((tm, width), lambda i: (i, col))
    before = lambda col: pl.BlockSpec((CONV_HALO, width), lambda i: (jnp.maximum(i * halo_per_tile - 1, 0), col))
    after = lambda col: pl.BlockSpec((CONV_HALO, width),
                                     lambda i: (jnp.minimum((i + 1) * halo_per_tile, last_halo), col))
    return pl.pallas_call(
        functools.partial(_conv_out_proj_router_kernel, tiles_per_seq=seq // tm),
        grid=(t // tm,),
        in_specs=[tile(0), tile(1), tile(2), before(1), before(2), after(1), after(2),
                  pl.BlockSpec((CONV_W, width), lambda i: (0, 0)),
                  pl.BlockSpec((tm, width), lambda i: (i, 0)),
                  pl.BlockSpec((width, d), lambda i: (0, 0)),
                  pl.BlockSpec((width, d), lambda i: (1, 0)),
                  pl.BlockSpec((tm, d), lambda i: (i, 0)),
                  pl.BlockSpec((1, d), lambda i: (0, 0)),
                  pl.BlockSpec((d, 2 * ne), lambda i: (0, 0))],
        out_specs=[pl.BlockSpec((tm, d), lambda i: (i, 0)),
                   pl.BlockSpec((tm, d), lambda i: (i, 0)),
                   pl.BlockSpec((tm, ne), lambda i: (i, 0))],
        out_shape=[jax.ShapeDtypeStruct((t, d), F32),
                   jax.ShapeDtypeStruct((t, d), BF16),
                   jax.ShapeDtypeStruct((t, ne), F32)],
        compiler_params=_params(("parallel",), 56),
        name="conv_out_proj_norm_router",
    )(p2d, p2d, p2d, p2d, p2d, p2d, p2d, taps.astype(F32), yb, w_bf16, w_bf16, x2d, g.reshape(1, d),
      _router_split(router))


def out_proj_norm_router(ya, yb, w_bf16, x2d, g, router, tm=512):
    t, d = x2d.shape
    ka = ya.shape[1]
    kb = yb.shape[1]
    assert ka == kb
    ne = router.shape[1]
    r_split = _router_split(router)
    return pl.pallas_call(
        _out_proj_router_kernel,
        grid=(t // tm,),
        in_specs=[pl.BlockSpec((tm, ka), lambda i: (i, 0)),
                  pl.BlockSpec((tm, kb), lambda i: (i, 0)),
                  pl.BlockSpec((ka, d), lambda i: (0, 0)),
                  pl.BlockSpec((kb, d), lambda i: (1, 0)),
                  pl.BlockSpec((tm, d), lambda i: (i, 0)),
                  pl.BlockSpec((1, d), lambda i: (0, 0)),
                  pl.BlockSpec((d, 2 * ne), lambda i: (0, 0))],
        out_specs=[pl.BlockSpec((tm, d), lambda i: (i, 0)),
                   pl.BlockSpec((tm, d), lambda i: (i, 0)),
                   pl.BlockSpec((tm, ne), lambda i: (i, 0))],
        out_shape=[jax.ShapeDtypeStruct((t, d), F32),
                   jax.ShapeDtypeStruct((t, d), BF16),
                   jax.ShapeDtypeStruct((t, ne), F32)],
        compiler_params=_params(("parallel",), 56),
        name="out_proj_norm_router",
    )(ya, yb, w_bf16, w_bf16, x2d, g.reshape(1, d), r_split)


def _route_kernel(lg_ref, slot_ref, aff_ref, *, cap):
    nb, ne, seq = aff_ref.shape
    logits = lg_ref[...].reshape(nb, ne, seq)
    mx = jnp.max(logits, axis=1, keepdims=True)
    ex = jnp.exp(logits - mx)
    a3 = ex / jnp.sum(ex, axis=1, keepdims=True)
    aff_ref[...] = a3
    a = a3.reshape(nb * ne, seq)
    nrows = nb * ne
    lanes = 128
    r = lax.broadcasted_iota(jnp.int32, (lanes, lanes), 0)
    c = lax.broadcasted_iota(jnp.int32, (lanes, lanes), 1)
    tri = jnp.where(r <= c, 1.0, 0.0).astype(BF16)

    def count(mask_f):
        return jnp.sum(mask_f, axis=1, keepdims=True)

    def prefix(x):
        parts = []
        run = jnp.zeros((nrows, 1), F32)
        for j in range(seq // lanes):
            blk = x[:, j * lanes:(j + 1) * lanes]
            inc = jnp.dot(blk.astype(BF16), tri, preferred_element_type=F32)
            parts.append(inc - blk + run)
            run = run + count(blk)
        return jnp.concatenate(parts, axis=1)

    def body(i, ans):
        cand = ans | jnp.left_shift(jnp.int32(1), 30 - i)
        cnt = count(jnp.where(a >= lax.bitcast_convert_type(cand, F32), 1.0, 0.0))
        return jnp.where(cnt >= cap, cand, ans)

    ans = lax.fori_loop(0, 31, body, jnp.zeros((nrows, 1), jnp.int32))
    thr = lax.bitcast_convert_type(ans, F32)
    gt = jnp.where(a > thr, 1.0, 0.0)
    eq = jnp.where(a >= thr, 1.0, 0.0) - gt
    need = cap - count(gt)
    sel = gt + eq * jnp.where(prefix(eq) < need, 1.0, 0.0)
    slot_ref[...] = jnp.where(sel > 0.5, prefix(sel), -1.0).astype(jnp.int32).reshape(nb, ne, seq)


def route(logits_rows, bsz, ne, seq, cap):
    return pl.pallas_call(
        functools.partial(_route_kernel, cap=cap),
        grid=(1,),
        in_specs=[pl.BlockSpec((bsz * ne, seq), lambda i: (0, 0))],
        out_specs=[pl.BlockSpec((bsz, ne, seq), lambda i: (0, 0, 0)),
                   pl.BlockSpec((bsz, ne, seq), lambda i: (0, 0, 0))],
        out_shape=[jax.ShapeDtypeStruct((bsz, ne, seq), jnp.int32),
                   jax.ShapeDtypeStruct((bsz, ne, seq), F32)],
        compiler_params=_params(("arbitrary",), 40),
        name="route",
    )(logits_rows)


def _gather_kernel(slot_ref, aff_ref, h_ref, xe_ref, gate_ref, *, cap, ncol):
    eg, seq = slot_ref.shape
    d = h_ref.shape[1]
    cidx = lax.broadcasted_iota(jnp.int32, (cap, seq), 0)
    rows = []
    for j in range(eg):
        hit = cidx == slot_ref[j:j + 1, :]
        rows.append(jnp.where(hit, 1.0, 0.0).astype(BF16))
        gate_ref[j] = jnp.sum(jnp.where(hit, aff_ref[j:j + 1, :], 0.0), axis=1, keepdims=True)
    p = jnp.concatenate(rows, axis=0)
    dc = d // ncol
    for c in range(ncol):
        xe = jnp.dot(p, h_ref[:, c * dc:(c + 1) * dc], preferred_element_type=F32)
        xe_ref[:, :, c * dc:(c + 1) * dc] = xe.reshape(eg, cap, dc).astype(xe_ref.dtype)


def gather_rows(slot, aff, h, cap, egroup=4, ncol=2):
    bsz, ne, seq = slot.shape
    d = h.shape[1]
    ngroups = ne // egroup
    grouped = lambda a: a.reshape(bsz, ngroups, egroup, seq)
    return pl.pallas_call(
        functools.partial(_gather_kernel, cap=cap, ncol=ncol),
        grid=(bsz, ngroups),
        in_specs=[pl.BlockSpec((None, None, egroup, seq), lambda b, g: (b, g, 0, 0)),
                  pl.BlockSpec((None, None, egroup, seq), lambda b, g: (b, g, 0, 0)),
                  pl.BlockSpec((seq, d), lambda b, g: (b, 0))],
        out_specs=[pl.BlockSpec((egroup, None, cap, d), lambda b, g: (g, b, 0, 0)),
                   pl.BlockSpec((egroup, None, cap, 1), lambda b, g: (g, b, 0, 0))],
        out_shape=[jax.ShapeDtypeStruct((ne, bsz, cap, d), BF16),
                   jax.ShapeDtypeStruct((ne, bsz, cap, 1), F32)],
        compiler_params=_params(("parallel", "parallel"), 48),
        name="moe_gather",
    )(grouped(slot), grouped(aff), h)


def _combine_kernel(slot_t_ref, ye_ref, x_ref, o_ref, *, cap):
    tb, ne = slot_t_ref.shape
    tn = x_ref.shape[1]
    cidx = lax.broadcasted_iota(jnp.int32, (tb, cap), 1)
    pt = jnp.concatenate([jnp.where(cidx == slot_t_ref[:, e:e + 1], 1.0, 0.0).astype(BF16) for e in range(ne)],
                         axis=1)
    ye = ye_ref[...].reshape(ne * cap, tn)
    o_ref[...] = x_ref[...] + jnp.dot(pt, ye, preferred_element_type=F32)


def combine(slot_t, ye, x2d, cap, tn=1024, tb=1024):
    bsz, seq, ne = slot_t.shape
    d = x2d.shape[1]
    nt = seq // tb
    return pl.pallas_call(
        functools.partial(_combine_kernel, cap=cap),
        grid=(bsz, d // tn, nt),
        in_specs=[pl.BlockSpec((None, tb, ne), lambda b, n, t: (b, t, 0)),
                  pl.BlockSpec((ne, None, cap, tn), lambda b, n, t: (0, b, 0, n)),
                  pl.BlockSpec((tb, tn), lambda b, n, t: (b * nt + t, n))],
        out_specs=pl.BlockSpec((tb, tn), lambda b, n, t: (b * nt + t, n)),
        out_shape=jax.ShapeDtypeStruct(x2d.shape, F32),
        compiler_params=_params(("parallel", "parallel", "parallel"), 48),
        name="moe_combine",
    )(slot_t, ye, x2d)


def _combine_norm_kernel(slot_t_ref, ye_ref, x_ref, g_ref, o_ref, *, cap):
    tb, ne = slot_t_ref.shape
    d = x_ref.shape[1]
    cidx = lax.broadcasted_iota(jnp.int32, (tb, cap), 1)
    pt = jnp.concatenate([jnp.where(cidx == slot_t_ref[:, e:e + 1], 1.0, 0.0).astype(BF16) for e in range(ne)],
                         axis=1)
    acc = jnp.dot(pt, ye_ref[...].reshape(ne * cap, d), preferred_element_type=F32)
    for c in range(tb // EPILOGUE_ROWS):
        rows = slice(c * EPILOGUE_ROWS, (c + 1) * EPILOGUE_ROWS)
        xn = x_ref[rows, :] + acc[rows, :]
        ms = jnp.mean(xn * xn, axis=-1, keepdims=True)
        o_ref[rows, :] = xn * lax.rsqrt(ms + RMS_EPS) * g_ref[...]


def combine_final_norm(slot_t, ye, x2d, g, cap, tb=256):
    bsz, seq, ne = slot_t.shape
    d = x2d.shape[1]
    nt = seq // tb
    return pl.pallas_call(
        functools.partial(_combine_norm_kernel, cap=cap),
        grid=(bsz, nt),
        in_specs=[pl.BlockSpec((None, tb, ne), lambda b, t: (b, t, 0)),
                  pl.BlockSpec((ne, None, cap, d), lambda b, t: (0, b, 0, 0)),
                  pl.BlockSpec((tb, d), lambda b, t: (b * nt + t, 0)),
                  pl.BlockSpec((1, d), lambda b, t: (0, 0))],
        out_specs=pl.BlockSpec((tb, d), lambda b, t: (b * nt + t, 0)),
        out_shape=jax.ShapeDtypeStruct(x2d.shape, F32),
        compiler_params=_params(("parallel", "arbitrary"), 56),
        name="moe_combine_final_norm",
    )(slot_t, ye, x2d, g.reshape(1, d))


def _expert_ffn_kernel(xe_ref, wg_ref, wu_ref, wd_ref, gate_ref, o_ref, acc_ref):
    f = pl.program_id(2)
    last = pl.num_programs(2) - 1
    rows_per = xe_ref.shape[0] // FFN_ROW_CHUNKS

    def step(first, final):
        wg = wg_ref[...].astype(BF16)
        wu = wu_ref[...].astype(BF16)
        wd = wd_ref[...].astype(BF16)
        for c in range(FFN_ROW_CHUNKS):
            rows = slice(c * rows_per, (c + 1) * rows_per)
            xe = xe_ref[rows, :]
            g = jnp.dot(xe, wg, preferred_element_type=F32)
            u = jnp.dot(xe, wu, preferred_element_type=F32)
            hid = (jax.nn.silu(g) * u).astype(BF16)
            part = jnp.dot(hid, wd, preferred_element_type=F32)
            total = part if first else acc_ref[rows, :] + part
            if final:
                o_ref[rows, :] = (total * gate_ref[rows, :]).astype(o_ref.dtype)
            else:
                acc_ref[rows, :] = total

    pl.when(f == 0)(lambda: step(True, False))
    pl.when(jnp.logical_and(f > 0, f < last))(lambda: step(False, False))
    pl.when(f == last)(lambda: step(False, True))


def expert_ffn(xe, w_gate, w_up, w_down, layer, gate, tm=1024, tf=256):
    ne, r, d = xe.shape
    fdim = w_gate.shape[3]
    assert fdim // tf >= 2, "the kernel distinguishes first / middle / last hidden tiles"
    return pl.pallas_call(
        _expert_ffn_kernel,
        grid=(ne, r // tm, fdim // tf),
        in_specs=[pl.BlockSpec((None, tm, d), lambda e, m, f: (e, m, 0)),
                  pl.BlockSpec((None, None, d, tf), lambda e, m, f: (layer, e, 0, f)),
                  pl.BlockSpec((None, None, d, tf), lambda e, m, f: (layer, e, 0, f)),
                  pl.BlockSpec((None, None, tf, d), lambda e, m, f: (layer, e, f, 0)),
                  pl.BlockSpec((None, tm, 1), lambda e, m, f: (e, m, 0))],
        out_specs=pl.BlockSpec((None, tm, d), lambda e, m, f: (e, m, 0)),
        out_shape=jax.ShapeDtypeStruct((ne, r, d), BF16),
        scratch_shapes=[pltpu.VMEM((tm, d), F32)],
        compiler_params=_params(("parallel", "parallel", "arbitrary"), 56),
        name="expert_ffn",
    )(xe, w_gate, w_up, w_down, gate)


def expert_choice_moe(x2d, h, logits, bsz, seq, layer, w_gate, w_up, w_down, final_gain=None):
    d = x2d.shape[1]
    ne = logits.shape[1]
    cap = EC_CAPACITY_FACTOR * seq // ne
    logits_rows = logits.reshape(bsz, seq, ne).transpose(0, 2, 1).reshape(bsz * ne, seq)
    slot, aff = route(logits_rows, bsz, ne, seq, cap)
    xe, gate = gather_rows(slot, aff, h, cap)
    ye = expert_ffn(xe.reshape(ne, bsz * cap, d), w_gate, w_up, w_down, layer,
                    gate.reshape(ne, bsz * cap, 1))
    slot_t, ye = slot.transpose(0, 2, 1), ye.reshape(ne, bsz, cap, d)
    if final_gain is None:
        return combine(slot_t, ye, x2d, cap)
    return combine_final_norm(slot_t, ye, x2d, final_gain, cap)


def kernel(x, norm_mix, norm_ffn, norm_final, w_in_ab, a_v_norm, a_spatial_w, a_spatial_b, b_rpb, w_out_ab,
           w_in_cd, c_conv, w_out_cd, router, w_gate, w_up, w_down):
    bsz, seq, d = x.shape
    depth = norm_mix.shape[0]
    half = d // 2
    nh = half // HEAD_DIM
    x2d = x.reshape(bsz * seq, d)
    for layer in range(depth):
        i = layer // 2
        if layer % 2 == 0:
            p = rms_matmul(x2d, norm_mix[layer], w_in_ab[i])
            p3 = p.reshape(bsz, seq, -1)
            ya = sgu(p, a_v_norm[i], a_spatial_w[i], a_spatial_b[i], half)
            yb = neighbourhood_attention(p3, b_rpb[i], 2 * half // HEAD_DIM, nh).reshape(bsz * seq, half)
            x2d, h, logits = out_proj_norm_router(ya, yb, w_out_ab[i].astype(BF16), x2d,
                                                  norm_ffn[layer], router[layer])
        else:
            p = rms_matmul(x2d, norm_mix[layer], w_in_cd[i])
            p3 = p.reshape(bsz, seq, -1)
            yd = dilated_attention(p3, 3 * half // HEAD_DIM, nh).reshape(bsz * seq, half)
            x2d, h, logits = conv_out_proj_norm_router(p, c_conv[i], yd, w_out_cd[i].astype(BF16), x2d,
                                                       norm_ffn[layer], router[layer], seq)
        x2d = expert_choice_moe(x2d, h, logits, bsz, seq, layer, w_gate, w_up, w_down,
                                final_gain=norm_final if layer == depth - 1 else None)
    return x2d.reshape(bsz, seq, d)
```
